```python
import jax
import jax.numpy as jnp
from jax import lax
import numpy as np

D_MODEL = 1024
BATCH = 8
SEQ = 2048
DEPTH = 2
DEC_BATCH = 128
DEC_SEQ = 4
PAST_LEN = 16384
PAGE_SIZE = 128

RET_HEADS = 4
RET_DK = 128
RET_DV = 256
RET_QK_W = RET_HEADS * RET_DK
RET_V_W = RET_HEADS * RET_DV
RET_CHUNK = 128
ROPE_BASE = 10000.0
RWKV_HEADS = 8
RWKV_N = 64
RWKV_W = RWKV_HEADS * RWKV_N
LORA_W = 64
LORA_A = 64
LORA_G = 128
LORA_V = 32
SHIFT_W = 3 * RWKV_W + LORA_W + LORA_A + LORA_G
N_EXPERTS = 32
TOP_K = 4
D_FF = D_MODEL
SWIGLU_LIMIT = 7.0
SWIGLU_ALPHA = 1.702
DN_ALPHA = (2 * DEPTH) ** 0.25
DN_BETA = (8 * DEPTH) ** -0.25
LN_EPS = 1e-5
RET_GN_EPS = 1e-5
RWKV_GN_EPS = 64e-5
IN_SIZES = (RET_QK_W, RET_QK_W, RET_V_W, RET_V_W, SHIFT_W, D_MODEL, D_MODEL)
IN_W = sum(IN_SIZES)
RWKV_SIZES = (RWKV_W, RWKV_W, RWKV_W, LORA_W, LORA_A, LORA_G)

kernel_name = 'hybrid_retention_rwkv7_moe_step'


def split_cols(z, sizes):
    offs = [int(o) for o in np.cumsum(sizes)[:-1]]
    return jnp.split(z, offs, axis=-1)


def layer_norm(x, g, b):
    xf = x.astype(jnp.float32)
    mu = jnp.mean(xf, -1, keepdims=True)
    var = jnp.mean(jnp.square(xf - mu), -1, keepdims=True)
    return ((xf - mu) * lax.rsqrt(var + LN_EPS) * g + b).astype(x.dtype)


def group_norm(y, g, b, eps):
    mu = jnp.mean(y, -1, keepdims=True)
    var = jnp.mean(jnp.square(y - mu), -1, keepdims=True)
    yn = (y - mu) * lax.rsqrt(var + eps)
    return yn.reshape(y.shape[0], y.shape[1], -1) * g + b


def rope(x, pos):
    half = x.shape[-1] // 2
    inv = ROPE_BASE ** (-jnp.arange(half, dtype=jnp.float32) / half)
    ang = pos[:, None] * inv[None, :]
    cos = jnp.cos(ang)[None, :, None, :]
    sin = jnp.sin(ang)[None, :, None, :]
    x1, x2 = x[..., :half], x[..., half:]
    return jnp.concatenate([x1 * cos - x2 * sin, x1 * sin + x2 * cos], axis=-1)


def retention_chunked(q, k, v, s0):
    B, T = q.shape[0], q.shape[1]
    C = RET_CHUNK if T % RET_CHUNK == 0 else T
    n = T // C
    lg = jnp.log1p(-jnp.exp2(-5.0 - jnp.arange(RET_HEADS, dtype=jnp.float32)))
    i = jnp.arange(C, dtype=jnp.float32)
    diff = i[:, None] - i[None, :]
    dmask = jnp.exp(jnp.where(diff[None] >= 0, diff[None] * lg[:, None, None], -jnp.inf))
    q_dec = jnp.exp((i[:, None] + 1.0) * lg[None, :])
    k_dec = jnp.exp((C - 1.0 - i)[:, None] * lg[None, :])
    c_dec = jnp.exp(C * lg)

    def to_chunks(a):
        return a.reshape(B, n, C, a.shape[2], a.shape[3]).swapaxes(0, 1)

    def step(s, xs):
        qc, kc, vc = xs
        sc = jnp.einsum('bihd,bjhd->bhij', qc, kc) * dmask
        intra = jnp.einsum('bhij,bjhe->bihe', sc, vc)
        cross = jnp.einsum('bihd,bhde->bihe', qc, s) * q_dec[None, :, :, None]
        s_new = s * c_dec[None, :, None, None] + jnp.einsum('bjhd,bjhe->bhde', kc * k_dec[None, :, :, None], vc)
        return s_new, intra + cross

    s_fin, y = lax.scan(step, s0, (to_chunks(q), to_chunks(k), to_chunks(v)))
    y = y.swapaxes(0, 1).reshape(B, T, RET_HEADS, RET_DV)
    return y, s_fin


def rwkv7_scan(r, w, k, v, kk, a, s0):
    def step(S, xs):
        r_t, w_t, k_t, v_t, kk_t, a_t = xs
        sa = jnp.einsum('bhij,bhj->bhi', S, -kk_t)
        S = S * w_t[:, :, None, :] + sa[..., :, None] * (kk_t * a_t)[:, :, None, :] + v_t[..., :, None] * k_t[:, :, None, :]
        y = jnp.einsum('bhij,bhj->bhi', S, r_t)
        return S, y

    xs = tuple(t.swapaxes(0, 1) for t in (r, w, k, v, kk, a))
    S, y = lax.scan(step, s0, xs)
    return y.swapaxes(0, 1), S


def heads(a, h, d):
    return a.reshape(a.shape[0], a.shape[1], h, d)


def hybrid_mixer(x, pos, s_ret, s_rwkv, s_shift, v_first, vres, p):
    f32 = jnp.float32
    B, T, _ = x.shape
    z = (x @ p['w_in']).astype(f32)
    rq, rk, rv, rg, u, ga, gb = split_cols(z, IN_SIZES)
    q = rope(heads(rq, RET_HEADS, RET_DK), pos)
    k = rope(heads(rk, RET_HEADS, RET_DK), pos) * (RET_DK ** -0.5)
    v = heads(rv, RET_HEADS, RET_DV)
    y_ret, s_ret_new = retention_chunked(q, k, v, s_ret.astype(f32))
    y_ret = group_norm(y_ret, p['ret_gn_g'], p['ret_gn_b'], RET_GN_EPS) * jax.nn.silu(rg)
    branch_a = y_ret @ p['w_ret_out']
    u_prev = jnp.concatenate([s_shift[:, None, :].astype(f32), u[:, :-1]], axis=1)
    um = u + (u_prev - u) * p['rw_mu']
    r, kw, vw, wd, ad, gd = split_cols(um, RWKV_SIZES)
    w_log = -jax.nn.softplus(-(p['rw_w0'] + jnp.tanh(wd) @ p['rw_w_up'])) - 0.5
    decay = jnp.exp(-jnp.exp(w_log))
    a = jax.nn.sigmoid(p['rw_a0'] + ad @ p['rw_a_up'])
    g = jax.nn.sigmoid(gd) @ p['rw_g_up']
    if vres is None:
        v_first = vw
    else:
        v0, vd, vu = vres
        vw = vw + (v_first - vw) * jax.nn.sigmoid(v0 + (vw @ vd) @ vu)
    kk = heads(kw * p['rw_k_k'], RWKV_HEADS, RWKV_N)
    kk = kk / jnp.maximum(jnp.sqrt(jnp.sum(jnp.square(kk), -1, keepdims=True)), 1e-12)
    kw = kw * (1.0 + (a - 1.0) * p['rw_k_a'])
    rh = heads(r, RWKV_HEADS, RWKV_N)
    kh = heads(kw, RWKV_HEADS, RWKV_N)
    vh = heads(vw, RWKV_HEADS, RWKV_N)
    y_rw, s_rwkv_new = rwkv7_scan(rh, heads(decay, RWKV_HEADS, RWKV_N), kh, vh, kk,
                                  heads(a, RWKV_HEADS, RWKV_N), s_rwkv.astype(f32))
    bonus = jnp.sum(rh * kh * p['rw_r_k'].reshape(RWKV_HEADS, RWKV_N), -1, keepdims=True) * vh
    y_rw = (group_norm(y_rw, p['rw_gn_g'], p['rw_gn_b'], RWKV_GN_EPS) + bonus.reshape(B, T, RWKV_W)) * g
    branch_b = y_rw @ p['w_rwkv_out']
    merged = jax.nn.sigmoid(ga) * branch_a + jax.nn.sigmoid(gb) * branch_b
    out = (merged @ p['w_o']).astype(x.dtype)
    return out, s_ret_new, s_rwkv_new, u[:, -1], v_first


def moe_ffn(x, p):
    f32 = jnp.float32
    B, T, D = x.shape
    xt = x.reshape(B * T, D)
    logits = (xt @ p['w_router'] + p['b_router']).astype(f32)
    top_v, top_i = lax.top_k(logits, TOP_K)
    probs = jax.nn.softmax(top_v, axis=-1)
    comb = jnp.einsum('nk,nke->en', probs, jax.nn.one_hot(top_i, N_EXPERTS, dtype=f32))

    def expert(acc, xs):
        wgu, bgu, wdn, bdn, c = xs
        h = (xt @ wgu + bgu).astype(f32)
        gate, up = jnp.split(h, 2, axis=-1)
        gate = jnp.minimum(gate, SWIGLU_LIMIT)
        up = jnp.clip(up, -SWIGLU_LIMIT, SWIGLU_LIMIT)
        act = gate * jax.nn.sigmoid(SWIGLU_ALPHA * gate) * (up + 1.0)
        return acc + c[:, None] * (act @ wdn + bdn), None

    acc, _ = lax.scan(expert, jnp.zeros((B * T, D), f32),
                      (p['w_gate_up'], p['b_gate_up'], p['w_down'], p['b_down'], comb))
    return acc.reshape(B, T, D).astype(x.dtype)


def run_trunk(x, pos, s_ret, s_rwkv, s_shift, layer, vres):
    v_first = None
    outs_ret, outs_rwkv, outs_shift = [], [], []
    for l in range(DEPTH):
        p = {name: arr[l] for name, arr in layer.items()}
        vr = None if l == 0 else (vres[0][l - 1], vres[1][l - 1], vres[2][l - 1])
        h, sr, sw, ss, v_first = hybrid_mixer(x, pos, s_ret[l], s_rwkv[l], s_shift[l], v_first, vr, p)
        x = layer_norm(DN_ALPHA * x + h, p['ln1_g'], p['ln1_b'])
        x = layer_norm(DN_ALPHA * x + moe_ffn(x, p), p['ln2_g'], p['ln2_b'])
        outs_ret.append(sr.astype(s_ret.dtype))
        outs_rwkv.append(sw.astype(s_rwkv.dtype))
        outs_shift.append(ss.astype(s_shift.dtype))
    return x, jnp.stack(outs_ret), jnp.stack(outs_rwkv), jnp.stack(outs_shift)


def setup_inputs(seed: int = 0) -> dict:
    key = jax.random.key(seed)
    ks = jax.random.split(key, 64)
    cnt = [0]
    f32 = jnp.float32

    def nxt():
        cnt[0] += 1
        return ks[cnt[0] - 1]

    def nrm(shape, scale):
        return jax.random.normal(nxt(), shape, f32) * scale

    def unif(shape, lo, hi):
        return jax.random.uniform(nxt(), shape, f32, lo, hi)

    col_scale = np.ones((IN_W,), np.float32)
    rv0 = 2 * RET_QK_W
    col_scale[rv0:rv0 + RET_V_W] = DN_BETA
    wv0 = 2 * RET_QK_W + 2 * RET_V_W + 2 * RWKV_W
    col_scale[wv0:wv0 + RWKV_W] = DN_BETA
    L1 = DEPTH - 1
    return {
        'x_prompt': nrm((BATCH, SEQ, D_MODEL), 1.0),
        'x_sample': nrm((DEC_BATCH, DEC_SEQ, D_MODEL), 1.0),
        'state_ret': nrm((DEPTH, DEC_BATCH, RET_HEADS, RET_DK, RET_DV), 0.5),
        'state_rwkv': nrm((DEPTH, DEC_BATCH, RWKV_HEADS, RWKV_N, RWKV_N), 0.2),
        'state_shift': nrm((DEPTH, DEC_BATCH, SHIFT_W), 1.0),
        'w_in': nrm((DEPTH, D_MODEL, IN_W), D_MODEL ** -0.5) * jnp.asarray(col_scale),
        'ret_gn_g': 1.0 + nrm((DEPTH, RET_V_W), 0.02),
        'ret_gn_b': nrm((DEPTH, RET_V_W), 0.02),
        'w_ret_out': nrm((DEPTH, RET_V_W, D_MODEL), RET_V_W ** -0.5),
        'rw_mu': unif((DEPTH, SHIFT_W), 0.0, 1.0),
        'rw_w0': unif((DEPTH, RWKV_W), -5.0, -0.5),
        'rw_w_up': nrm((DEPTH, LORA_W, RWKV_W), 0.1),
        'rw_a0': nrm((DEPTH, RWKV_W), 0.1),
        'rw_a_up': nrm((DEPTH, LORA_A, RWKV_W), 0.1),
        'rw_g_up': nrm((DEPTH, LORA_G, RWKV_W), LORA_G ** -0.5),
        'rw_k_k': 0.85 + nrm((DEPTH, RWKV_W), 0.02),
        'rw_k_a': 1.0 + nrm((DEPTH, RWKV_W), 0.02),
        'rw_r_k': nrm((DEPTH, RWKV_W), 0.1),
        'rw_gn_g': 1.0 + nrm((DEPTH, RWKV_W), 0.02),
        'rw_gn_b': nrm((DEPTH, RWKV_W), 0.02),
        'rw_v0': nrm((L1, RWKV_W), 0.1),
        'rw_vres_down': nrm((L1, RWKV_W, LORA_V), RWKV_W ** -0.5),
        'rw_vres_up': nrm((L1, LORA_V, RWKV_W), 0.1),
        'w_rwkv_out': nrm((DEPTH, RWKV_W, D_MODEL), RWKV_W ** -0.5),
        'w_o': nrm((DEPTH, D_MODEL, D_MODEL), DN_BETA * D_MODEL ** -0.5),
        'ln1_g': 1.0 + nrm((DEPTH, D_MODEL), 0.02),
        'ln1_b': nrm((DEPTH, D_MODEL), 0.02),
        'w_router': nrm((DEPTH, D_MODEL, N_EXPERTS), D_MODEL ** -0.5),
        'b_router': nrm((DEPTH, N_EXPERTS), 0.01),
        'w_gate_up': nrm((DEPTH, N_EXPERTS, D_MODEL, 2 * D_FF), D_MODEL ** -0.5),
        'b_gate_up': nrm((DEPTH, N_EXPERTS, 2 * D_FF), 0.01),
        'w_down': nrm((DEPTH, N_EXPERTS, D_FF, D_MODEL), DN_BETA * D_FF ** -0.5),
        'b_down': nrm((DEPTH, N_EXPERTS, D_MODEL), 0.01),
        'ln2_g': 1.0 + nrm((DEPTH, D_MODEL), 0.02),
        'ln2_b': nrm((DEPTH, D_MODEL), 0.02),
    }


def reference(x_prompt, x_sample, state_ret, state_rwkv, state_shift, w_in, ret_gn_g, ret_gn_b, w_ret_out,
              rw_mu, rw_w0, rw_w_up, rw_a0, rw_a_up, rw_g_up, rw_k_k, rw_k_a, rw_r_k, rw_gn_g, rw_gn_b,
              rw_v0, rw_vres_down, rw_vres_up, w_rwkv_out, w_o, ln1_g, ln1_b, w_router, b_router,
              w_gate_up, b_gate_up, w_down, b_down, ln2_g, ln2_b):
    layer = {
        'w_in': w_in, 'ret_gn_g': ret_gn_g, 'ret_gn_b': ret_gn_b, 'w_ret_out': w_ret_out,
        'rw_mu': rw_mu, 'rw_w0': rw_w0, 'rw_w_up': rw_w_up, 'rw_a0': rw_a0, 'rw_a_up': rw_a_up,
        'rw_g_up': rw_g_up, 'rw_k_k': rw_k_k, 'rw_k_a': rw_k_a, 'rw_r_k': rw_r_k,
        'rw_gn_g': rw_gn_g, 'rw_gn_b': rw_gn_b, 'w_rwkv_out': w_rwkv_out, 'w_o': w_o,
        'ln1_g': ln1_g, 'ln1_b': ln1_b, 'w_router': w_router, 'b_router': b_router,
        'w_gate_up': w_gate_up, 'b_gate_up': b_gate_up, 'w_down': w_down, 'b_down': b_down,
        'ln2_g': ln2_g, 'ln2_b': ln2_b,
    }
    vres = (rw_v0, rw_vres_down, rw_vres_up)
    bp, tp = x_prompt.shape[0], x_prompt.shape[1]
    ts = x_sample.shape[1]
    zr = jnp.zeros((DEPTH, bp, RET_HEADS, RET_DK, RET_DV), state_ret.dtype)
    zw = jnp.zeros((DEPTH, bp, RWKV_HEADS, RWKV_N, RWKV_N), state_rwkv.dtype)
    zs = jnp.zeros((DEPTH, bp, SHIFT_W), state_shift.dtype)
    pos_p = jnp.arange(tp, dtype=jnp.float32)
    pos_s = PAST_LEN + jnp.arange(ts, dtype=jnp.float32)
    y_prompt, ret_p, rwkv_p, shift_p = run_trunk(x_prompt, pos_p, zr, zw, zs, layer, vres)
    y_sample, ret_s, rwkv_s, shift_s = run_trunk(x_sample, pos_s, state_ret, state_rwkv, state_shift, layer, vres)
    return (y_prompt, y_sample, ret_p, rwkv_p, shift_p, ret_s, rwkv_s, shift_s)
```

```python
import functools

import jax
import jax.numpy as jnp
from jax import lax
from jax.experimental import pallas as pl
from jax.experimental.pallas import tpu as pltpu

F32 = jnp.float32
BF16 = jnp.bfloat16
HI = lax.Precision.HIGHEST

D_MODEL = 1024
DEPTH = 2
RET_HEADS = 4
RET_DK = 128
RET_DV = 256
RET_QK_W = RET_HEADS * RET_DK
RET_V_W = RET_HEADS * RET_DV
RET_W = 2 * RET_QK_W + 2 * RET_V_W
RET_CHUNK = 128
ROPE_BASE = 10000.0
RWKV_HEADS = 8
RWKV_N = 64
RWKV_W = RWKV_HEADS * RWKV_N
LORA_W = 64
LORA_A = 64
LORA_G = 128
SHIFT_W = 3 * RWKV_W + LORA_W + LORA_A + LORA_G
RWKV_CHUNK = 64
N_EXPERTS = 32
TOP_K = 4
D_FF = D_MODEL
SWIGLU_LIMIT = 7.0
SWIGLU_ALPHA = 1.702
DN_ALPHA = (2 * DEPTH) ** 0.25
LN_EPS = 1e-5
RET_GN_EPS = 1e-5
RWKV_GN_EPS = 64e-5

VMEM_LIMIT = 48 * 1024 * 1024
TOKEN_TILE = 512
MOE_TILE = 768
FF_TILE = 512


def _params(*sem):
    return pltpu.CompilerParams(dimension_semantics=sem, vmem_limit_bytes=VMEM_LIMIT)


def _dot(a, b):
    return jnp.dot(a.astype(BF16), b.astype(BF16), preferred_element_type=F32)


def _dot_hi(a, b):
    return jnp.dot(a, b, precision=HI, preferred_element_type=F32)


def _dot_nt_hi(a, b):
    return lax.dot_general(a, b, (((1,), (1,)), ((), ())), precision=HI, preferred_element_type=F32)


def _dot_tn_hi(a, b):
    return lax.dot_general(a, b, (((0,), (0,)), ((), ())), precision=HI, preferred_element_type=F32)


def _sigmoid(x):
    return 1.0 / (1.0 + jnp.exp(-x))


def _layer_norm(x, g, b):
    mu = jnp.mean(x, -1, keepdims=True)
    d = x - mu
    var = jnp.mean(d * d, -1, keepdims=True)
    return d * lax.rsqrt(var + LN_EPS) * g + b


def _matmul_kernel(x_ref, w_ref, o_ref):
    o_ref[...] = _dot(x_ref[...], w_ref[...])


def _matmul(x, w, tm):
    m, k = x.shape
    n = w.shape[1]
    return pl.pallas_call(
        _matmul_kernel,
        grid=(m // tm,),
        in_specs=[pl.BlockSpec((tm, k), lambda i: (i, 0)), pl.BlockSpec((k, n), lambda i: (0, 0))],
        out_specs=pl.BlockSpec((tm, n), lambda i: (i, 0)),
        out_shape=jax.ShapeDtypeStruct((m, n), F32),
        compiler_params=_params("parallel"),
        name="in_proj",
    )(x, w)


def _ret_kernel(has_state, q_ref, k_ref, v_ref, g_ref, cos_ref, sin_ref, dm_ref, qd_ref, kd_ref, cd_ref,
                gng_ref, gnb_ref, *rest):
    if has_state:
        s0_ref, y_ref, so_ref, s_scr = rest
    else:
        y_ref, so_ref, s_scr = rest
    c = pl.program_id(2)

    @pl.when(c == 0)
    def _():
        if has_state:
            s_scr[...] = s0_ref[:, 0]
        else:
            s_scr[...] = jnp.zeros_like(s_scr)

    bb, cl, _ = q_ref.shape
    cos = cos_ref[...]
    sin = sin_ref[...]

    def rope(x):
        x2 = x.reshape(bb * cl, RET_DK)
        rot = pltpu.roll(x2, RET_DK // 2, axis=1).reshape(bb, cl, RET_DK)
        return x * cos + rot * sin

    q = rope(q_ref[...])
    k = rope(k_ref[...]) * (RET_DK ** -0.5)
    v = v_ref[...].astype(BF16)
    s = s_scr[...]
    qb = q.astype(BF16)
    sc = jnp.einsum('bid,bjd->bij', qb, k.astype(BF16), preferred_element_type=F32) * dm_ref[0]
    intra = jnp.einsum('bij,bje->bie', sc.astype(BF16), v, preferred_element_type=F32)
    cross = jnp.einsum('bid,bde->bie', qb, s.astype(BF16), preferred_element_type=F32) * qd_ref[0]
    kd = (k * kd_ref[0]).astype(BF16)
    s_new = s * cd_ref[0] + jnp.einsum('bjd,bje->bde', kd, v, preferred_element_type=F32)
    s_scr[...] = s_new

    y = intra + cross
    mu = jnp.mean(y, -1, keepdims=True)
    d = y - mu
    var = jnp.mean(d * d, -1, keepdims=True)
    yn = d * lax.rsqrt(var + RET_GN_EPS) * gng_ref[...] + gnb_ref[...]
    rg = g_ref[...]
    y_ref[...] = yn * (rg * _sigmoid(rg))

    @pl.when(c == pl.num_programs(2) - 1)
    def _():
        so_ref[:, 0] = s_new


def _retention(z, pos, t_real, s0, gn_g, gn_b, bb):
    b, tp, _ = z.shape
    cl = RET_CHUNK if t_real % RET_CHUNK == 0 else tp
    cr = min(cl, t_real)
    nc = tp // cl
    half = RET_DK // 2
    inv = ROPE_BASE ** (-jnp.arange(half, dtype=F32) / half)
    ang = pos[:, None] * inv[None, :]
    cos = jnp.concatenate([jnp.cos(ang), jnp.cos(ang)], -1)
    sin = jnp.concatenate([-jnp.sin(ang), jnp.sin(ang)], -1)
    lg = jnp.log1p(-jnp.exp2(-5.0 - jnp.arange(RET_HEADS, dtype=F32)))
    i = jnp.arange(cl, dtype=F32)
    real = i < cr
    diff = i[:, None] - i[None, :]
    ok = (diff >= 0) & real[:, None] & real[None, :]
    dmask = jnp.exp(jnp.where(ok[None], diff[None] * lg[:, None, None], -jnp.inf))
    q_dec = jnp.exp((i[None, :] + 1.0) * lg[:, None])[..., None]
    k_dec = jnp.where(real[None, :], jnp.exp((cr - 1.0 - i)[None, :] * lg[:, None]), 0.0)[..., None]
    c_dec = jnp.exp(cr * lg)[:, None, None]

    has_state = s0 is not None
    qk_spec = lambda off: pl.BlockSpec((bb, cl, RET_DK), lambda bi, h, c: (bi, c, off + h))
    v_spec = lambda off: pl.BlockSpec((bb, cl, RET_DV), lambda bi, h, c: (bi, c, off + h))
    head_spec = lambda shape: pl.BlockSpec((1,) + shape, lambda bi, h, c: (h, 0, 0))
    s_spec = pl.BlockSpec((bb, 1, RET_DK, RET_DV), lambda bi, h, c: (bi, h, 0, 0))
    in_specs = [
        qk_spec(0), qk_spec(RET_HEADS), v_spec(RET_HEADS), v_spec(2 * RET_HEADS),
        pl.BlockSpec((cl, RET_DK), lambda bi, h, c: (c, 0)), pl.BlockSpec((cl, RET_DK), lambda bi, h, c: (c, 0)),
        head_spec((cl, cl)), head_spec((cl, 1)), head_spec((cl, 1)), head_spec((1, 1)),
        pl.BlockSpec((1, RET_DV), lambda bi, h, c: (0, h)), pl.BlockSpec((1, RET_DV), lambda bi, h, c: (0, h)),
    ]
    args = [z, z, z, z, cos, sin, dmask, q_dec, k_dec, c_dec, gn_g.reshape(1, -1), gn_b.reshape(1, -1)]
    if has_state:
        in_specs.append(s_spec)
        args.append(s0)
    return pl.pallas_call(
        functools.partial(_ret_kernel, has_state),
        grid=(b // bb, RET_HEADS, nc),
        in_specs=in_specs,
        out_specs=[pl.BlockSpec((bb, cl, RET_DV), lambda bi, h, c: (bi, c, h)), s_spec],
        out_shape=[jax.ShapeDtypeStruct((b, tp, RET_V_W), F32),
                   jax.ShapeDtypeStruct((b, RET_HEADS, RET_DK, RET_DV), F32)],
        scratch_shapes=[pltpu.VMEM((bb, RET_DK, RET_DV), F32)],
        compiler_params=_params("parallel", "parallel", "arbitrary"),
        name="retention",
    )(*args)


def _head_sum(x, bd):
    return _dot_hi(x, bd)


def _rwkv_pre_kernel(has_vres, u_ref, up_ref, mu_ref, w0_ref, wup_ref, a0_ref, aup_ref, gup_ref, kk_ref, ka_ref,
                     bd_ref, *rest):
    if has_vres:
        vf_ref, v0_ref, vd_ref, vu_ref, r_o, lw_o, k_o, v_o, a_o, b_o, g_o = rest
    else:
        r_o, lw_o, k_o, v_o, a_o, b_o, g_o = rest
    u = u_ref[...]
    um = u + (up_ref[...] - u) * mu_ref[...]
    w1, w2, w3 = RWKV_W, 2 * RWKV_W, 3 * RWKV_W
    r = um[:, :w1]
    kw = um[:, w1:w2]
    vw = um[:, w2:w3]
    wd = um[:, w3:w3 + LORA_W]
    ad = um[:, w3 + LORA_W:w3 + LORA_W + LORA_A]
    gd = um[:, w3 + LORA_W + LORA_A:]
    xw = w0_ref[...] + _dot_hi(jnp.tanh(wd), wup_ref[...])
    softplus = jnp.maximum(-xw, 0.0) + jnp.log1p(jnp.exp(-jnp.abs(xw)))
    lw_o[...] = -jnp.exp(-softplus - 0.5)
    a = _sigmoid(a0_ref[...] + _dot_hi(ad, aup_ref[...]))
    g_o[...] = _dot_hi(_sigmoid(gd), gup_ref[...])
    if has_vres:
        gate = _sigmoid(v0_ref[...] + _dot_hi(_dot_hi(vw, vd_ref[...]), vu_ref[...]))
        vw = vw + (vf_ref[...] - vw) * gate
    kk = kw * kk_ref[...]
    norm = jnp.sqrt(_head_sum(kk * kk, bd_ref[...]))
    kk = kk / jnp.maximum(norm, 1e-12)
    r_o[...] = r
    k_o[...] = kw * (1.0 + (a - 1.0) * ka_ref[...])
    v_o[...] = vw
    a_o[...] = -kk
    b_o[...] = kk * a


def _head_blockdiag():
    h = jnp.arange(RWKV_W) // RWKV_N
    return (h[:, None] == h[None, :]).astype(F32)


def _rwkv_pre(u, u_prev, p, l, v_first, vres):
    n = u.shape[0]
    tm = TOKEN_TILE
    row = lambda a: a.reshape(1, -1)
    tile = lambda w: pl.BlockSpec((tm, w), lambda i: (i, 0))
    full = lambda a: pl.BlockSpec(a.shape, lambda i: (0,) * a.ndim)
    has_vres = vres is not None
    args = [u, u_prev, row(p['rw_mu'][l]), row(p['rw_w0'][l]), p['rw_w_up'][l], row(p['rw_a0'][l]), p['rw_a_up'][l],
            p['rw_g_up'][l], row(p['rw_k_k'][l]), row(p['rw_k_a'][l]), _head_blockdiag()]
    in_specs = [tile(SHIFT_W), tile(SHIFT_W)] + [full(a) for a in args[2:]]
    if has_vres:
        extra = [v_first, row(vres[0]), vres[1], vres[2]]
        in_specs += [tile(RWKV_W)] + [full(a) for a in extra[1:]]
        args += extra
    return pl.pallas_call(
        functools.partial(_rwkv_pre_kernel, has_vres),
        grid=(n // tm,),
        in_specs=in_specs,
        out_specs=[tile(RWKV_W)] * 7,
        out_shape=[jax.ShapeDtypeStruct((n, RWKV_W), F32)] * 7,
        compiler_params=_params("parallel"),
        name="rwkv_pre",
    )(*args)


def _rwkv_chunk_kernel(has_state, r_ref, lw_ref, k_ref, v_ref, a_ref, b_ref, tri_ref, *rest):
    if has_state:
        s0_ref, y_ref, so_ref, s_scr = rest
    else:
        y_ref, so_ref, s_scr = rest
    c = pl.program_id(1)

    @pl.when(c == 0)
    def _():
        if has_state:
            s_scr[...] = s0_ref[0]
        else:
            s_scr[...] = jnp.zeros_like(s_scr)

    cl = RWKV_CHUNK
    lw = lw_ref[0]
    cum = _dot_hi(tri_ref[...], lw)
    last = cum[cl - 1:cl, :]
    e_pos = jnp.exp(cum)
    e_neg = jnp.exp(-cum)
    e_end = jnp.exp(last - cum)
    at_all = a_ref[0] * jnp.exp(cum - lw)
    rt_all = r_ref[0] * e_pos
    bt_all = b_ref[0] * e_neg
    kt_all = k_ref[0] * e_neg
    bw_all = b_ref[0] * e_end
    kw_all = k_ref[0] * e_end
    wc_all = jnp.exp(last)
    v_all = v_ref[0]

    ti = lax.broadcasted_iota(jnp.int32, (cl, cl), 0)
    si = lax.broadcasted_iota(jnp.int32, (cl, cl), 1)
    strict = (ti > si).astype(F32)
    incl = (ti >= si).astype(F32)
    eye = (ti == si).astype(F32)

    for h in range(RWKV_HEADS):
        sl = slice(h * RWKV_N, (h + 1) * RWKV_N)
        at, rt, bt, kt, vh = at_all[:, sl], rt_all[:, sl], bt_all[:, sl], kt_all[:, sl], v_all[:, sl]
        a_ab = _dot_nt_hi(at, bt) * strict
        a_ak = _dot_nt_hi(at, kt) * strict
        a_rb = _dot_nt_hi(rt, bt) * incl
        a_rk = _dot_nt_hi(rt, kt) * incl
        inv = eye + a_ab
        pw = a_ab
        for _ in range(cl.bit_length() - 2):
            pw = _dot_hi(pw, pw)
            inv = inv + _dot_hi(inv, pw)
        s0 = s_scr[h]
        u = _dot_hi(inv, _dot_nt_hi(at, s0) + _dot_hi(a_ak, vh))
        y = _dot_nt_hi(rt, s0) + _dot_hi(a_rb, u) + _dot_hi(a_rk, vh)
        s_new = s0 * wc_all[:, sl] + _dot_tn_hi(u, bw_all[:, sl]) + _dot_tn_hi(vh, kw_all[:, sl])
        s_scr[h] = s_new
        y_ref[0, :, sl] = y

    @pl.when(c == pl.num_programs(1) - 1)
    def _():
        so_ref[0] = s_scr[...]


def _rwkv_chunks(r, lw, k, v, a, b, s0):
    bsz, tp, _ = r.shape
    cl = RWKV_CHUNK
    has_state = s0 is not None
    seq = pl.BlockSpec((1, cl, RWKV_W), lambda bi, c: (bi, c, 0))
    s_spec = pl.BlockSpec((1, RWKV_HEADS, RWKV_N, RWKV_N), lambda bi, c: (bi, 0, 0, 0))
    tri = (jnp.arange(cl)[:, None] >= jnp.arange(cl)[None, :]).astype(F32)
    in_specs = [seq] * 6 + [pl.BlockSpec((cl, cl), lambda bi, c: (0, 0))]
    args = [r, lw, k, v, a, b, tri]
    if has_state:
        in_specs.append(s_spec)
        args.append(s0)
    return pl.pallas_call(
        functools.partial(_rwkv_chunk_kernel, has_state),
        grid=(bsz, tp // cl),
        in_specs=in_specs,
        out_specs=[seq, s_spec],
        out_shape=[jax.ShapeDtypeStruct((bsz, tp, RWKV_W), F32),
                   jax.ShapeDtypeStruct((bsz, RWKV_HEADS, RWKV_N, RWKV_N), F32)],
        scratch_shapes=[pltpu.VMEM((RWKV_HEADS, RWKV_N, RWKV_N), F32)],
        compiler_params=_params("parallel", "arbitrary"),
        name="rwkv_chunks",
    )(*args)


def _post_kernel(x_ref, yrw_ref, r_ref, k_ref, v_ref, g_ref, yret_ref, ga_ref, gb_ref, bd_ref, rk_ref, gng_ref,
                 gnb_ref, wret_ref, wrw_ref, wo_ref, ln_g_ref, ln_b_ref, wr_ref, br_ref, x1_ref, comb_ref):
    bd = bd_ref[...]
    y = yrw_ref[...]
    mu = _head_sum(y, bd) * (1.0 / RWKV_N)
    d = y - mu
    var = _head_sum(d * d, bd) * (1.0 / RWKV_N)
    yn = d * lax.rsqrt(var + RWKV_GN_EPS) * gng_ref[...] + gnb_ref[...]
    v = v_ref[...]
    bonus = _head_sum(r_ref[...] * k_ref[...] * rk_ref[...], bd) * v
    yb = (yn + bonus) * g_ref[...]
    branch_a = _dot(yret_ref[...], wret_ref[...])
    branch_b = _dot(yb, wrw_ref[...])
    merged = _sigmoid(ga_ref[...]) * branch_a + _sigmoid(gb_ref[...]) * branch_b
    out = _dot(merged, wo_ref[...])
    x1 = _layer_norm(DN_ALPHA * x_ref[...] + out, ln_g_ref[...], ln_b_ref[...])
    x1_ref[...] = x1

    logits = _dot_hi(x1, wr_ref[...]) + br_ref[...]
    lane = lax.broadcasted_iota(jnp.int32, logits.shape, 1)
    work = logits
    sels, vals = [], []
    for _ in range(TOP_K):
        m = jnp.max(work, -1, keepdims=True)
        idx = jnp.min(jnp.where(work == m, lane, N_EXPERTS), -1, keepdims=True)
        sel = lane == idx
        sels.append(sel)
        vals.append(m)
        work = jnp.where(sel, -jnp.inf, work)
    exps = [jnp.exp(m - vals[0]) for m in vals]
    inv_den = 1.0 / sum(exps)
    comb = jnp.zeros_like(logits)
    for sel, e in zip(sels, exps):
        comb = comb + jnp.where(sel, e * inv_den, 0.0)
    comb_ref[...] = comb


def _post(x, yrw, r, k, v, g, yret, gates, p, l, wret, wrw, wo):
    n = x.shape[0]
    tm = TOKEN_TILE
    row = lambda a: a.reshape(1, -1)
    tile = lambda w, j=0: pl.BlockSpec((tm, w), lambda i: (i, j))
    full = lambda a: pl.BlockSpec(a.shape, lambda i: (0,) * a.ndim)
    consts = [_head_blockdiag(), row(p['rw_r_k'][l]), row(p['rw_gn_g'][l]), row(p['rw_gn_b'][l]), wret, wrw, wo,
              row(p['ln1_g'][l]), row(p['ln1_b'][l]), p['w_router'][l], row(p['b_router'][l])]
    in_specs = ([tile(D_MODEL)] + [tile(RWKV_W)] * 5 + [tile(RET_V_W), tile(D_MODEL, 0), tile(D_MODEL, 1)]
                + [full(a) for a in consts])
    return pl.pallas_call(
        _post_kernel,
        grid=(n // tm,),
        in_specs=in_specs,
        out_specs=[tile(D_MODEL), tile(N_EXPERTS)],
        out_shape=[jax.ShapeDtypeStruct((n, D_MODEL), F32), jax.ShapeDtypeStruct((n, N_EXPERTS), F32)],
        compiler_params=_params("parallel"),
        name="merge_ln_router",
    )(x, yrw, r, k, v, g, yret, gates, gates, *consts)


def _moe_kernel(x_ref, comb_ref, wg_ref, wu_ref, bg_ref, bu_ref, wd_ref, bd_ref, ln_g_ref, ln_b_ref, o_ref,
                acc_ref, xb_ref):
    e = pl.program_id(1)
    f = pl.program_id(2)

    @pl.when((e == 0) & (f == 0))
    def _():
        acc_ref[...] = jnp.zeros_like(acc_ref)
        xb_ref[...] = x_ref[...].astype(BF16)

    comb = comb_ref[...]
    lane = lax.broadcasted_iota(jnp.int32, comb.shape, 1)
    c = jnp.sum(jnp.where(lane == e, comb, 0.0), -1, keepdims=True)
    xb = xb_ref[...]
    gate = jnp.dot(xb, wg_ref[0, 0].astype(BF16), preferred_element_type=F32) + bg_ref[0, 0]
    up = jnp.dot(xb, wu_ref[0, 0].astype(BF16), preferred_element_type=F32) + bu_ref[0, 0]
    gate = jnp.minimum(gate, SWIGLU_LIMIT)
    up = jnp.clip(up, -SWIGLU_LIMIT, SWIGLU_LIMIT)
    act = gate * _sigmoid(SWIGLU_ALPHA * gate) * (up + 1.0)
    acc_ref[...] += c * _dot(act, wd_ref[0, 0])

    @pl.when(f == 0)
    def _():
        acc_ref[...] += c * bd_ref[0, 0]

    @pl.when((e == pl.num_programs(1) - 1) & (f == pl.num_programs(2) - 1))
    def _():
        o_ref[...] = _layer_norm(DN_ALPHA * x_ref[...] + acc_ref[...], ln_g_ref[...], ln_b_ref[...])


def _moe(x1, comb, p, l):
    n = x1.shape[0]
    tm = MOE_TILE
    nf = D_FF // FF_TILE
    row = lambda a: a.reshape(1, -1)
    bgu = p['b_gate_up'].reshape(DEPTH, N_EXPERTS, 1, 2 * D_FF)
    bdn = p['b_down'].reshape(DEPTH, N_EXPERTS, 1, D_MODEL)
    in_specs = [
        pl.BlockSpec((tm, D_MODEL), lambda i, e, f: (i, 0)),
        pl.BlockSpec((tm, N_EXPERTS), lambda i, e, f: (i, 0)),
        pl.BlockSpec((1, 1, D_MODEL, FF_TILE), lambda i, e, f: (l, e, 0, f)),
        pl.BlockSpec((1, 1, D_MODEL, FF_TILE), lambda i, e, f: (l, e, 0, nf + f)),
        pl.BlockSpec((1, 1, 1, FF_TILE), lambda i, e, f: (l, e, 0, f)),
        pl.BlockSpec((1, 1, 1, FF_TILE), lambda i, e, f: (l, e, 0, nf + f)),
        pl.BlockSpec((1, 1, FF_TILE, D_MODEL), lambda i, e, f: (l, e, f, 0)),
        pl.BlockSpec((1, 1, 1, D_MODEL), lambda i, e, f: (l, e, 0, 0)),
        pl.BlockSpec((1, D_MODEL), lambda i, e, f: (0, 0)),
        pl.BlockSpec((1, D_MODEL), lambda i, e, f: (0, 0)),
    ]
    return pl.pallas_call(
        _moe_kernel,
        grid=(n // tm, N_EXPERTS, nf),
        in_specs=in_specs,
        out_specs=pl.BlockSpec((tm, D_MODEL), lambda i, e, f: (i, 0)),
        out_shape=jax.ShapeDtypeStruct((n, D_MODEL), F32),
        scratch_shapes=[pltpu.VMEM((tm, D_MODEL), F32), pltpu.VMEM((tm, D_MODEL), BF16)],
        compiler_params=_params("parallel", "arbitrary", "arbitrary"),
        name="moe_ln",
    )(x1, comb, p['w_gate_up'], p['w_gate_up'], bgu, bgu, p['w_down'], bdn, row(p['ln2_g'][l]), row(p['ln2_b'][l]))


def _pad_time(a, tp):
    return jnp.pad(a, ((0, 0), (0, tp - a.shape[1]), (0, 0)))


def kernel(x_prompt, x_sample, state_ret, state_rwkv, state_shift, w_in, ret_gn_g, ret_gn_b, w_ret_out, rw_mu, rw_w0, rw_w_up, rw_a0, rw_a_up, rw_g_up, rw_k_k, rw_k_a, rw_r_k, rw_gn_g, rw_gn_b, rw_v0, rw_vres_down, rw_vres_up, w_rwkv_out, w_o, ln1_g, ln1_b, w_router, b_router, w_gate_up, b_gate_up, w_down, b_down, ln2_g, ln2_b):
    p = dict(rw_mu=rw_mu, rw_w0=rw_w0, rw_w_up=rw_w_up, rw_a0=rw_a0, rw_a_up=rw_a_up, rw_g_up=rw_g_up,
             rw_k_k=rw_k_k, rw_k_a=rw_k_a, rw_r_k=rw_r_k, rw_gn_g=rw_gn_g, rw_gn_b=rw_gn_b, ln1_g=ln1_g, ln1_b=ln1_b,
             w_router=w_router, b_router=b_router, w_gate_up=w_gate_up, b_gate_up=b_gate_up, w_down=w_down,
             b_down=b_down, ln2_g=ln2_g, ln2_b=ln2_b)
    bp, tp, _ = x_prompt.shape
    bs, ts, _ = x_sample.shape
    np_, ns = bp * tp, bs * ts
    past_len = 16384
    pos_p = jnp.arange(tp, dtype=F32)
    ts_ret = 8
    pos_s = past_len + jnp.arange(ts_ret, dtype=F32)
    ts_rw = RWKV_CHUNK

    x = jnp.concatenate([x_prompt.reshape(np_, D_MODEL), x_sample.reshape(ns, D_MODEL)], 0)
    outs = {k: [] for k in ('ret_p', 'rw_p', 'sh_p', 'ret_s', 'rw_s', 'sh_s')}
    v_first = None
    u_off = RET_W
    g_off = RET_W + SHIFT_W
    for l in range(DEPTH):
        w_in_b = w_in[l].astype(BF16)
        z_ret = _matmul(x, w_in_b[:, :u_off], TOKEN_TILE)
        u = _matmul(x, w_in_b[:, u_off:g_off], TOKEN_TILE)
        gates = _matmul(x, w_in_b[:, g_off:], TOKEN_TILE)

        zr_p = z_ret[:np_].reshape(bp, tp, RET_W)
        zr_s = _pad_time(z_ret[np_:].reshape(bs, ts, RET_W), ts_ret)
        yret_p, sret_p = _retention(zr_p, pos_p, tp, None, ret_gn_g[l], ret_gn_b[l], 1)
        yret_s, sret_s = _retention(zr_s, pos_s, ts, state_ret[l], ret_gn_g[l], ret_gn_b[l], 8)
        yret = jnp.concatenate([yret_p.reshape(np_, RET_V_W), yret_s[:, :ts].reshape(ns, RET_V_W)], 0)

        u_p = u[:np_].reshape(bp, tp, SHIFT_W)
        u_s = u[np_:].reshape(bs, ts, SHIFT_W)
        prev_p = jnp.concatenate([jnp.zeros((bp, 1, SHIFT_W), F32), u_p[:, :-1]], 1)
        prev_s = jnp.concatenate([state_shift[l][:, None, :], u_s[:, :-1]], 1)
        u_prev = jnp.concatenate([prev_p.reshape(np_, SHIFT_W), prev_s.reshape(ns, SHIFT_W)], 0)
        vres = None if l == 0 else (rw_v0[l - 1], rw_vres_down[l - 1], rw_vres_up[l - 1])
        r, lw, k, v, a, b, g = _rwkv_pre(u, u_prev, p, l, v_first, vres)
        if l == 0:
            v_first = v
        seqs = (r, lw, k, v, a, b)
        yrw_p, srw_p = _rwkv_chunks(*[t[:np_].reshape(bp, tp, RWKV_W) for t in seqs], None)
        yrw_s, srw_s = _rwkv_chunks(*[_pad_time(t[np_:].reshape(bs, ts, RWKV_W), ts_rw) for t in seqs],
                                    state_rwkv[l])
        yrw = jnp.concatenate([yrw_p.reshape(np_, RWKV_W), yrw_s[:, :ts].reshape(ns, RWKV_W)], 0)

        x1, comb = _post(x, yrw, r, k, v, g, yret, gates, p, l, w_ret_out[l].astype(BF16),
                         w_rwkv_out[l].astype(BF16), w_o[l].astype(BF16))
        x = _moe(x1, comb, p, l)

        outs['ret_p'].append(sret_p)
        outs['ret_s'].append(sret_s)
        outs['rw_p'].append(srw_p)
        outs['rw_s'].append(srw_s)
        outs['sh_p'].append(u_p[:, -1])
        outs['sh_s'].append(u_s[:, -1])

    y_prompt = x[:np_].reshape(bp, tp, D_MODEL)
    y_sample = x[np_:].reshape(bs, ts, D_MODEL)
    st = {k: jnp.stack(v) for k, v in outs.items()}
    return (y_prompt, y_sample, st['ret_p'], st['rw_p'], st['sh_p'], st['ret_s'], st['rw_s'], st['sh_s'])
```

```python
import functools

import jax
import jax.numpy as jnp
from jax import lax
from jax.experimental import pallas as pl
from jax.experimental.pallas import tpu as pltpu

F32 = jnp.float32
BF16 = jnp.bfloat16
HI = lax.Precision.HIGHEST

D_MODEL = 1024
DEPTH = 2
PAST_LEN = 16384
RET_HEADS = 4
RET_DK = 128
RET_DV = 256
RET_QK_W = RET_HEADS * RET_DK
RET_V_W = RET_HEADS * RET_DV
RET_W = 2 * RET_QK_W + 2 * RET_V_W
RET_CHUNK = 128
ROPE_BASE = 10000.0
RWKV_HEADS = 8
RWKV_N = 64
RWKV_W = RWKV_HEADS * RWKV_N
LORA_W = 64
LORA_A = 64
LORA_G = 128
SHIFT_W = 3 * RWKV_W + LORA_W + LORA_A + LORA_G
RWKV_CHUNK = 64
N_EXPERTS = 32
TOP_K = 4
D_FF = D_MODEL
SWIGLU_LIMIT = 7.0
SWIGLU_ALPHA = 1.702
DN_ALPHA = (2 * DEPTH) ** 0.25
LN_EPS = 1e-5
RET_GN_EPS = 1e-5
RWKV_GN_EPS = 64e-5

VMEM_LIMIT = 56 * 1024 * 1024
TOKEN_TILE = 512
EXPERT_ROW_TILE = 512
COMBINE_TILE = 256
RWKV_BATCH_BLOCK = 2
RWKV_STEP_BATCH_BLOCK = 8
EXACT_PASSES = 3


def _params(*sem):
    return pltpu.CompilerParams(dimension_semantics=sem, vmem_limit_bytes=VMEM_LIMIT)


def _split(x):
    hi = x.astype(BF16)
    lo = (x - hi.astype(F32)).astype(BF16)
    return hi, lo


def _split_weight(w, passes):
    hi = w.astype(BF16)
    if passes == 1:
        return (hi,)
    return (hi, (w - hi.astype(F32)).astype(BF16))


def _mm(a, b, spec, passes):
    dg = lambda x, y: jnp.einsum(spec, x, y, preferred_element_type=F32)
    if passes == 1:
        return dg(a.astype(BF16), b.astype(BF16))
    ah, al = _split(a)
    bh, bl = _split(b)
    return dg(ah, bh) + (dg(ah, bl) + dg(al, bh))


def _mm_w(a, w_refs):
    dg = lambda x, y: jnp.dot(x, y, preferred_element_type=F32)
    if len(w_refs) == 1:
        return dg(a.astype(BF16), w_refs[0][...])
    ah, al = _split(a)
    return dg(ah, w_refs[0][...]) + (dg(ah, w_refs[1][...]) + dg(al, w_refs[0][...]))


def _dot_hi(a, b):
    return jnp.dot(a, b, precision=HI, preferred_element_type=F32)


def _sigmoid(x):
    return 1.0 / (1.0 + jnp.exp(-x))


def _layer_norm(x, g, b):
    mu = jnp.mean(x, -1, keepdims=True)
    d = x - mu
    var = jnp.mean(d * d, -1, keepdims=True)
    return d * lax.rsqrt(var + LN_EPS) * g + b


def _matmul_kernel(x_ref, *refs):
    o_ref = refs[-1]
    o_ref[...] = _mm_w(x_ref[...], refs[:-1])


def _matmul(x, w, tm, tn, passes):
    m, k = x.shape
    n = w.shape[1]
    ws = _split_weight(w, passes)
    return pl.pallas_call(
        _matmul_kernel,
        grid=(n // tn, m // tm),
        in_specs=[pl.BlockSpec((tm, k), lambda j, i: (i, 0))] + [pl.BlockSpec((k, tn), lambda j, i: (0, j))] * len(ws),
        out_specs=pl.BlockSpec((tm, tn), lambda j, i: (i, j)),
        out_shape=jax.ShapeDtypeStruct((m, n), F32),
        compiler_params=_params("parallel", "parallel"),
        name="in_proj",
    )(x, *ws)


def _ret_kernel(has_state, passes, q_ref, k_ref, v_ref, g_ref, cos_ref, sin_ref, dm_ref, qd_ref, kd_ref, cd_ref,
                gng_ref, gnb_ref, *rest):
    if has_state:
        s0_ref, y_ref, so_ref, s_scr = rest
    else:
        y_ref, so_ref, s_scr = rest
    c = pl.program_id(2)

    @pl.when(c == 0)
    def _():
        if has_state:
            s_scr[...] = s0_ref[:, 0]
        else:
            s_scr[...] = jnp.zeros_like(s_scr)

    bb, cl, _ = q_ref.shape
    cos = cos_ref[...]
    sin = sin_ref[...]
    mm = functools.partial(_mm, passes=passes)

    def rope(x):
        x2 = x.reshape(bb * cl, RET_DK)
        rot = pltpu.roll(x2, RET_DK // 2, axis=1).reshape(bb, cl, RET_DK)
        return x * cos + rot * sin

    q = rope(q_ref[...])
    k = rope(k_ref[...]) * (RET_DK ** -0.5)
    v = v_ref[...]
    s = s_scr[...]
    sc = mm(q, k, 'bid,bjd->bij') * dm_ref[0]
    intra = mm(sc, v, 'bij,bje->bie')
    cross = mm(q, s, 'bid,bde->bie') * qd_ref[0]
    s_new = s * cd_ref[0] + mm(k * kd_ref[0], v, 'bjd,bje->bde')
    s_scr[...] = s_new

    y = intra + cross
    mu = jnp.mean(y, -1, keepdims=True)
    d = y - mu
    var = jnp.mean(d * d, -1, keepdims=True)
    yn = d * lax.rsqrt(var + RET_GN_EPS) * gng_ref[...] + gnb_ref[...]
    rg = g_ref[...]
    y_ref[...] = yn * (rg * _sigmoid(rg))

    @pl.when(c == pl.num_programs(2) - 1)
    def _():
        so_ref[:, 0] = s_new


def _retention(z, pos, t_real, s0, gn_g, gn_b, bb, passes):
    b, tp, _ = z.shape
    cl = RET_CHUNK if t_real % RET_CHUNK == 0 else tp
    cr = min(cl, t_real)
    nc = tp // cl
    half = RET_DK // 2
    inv = ROPE_BASE ** (-jnp.arange(half, dtype=F32) / half)
    ang = pos[:, None] * inv[None, :]
    cos = jnp.concatenate([jnp.cos(ang), jnp.cos(ang)], -1)
    sin = jnp.concatenate([-jnp.sin(ang), jnp.sin(ang)], -1)
    lg = jnp.log1p(-jnp.exp2(-5.0 - jnp.arange(RET_HEADS, dtype=F32)))
    i = jnp.arange(cl, dtype=F32)
    real = i < cr
    diff = i[:, None] - i[None, :]
    ok = (diff >= 0) & real[:, None] & real[None, :]
    dmask = jnp.exp(jnp.where(ok[None], diff[None] * lg[:, None, None], -jnp.inf))
    q_dec = jnp.exp((i[None, :] + 1.0) * lg[:, None])[..., None]
    k_dec = jnp.where(real[None, :], jnp.exp((cr - 1.0 - i)[None, :] * lg[:, None]), 0.0)[..., None]
    c_dec = jnp.exp(cr * lg)[:, None, None]

    has_state = s0 is not None
    qk_spec = lambda off: pl.BlockSpec((bb, cl, RET_DK), lambda bi, h, c: (bi, c, off + h))
    v_spec = lambda off: pl.BlockSpec((bb, cl, RET_DV), lambda bi, h, c: (bi, c, off + h))
    head_spec = lambda shape: pl.BlockSpec((1,) + shape, lambda bi, h, c: (h, 0, 0))
    s_spec = pl.BlockSpec((bb, 1, RET_DK, RET_DV), lambda bi, h, c: (bi, h, 0, 0))
    in_specs = [
        qk_spec(0), qk_spec(RET_HEADS), v_spec(RET_HEADS), v_spec(2 * RET_HEADS),
        pl.BlockSpec((cl, RET_DK), lambda bi, h, c: (c, 0)), pl.BlockSpec((cl, RET_DK), lambda bi, h, c: (c, 0)),
        head_spec((cl, cl)), head_spec((cl, 1)), head_spec((cl, 1)), head_spec((1, 1)),
        pl.BlockSpec((1, RET_DV), lambda bi, h, c: (0, h)), pl.BlockSpec((1, RET_DV), lambda bi, h, c: (0, h)),
    ]
    args = [z, z, z, z, cos, sin, dmask, q_dec, k_dec, c_dec, gn_g.reshape(1, -1), gn_b.reshape(1, -1)]
    if has_state:
        in_specs.append(s_spec)
        args.append(s0)
    return pl.pallas_call(
        functools.partial(_ret_kernel, has_state, passes),
        grid=(b // bb, RET_HEADS, nc),
        in_specs=in_specs,
        out_specs=[pl.BlockSpec((bb, cl, RET_DV), lambda bi, h, c: (bi, c, h)), s_spec],
        out_shape=[jax.ShapeDtypeStruct((b, tp, RET_V_W), F32),
                   jax.ShapeDtypeStruct((b, RET_HEADS, RET_DK, RET_DV), F32)],
        scratch_shapes=[pltpu.VMEM((bb, RET_DK, RET_DV), F32)],
        compiler_params=_params("parallel", "parallel", "arbitrary"),
        name="retention",
    )(*args)


def _head_sum(x, bd):
    return _dot_hi(x, bd)


def _rwkv_pre_kernel(has_vres, u_ref, up_ref, mu_ref, w0_ref, wup_ref, a0_ref, aup_ref, gup_ref, kk_ref, ka_ref,
                     bd_ref, *rest):
    if has_vres:
        vf_ref, v0_ref, vd_ref, vu_ref, r_o, lw_o, k_o, v_o, a_o, b_o, g_o = rest
    else:
        r_o, lw_o, k_o, v_o, a_o, b_o, g_o = rest
    u = u_ref[...]
    um = u + (up_ref[...] - u) * mu_ref[...]
    w1, w2, w3 = RWKV_W, 2 * RWKV_W, 3 * RWKV_W
    r = um[:, :w1]
    kw = um[:, w1:w2]
    vw = um[:, w2:w3]
    wd = um[:, w3:w3 + LORA_W]
    ad = um[:, w3 + LORA_W:w3 + LORA_W + LORA_A]
    gd = um[:, w3 + LORA_W + LORA_A:]
    xw = w0_ref[...] + _dot_hi(jnp.tanh(wd), wup_ref[...])
    softplus = jnp.maximum(-xw, 0.0) + jnp.log1p(jnp.exp(-jnp.abs(xw)))
    lw_o[...] = -jnp.exp(-softplus - 0.5)
    a = _sigmoid(a0_ref[...] + _dot_hi(ad, aup_ref[...]))
    g_o[...] = _dot_hi(_sigmoid(gd), gup_ref[...])
    if has_vres:
        gate = _sigmoid(v0_ref[...] + _dot_hi(_dot_hi(vw, vd_ref[...]), vu_ref[...]))
        vw = vw + (vf_ref[...] - vw) * gate
    kk = kw * kk_ref[...]
    norm = jnp.sqrt(_head_sum(kk * kk, bd_ref[...]))
    kk = kk / jnp.maximum(norm, 1e-12)
    r_o[...] = r
    k_o[...] = kw * (1.0 + (a - 1.0) * ka_ref[...])
    v_o[...] = vw
    a_o[...] = -kk
    b_o[...] = kk * a


def _head_blockdiag():
    h = jnp.arange(RWKV_W) // RWKV_N
    return (h[:, None] == h[None, :]).astype(F32)


def _rwkv_pre(u, u_prev, p, l, v_first, vres):
    n = u.shape[0]
    tm = TOKEN_TILE
    row = lambda a: a.reshape(1, -1)
    tile = lambda w: pl.BlockSpec((tm, w), lambda i: (i, 0))
    full = lambda a: pl.BlockSpec(a.shape, lambda i: (0,) * a.ndim)
    has_vres = vres is not None
    args = [u, u_prev, row(p['rw_mu'][l]), row(p['rw_w0'][l]), p['rw_w_up'][l], row(p['rw_a0'][l]), p['rw_a_up'][l],
            p['rw_g_up'][l], row(p['rw_k_k'][l]), row(p['rw_k_a'][l]), _head_blockdiag()]
    in_specs = [tile(SHIFT_W), tile(SHIFT_W)] + [full(a) for a in args[2:]]
    if has_vres:
        extra = [v_first, row(vres[0]), vres[1], vres[2]]
        in_specs += [tile(RWKV_W)] + [full(a) for a in extra[1:]]
        args += extra
    return pl.pallas_call(
        functools.partial(_rwkv_pre_kernel, has_vres),
        grid=(n // tm,),
        in_specs=in_specs,
        out_specs=[tile(RWKV_W)] * 7,
        out_shape=[jax.ShapeDtypeStruct((n, RWKV_W), F32)] * 7,
        compiler_params=_params("parallel"),
        name="rwkv_pre",
    )(*args)


def _rwkv_chunk_kernel(passes, r_ref, lw_ref, k_ref, v_ref, a_ref, b_ref, tri_ref, y_ref, so_ref, s_scr):
    c = pl.program_id(1)

    @pl.when(c == 0)
    def _():
        s_scr[...] = jnp.zeros_like(s_scr)

    bb, cl, _ = r_ref.shape
    mm = functools.partial(_mm, passes=passes)
    ti = lax.broadcasted_iota(jnp.int32, (cl, cl), 0)
    si = lax.broadcasted_iota(jnp.int32, (cl, cl), 1)
    strict = (ti > si).astype(F32)
    incl = (ti >= si).astype(F32)
    eye = (ti == si).astype(F32)

    def heads(x):
        return jnp.stack([x[bi][:, h * RWKV_N:(h + 1) * RWKV_N] for bi in range(bb) for h in range(RWKV_HEADS)])

    lw = lw_ref[...]
    cum = jnp.stack([_dot_hi(tri_ref[...], lw[bi]) for bi in range(bb)])
    last = cum[:, cl - 1:cl, :]
    e_neg = jnp.exp(-cum)
    e_end = jnp.exp(last - cum)
    at = heads(a_ref[...] * jnp.exp(cum - lw))
    rt = heads(r_ref[...] * jnp.exp(cum))
    bt = heads(b_ref[...] * e_neg)
    kt = heads(k_ref[...] * e_neg)
    bw = heads(b_ref[...] * e_end)
    kw = heads(k_ref[...] * e_end)
    wc = heads(jnp.exp(last))
    vh = heads(v_ref[...])

    lhs = jnp.concatenate([at, rt], 1)
    gram = mm(lhs, jnp.concatenate([bt, kt], 1), 'gik,gjk->gij')
    a_ab = gram[:, :cl, :cl] * strict
    a_ak = gram[:, :cl, cl:] * strict
    a_rb = gram[:, cl:, :cl] * incl
    a_rk = gram[:, cl:, cl:] * incl
    inv = eye + a_ab
    pw = a_ab
    for _ in range(cl.bit_length() - 2):
        pw = mm(pw, pw, 'gij,gjk->gik')
        inv = inv + mm(inv, pw, 'gij,gjk->gik')
    s0 = s_scr[...].reshape(bb * RWKV_HEADS, RWKV_N, RWKV_N)
    xs = mm(lhs, s0, 'gtj,gij->gti')
    av = mm(jnp.concatenate([a_ak, a_rk], 1), vh, 'gts,gsi->gti')
    u = mm(inv, xs[:, :cl] + av[:, :cl], 'gts,gsi->gti')
    y = xs[:, cl:] + av[:, cl:] + mm(a_rb, u, 'gts,gsi->gti')
    s_new = s0 * wc + mm(jnp.concatenate([u, vh], 1), jnp.concatenate([bw, kw], 1), 'gti,gtj->gij')
    s_scr[...] = s_new.reshape(bb, RWKV_HEADS, RWKV_N, RWKV_N)
    for bi in range(bb):
        for h in range(RWKV_HEADS):
            y_ref[bi, :, h * RWKV_N:(h + 1) * RWKV_N] = y[bi * RWKV_HEADS + h]

    @pl.when(c == pl.num_programs(1) - 1)
    def _():
        so_ref[...] = s_scr[...]


def _rwkv_chunks(r, lw, k, v, a, b, passes):
    bsz, tp, _ = r.shape
    cl = RWKV_CHUNK
    bb = RWKV_BATCH_BLOCK
    seq = pl.BlockSpec((bb, cl, RWKV_W), lambda bi, c: (bi, c, 0))
    s_spec = pl.BlockSpec((bb, RWKV_HEADS, RWKV_N, RWKV_N), lambda bi, c: (bi, 0, 0, 0))
    tri = (jnp.arange(cl)[:, None] >= jnp.arange(cl)[None, :]).astype(F32)
    return pl.pallas_call(
        functools.partial(_rwkv_chunk_kernel, passes),
        grid=(bsz // bb, tp // cl),
        in_specs=[seq] * 6 + [pl.BlockSpec((cl, cl), lambda bi, c: (0, 0))],
        out_specs=[seq, s_spec],
        out_shape=[jax.ShapeDtypeStruct((bsz, tp, RWKV_W), F32),
                   jax.ShapeDtypeStruct((bsz, RWKV_HEADS, RWKV_N, RWKV_N), F32)],
        scratch_shapes=[pltpu.VMEM((bb, RWKV_HEADS, RWKV_N, RWKV_N), F32)],
        compiler_params=_params("parallel", "arbitrary"),
        name="rwkv_chunks",
    )(r, lw, k, v, a, b, tri)


def _rwkv_step_kernel(r_ref, lw_ref, k_ref, a_ref, b_ref, vt_ref, s0_ref, yt_ref, so_ref):
    s = s0_ref[...]
    for t in range(r_ref.shape[2]):
        row = lambda ref: ref[:, :, t:t + 1, :]
        sa = jnp.sum(s * row(a_ref), -1, keepdims=True)
        s = s * jnp.exp(row(lw_ref)) + sa * row(b_ref) + vt_ref[:, :, :, t:t + 1] * row(k_ref)
        yt_ref[:, :, :, t:t + 1] = jnp.sum(s * row(r_ref), -1, keepdims=True)
    so_ref[...] = s


def _rwkv_steps(r, lw, k, v, a, b, s0):
    bsz, t, _ = r.shape
    bb = RWKV_STEP_BATCH_BLOCK
    rows = lambda x: x.reshape(bsz, t, RWKV_HEADS, RWKV_N).transpose(0, 2, 1, 3)
    vt = v.reshape(bsz, t, RWKV_HEADS, RWKV_N).transpose(0, 2, 3, 1)
    row_spec = pl.BlockSpec((bb, RWKV_HEADS, t, RWKV_N), lambda i: (i, 0, 0, 0))
    col_spec = pl.BlockSpec((bb, RWKV_HEADS, RWKV_N, t), lambda i: (i, 0, 0, 0))
    s_spec = pl.BlockSpec((bb, RWKV_HEADS, RWKV_N, RWKV_N), lambda i: (i, 0, 0, 0))
    yt, s_new = pl.pallas_call(
        _rwkv_step_kernel,
        grid=(bsz // bb,),
        in_specs=[row_spec] * 5 + [col_spec, s_spec],
        out_specs=[col_spec, s_spec],
        out_shape=[jax.ShapeDtypeStruct((bsz, RWKV_HEADS, RWKV_N, t), F32),
                   jax.ShapeDtypeStruct((bsz, RWKV_HEADS, RWKV_N, RWKV_N), F32)],
        compiler_params=_params("parallel"),
        name="rwkv_steps",
    )(rows(r), rows(lw), rows(k), rows(a), rows(b), vt, s0)
    return yt.transpose(0, 3, 1, 2).reshape(bsz, t, RWKV_W), s_new


def _post_kernel(n_w, x_ref, yrw_ref, r_ref, k_ref, v_ref, g_ref, yret_ref, ga_ref, gb_ref, bd_ref, rk_ref, gng_ref,
                 gnb_ref, ln_g_ref, ln_b_ref, wr_ref, br_ref, *rest):
    wret, wrw, wo = rest[:n_w], rest[n_w:2 * n_w], rest[2 * n_w:3 * n_w]
    x1_ref, xb_ref, ids_ref, probs_ref = rest[3 * n_w:]
    bd = bd_ref[...]
    y = yrw_ref[...]
    mu = _head_sum(y, bd) * (1.0 / RWKV_N)
    d = y - mu
    var = _head_sum(d * d, bd) * (1.0 / RWKV_N)
    yn = d * lax.rsqrt(var + RWKV_GN_EPS) * gng_ref[...] + gnb_ref[...]
    bonus = _head_sum(r_ref[...] * k_ref[...] * rk_ref[...], bd) * v_ref[...]
    yb = (yn + bonus) * g_ref[...]
    merged = _sigmoid(ga_ref[...]) * _mm_w(yret_ref[...], wret) + _sigmoid(gb_ref[...]) * _mm_w(yb, wrw)
    out = _mm_w(merged, wo)
    x1 = _layer_norm(DN_ALPHA * x_ref[...] + out, ln_g_ref[...], ln_b_ref[...])
    x1_ref[...] = x1
    xb_ref[...] = x1.astype(BF16)

    logits = _dot_hi(x1, wr_ref[...]) + br_ref[...]
    lane = lax.broadcasted_iota(jnp.int32, logits.shape, 1)
    work = logits
    ids, vals = [], []
    for _ in range(TOP_K):
        m = jnp.max(work, -1, keepdims=True)
        idx = jnp.min(jnp.where(work == m, lane, N_EXPERTS), -1, keepdims=True)
        ids.append(idx)
        vals.append(m)
        work = jnp.where(lane == idx, -jnp.inf, work)
    exps = [jnp.exp(m - vals[0]) for m in vals]
    inv_den = 1.0 / sum(exps)
    slot = lax.broadcasted_iota(jnp.int32, ids_ref.shape, 1)
    ids_out = jnp.zeros(ids_ref.shape, jnp.int32)
    probs_out = jnp.zeros(probs_ref.shape, F32)
    for j in range(TOP_K):
        ids_out = jnp.where(slot == j, ids[j], ids_out)
        probs_out = jnp.where(slot == j, exps[j] * inv_den, probs_out)
    ids_ref[...] = ids_out
    probs_ref[...] = probs_out


def _post(x, yrw, r, k, v, g, yret, gates, p, l, passes):
    n = x.shape[0]
    tm = TOKEN_TILE
    row = lambda a: a.reshape(1, -1)
    tile = lambda w, j=0: pl.BlockSpec((tm, w), lambda i: (i, j))
    full = lambda a: pl.BlockSpec(a.shape, lambda i: (0,) * a.ndim)
    weights = (_split_weight(p['w_ret_out'][l], passes) + _split_weight(p['w_rwkv_out'][l], passes)
               + _split_weight(p['w_o'][l], passes))
    consts = [_head_blockdiag(), row(p['rw_r_k'][l]), row(p['rw_gn_g'][l]), row(p['rw_gn_b'][l]),
              row(p['ln1_g'][l]), row(p['ln1_b'][l]), p['w_router'][l], row(p['b_router'][l]), *weights]
    in_specs = ([tile(D_MODEL)] + [tile(RWKV_W)] * 5 + [tile(RET_V_W), tile(D_MODEL, 0), tile(D_MODEL, 1)]
                + [full(a) for a in consts])
    return pl.pallas_call(
        functools.partial(_post_kernel, len(weights) // 3),
        grid=(n // tm,),
        in_specs=in_specs,
        out_specs=[tile(D_MODEL), tile(D_MODEL), tile(TOP_K), tile(TOP_K)],
        out_shape=[jax.ShapeDtypeStruct((n, D_MODEL), F32), jax.ShapeDtypeStruct((n, D_MODEL), BF16),
                   jax.ShapeDtypeStruct((n, TOP_K), jnp.int32), jax.ShapeDtypeStruct((n, TOP_K), F32)],
        compiler_params=_params("parallel"),
        name="merge_ln_router",
    )(x, yrw, r, k, v, g, yret, gates, gates, *consts)


def _expert_kernel(tile_ref, exp_ref, first_ref, valid_ref, lo_ref, hi_ref, x_ref, p_ref, wgu_ref, bgu_ref, wd_ref,
                   bd_ref, o_ref, wgu_s, wd_s):
    i = pl.program_id(0)
    tm = x_ref.shape[0]

    @pl.when(valid_ref[i] == 1)
    def _():
        @pl.when((i == 0) | (exp_ref[jnp.maximum(i - 1, 0)] != exp_ref[i]))
        def _():
            wgu_s[...] = wgu_ref[0, 0].astype(BF16)
            wd_s[...] = wd_ref[0, 0].astype(BF16)

        h = jnp.dot(x_ref[...], wgu_s[...], preferred_element_type=F32) + bgu_ref[0, 0]
        gate = jnp.minimum(h[:, :D_FF], SWIGLU_LIMIT)
        up = jnp.clip(h[:, D_FF:], -SWIGLU_LIMIT, SWIGLU_LIMIT)
        act = gate * _sigmoid(SWIGLU_ALPHA * gate) * (up + 1.0)
        y = (jnp.dot(act.astype(BF16), wd_s[...], preferred_element_type=F32) + bd_ref[0, 0]) * p_ref[...]
        rows = tile_ref[i] * tm + lax.broadcasted_iota(jnp.int32, (tm, 1), 0)
        mine = (rows >= lo_ref[i]) & (rows < hi_ref[i])

        @pl.when(first_ref[i] == 1)
        def _():
            o_ref[...] = jnp.where(mine, y, 0.0)

        @pl.when(first_ref[i] == 0)
        def _():
            o_ref[...] = jnp.where(mine, y, o_ref[...])


def _expert_schedule(counts, n_rows, tm):
    n_tiles = n_rows // tm
    n_entries = n_tiles + N_EXPERTS - 1
    end = jnp.cumsum(counts)
    off = end - counts
    first_tile = off // tm
    n_t = jnp.where(counts > 0, (end - 1) // tm - first_tile + 1, 0)
    entry_end = jnp.cumsum(n_t)
    entry_off = entry_end - n_t
    total = entry_end[-1]
    i = jnp.minimum(jnp.arange(n_entries), total - 1)
    e = jnp.sum(entry_end[None, :] <= i[:, None], -1).astype(jnp.int32)
    t = (first_tile[e] + i - entry_off[e]).astype(jnp.int32)
    valid = (jnp.arange(n_entries) < total).astype(jnp.int32)
    first = jnp.concatenate([jnp.ones((1,), jnp.int32), (t[1:] != t[:-1]).astype(jnp.int32)])
    return t, e, first, valid, off[e].astype(jnp.int32), end[e].astype(jnp.int32)


def _experts(xs, ps, sched, p, l):
    n_rows = xs.shape[0]
    tm = EXPERT_ROW_TILE
    bgu = p['b_gate_up'].reshape(DEPTH, N_EXPERTS, 1, 2 * D_FF)
    bdn = p['b_down'].reshape(DEPTH, N_EXPERTS, 1, D_MODEL)
    by_tile = lambda w: pl.BlockSpec((tm, w), lambda i, t, e, *_: (t[i], 0))
    by_expert = lambda a, b: pl.BlockSpec((1, 1, a, b), lambda i, t, e, *_: (l, e[i], 0, 0))
    grid_spec = pltpu.PrefetchScalarGridSpec(
        num_scalar_prefetch=6,
        grid=(sched[0].shape[0],),
        in_specs=[by_tile(D_MODEL), by_tile(1), by_expert(D_MODEL, 2 * D_FF), by_expert(1, 2 * D_FF),
                  by_expert(D_FF, D_MODEL), by_expert(1, D_MODEL)],
        out_specs=by_tile(D_MODEL),
        scratch_shapes=[pltpu.VMEM((D_MODEL, 2 * D_FF), BF16), pltpu.VMEM((D_FF, D_MODEL), BF16)],
    )
    return pl.pallas_call(
        _expert_kernel,
        grid_spec=grid_spec,
        out_shape=jax.ShapeDtypeStruct((n_rows, D_MODEL), F32),
        compiler_params=_params("arbitrary"),
        name="experts",
    )(*sched, xs, ps, p['w_gate_up'], bgu, p['w_down'], bdn)


def _combine_kernel(x_ref, y_ref, ln_g_ref, ln_b_ref, o_ref):
    moe = (y_ref[0] + y_ref[1]) + (y_ref[2] + y_ref[3])
    o_ref[...] = _layer_norm(DN_ALPHA * x_ref[...] + moe, ln_g_ref[...], ln_b_ref[...])


def _combine(x1, yk, ln_g, ln_b):
    n = x1.shape[0]
    tm = COMBINE_TILE
    return pl.pallas_call(
        _combine_kernel,
        grid=(n // tm,),
        in_specs=[pl.BlockSpec((tm, D_MODEL), lambda i: (i, 0)), pl.BlockSpec((TOP_K, tm, D_MODEL), lambda i: (0, i, 0)),
                  pl.BlockSpec((1, D_MODEL), lambda i: (0, 0)), pl.BlockSpec((1, D_MODEL), lambda i: (0, 0))],
        out_specs=pl.BlockSpec((tm, D_MODEL), lambda i: (i, 0)),
        out_shape=jax.ShapeDtypeStruct((n, D_MODEL), F32),
        compiler_params=_params("parallel"),
        name="combine_ln",
    )(x1, yk, ln_g.reshape(1, -1), ln_b.reshape(1, -1))


def _moe(x1, xb, ids, probs, p, l):
    n = x1.shape[0]
    flat_e = ids.reshape(-1)
    order = jnp.argsort(flat_e, stable=True).astype(jnp.int32)
    dest = jnp.argsort(order).astype(jnp.int32)
    counts = jnp.sum(flat_e[:, None] == jnp.arange(N_EXPERTS)[None, :], 0).astype(jnp.int32)
    xs = jnp.take(xb, order // TOP_K, axis=0)
    ps = jnp.take(probs.reshape(-1), order)[:, None]
    ys = _experts(xs, ps, _expert_schedule(counts, n * TOP_K, EXPERT_ROW_TILE), p, l)
    yk = jnp.take(ys, dest.reshape(n, TOP_K).T.reshape(-1), axis=0).reshape(TOP_K, n, D_MODEL)
    return _combine(x1, yk, p['ln2_g'][l], p['ln2_b'][l])


def _pad_time(a, tp):
    return jnp.pad(a, ((0, 0), (0, tp - a.shape[1]), (0, 0)))


def kernel(x_prompt, x_sample, state_ret, state_rwkv, state_shift, w_in, ret_gn_g, ret_gn_b, w_ret_out, rw_mu, rw_w0, rw_w_up, rw_a0, rw_a_up, rw_g_up, rw_k_k, rw_k_a, rw_r_k, rw_gn_g, rw_gn_b, rw_v0, rw_vres_down, rw_vres_up, w_rwkv_out, w_o, ln1_g, ln1_b, w_router, b_router, w_gate_up, b_gate_up, w_down, b_down, ln2_g, ln2_b):
    p = dict(w_ret_out=w_ret_out, rw_mu=rw_mu, rw_w0=rw_w0, rw_w_up=rw_w_up, rw_a0=rw_a0, rw_a_up=rw_a_up,
             rw_g_up=rw_g_up, rw_k_k=rw_k_k, rw_k_a=rw_k_a, rw_r_k=rw_r_k, rw_gn_g=rw_gn_g, rw_gn_b=rw_gn_b,
             w_rwkv_out=w_rwkv_out, w_o=w_o, ln1_g=ln1_g, ln1_b=ln1_b, w_router=w_router, b_router=b_router,
             w_gate_up=w_gate_up, b_gate_up=b_gate_up, w_down=w_down, b_down=b_down, ln2_g=ln2_g, ln2_b=ln2_b)
    bp, tp, _ = x_prompt.shape
    bs, ts, _ = x_sample.shape
    np_, ns = bp * tp, bs * ts
    pos_p = jnp.arange(tp, dtype=F32)
    ts_ret = 8
    pos_s = PAST_LEN + jnp.arange(ts_ret, dtype=F32)

    x = jnp.concatenate([x_prompt.reshape(np_, D_MODEL), x_sample.reshape(ns, D_MODEL)], 0)
    outs = {k: [] for k in ('ret_p', 'rw_p', 'sh_p', 'ret_s', 'rw_s', 'sh_s')}
    v_first = None
    u_off = RET_W
    g_off = RET_W + SHIFT_W
    for l in range(DEPTH):
        passes = EXACT_PASSES if l == 0 else 1
        z_ret = _matmul(x, w_in[l][:, :u_off], TOKEN_TILE, RET_W // 2, passes)
        u = _matmul(x, w_in[l][:, u_off:g_off], TOKEN_TILE, SHIFT_W, passes)
        gates = _matmul(x, w_in[l][:, g_off:], TOKEN_TILE, D_MODEL, passes)

        zr_p = z_ret[:np_].reshape(bp, tp, RET_W)
        zr_s = _pad_time(z_ret[np_:].reshape(bs, ts, RET_W), ts_ret)
        yret_p, sret_p = _retention(zr_p, pos_p, tp, None, ret_gn_g[l], ret_gn_b[l], 1, passes)
        yret_s, sret_s = _retention(zr_s, pos_s, ts, state_ret[l], ret_gn_g[l], ret_gn_b[l], 8, passes)
        yret = jnp.concatenate([yret_p.reshape(np_, RET_V_W), yret_s[:, :ts].reshape(ns, RET_V_W)], 0)

        u_p = u[:np_].reshape(bp, tp, SHIFT_W)
        u_s = u[np_:].reshape(bs, ts, SHIFT_W)
        prev_p = jnp.concatenate([jnp.zeros((bp, 1, SHIFT_W), F32), u_p[:, :-1]], 1)
        prev_s = jnp.concatenate([state_shift[l][:, None, :], u_s[:, :-1]], 1)
        u_prev = jnp.concatenate([prev_p.reshape(np_, SHIFT_W), prev_s.reshape(ns, SHIFT_W)], 0)
        vres = None if l == 0 else (rw_v0[l - 1], rw_vres_down[l - 1], rw_vres_up[l - 1])
        r, lw, k, v, a, b, g = _rwkv_pre(u, u_prev, p, l, v_first, vres)
        if l == 0:
            v_first = v
        seqs = (r, lw, k, v, a, b)
        yrw_p, srw_p = _rwkv_chunks(*[t[:np_].reshape(bp, tp, RWKV_W) for t in seqs], passes)
        yrw_s, srw_s = _rwkv_steps(*[t[np_:].reshape(bs, ts, RWKV_W) for t in seqs], state_rwkv[l])
        yrw = jnp.concatenate([yrw_p.reshape(np_, RWKV_W), yrw_s.reshape(ns, RWKV_W)], 0)

        x1, xb, ids, probs = _post(x, yrw, r, k, v, g, yret, gates, p, l, passes)
        x = _moe(x1, xb, ids, probs, p, l)

        outs['ret_p'].append(sret_p)
        outs['ret_s'].append(sret_s)
        outs['rw_p'].append(srw_p)
        outs['rw_s'].append(srw_s)
        outs['sh_p'].append(u_p[:, -1])
        outs['sh_s'].append(u_s[:, -1])

    y_prompt = x[:np_].reshape(bp, tp, D_MODEL)
    y_sample = x[np_:].reshape(bs, ts, D_MODEL)
    st = {k: jnp.stack(v) for k, v in outs.items()}
    return (y_prompt, y_sample, st['ret_p'], st['rw_p'], st['sh_p'], st['ret_s'], st['rw_s'], st['sh_s'])
```

```python
import functools

import jax
import jax.numpy as jnp
from jax import lax
from jax.experimental import pallas as pl
from jax.experimental.pallas import tpu as pltpu

F32 = jnp.float32
BF16 = jnp.bfloat16
HI = lax.Precision.HIGHEST

D_MODEL = 1024
DEPTH = 2
PAST_LEN = 16384
RET_HEADS = 4
RET_DK = 128
RET_DV = 256
RET_QK_W = RET_HEADS * RET_DK
RET_V_W = RET_HEADS * RET_DV
RET_W = 2 * RET_QK_W + 2 * RET_V_W
RET_CHUNK = 128
ROPE_BASE = 10000.0
RWKV_HEADS = 8
RWKV_N = 64
RWKV_W = RWKV_HEADS * RWKV_N
LORA_W = 64
LORA_A = 64
LORA_G = 128
SHIFT_W = 3 * RWKV_W + LORA_W + LORA_A + LORA_G
RWKV_CHUNK = 64
N_EXPERTS = 32
TOP_K = 4
D_FF = D_MODEL
SWIGLU_LIMIT = 7.0
SWIGLU_ALPHA = 1.702
DN_ALPHA = (2 * DEPTH) ** 0.25
LN_EPS = 1e-5
RET_GN_EPS = 1e-5
RWKV_GN_EPS = 64e-5

VMEM_LIMIT = 56 * 1024 * 1024
TOKEN_TILE = 512
EXPERT_ROW_TILE = 512
RWKV_STEP_BATCH_BLOCK = 8
EXACT_PASSES = 3


def _params(*sem):
    return pltpu.CompilerParams(dimension_semantics=sem, vmem_limit_bytes=VMEM_LIMIT)


def _split(x):
    hi = x.astype(BF16)
    lo = (x - hi.astype(F32)).astype(BF16)
    return hi, lo


def _split_kernel(w_ref, hi_ref, lo_ref):
    hi_ref[...], lo_ref[...] = _split(w_ref[...])


def _split_weight(w, passes):
    if passes == 1:
        return (w.astype(BF16),)
    rows = 256
    spec = pl.BlockSpec((rows, w.shape[1]), lambda i: (i, 0))
    return tuple(pl.pallas_call(
        _split_kernel,
        grid=(w.shape[0] // rows,),
        in_specs=[spec],
        out_specs=[spec, spec],
        out_shape=[jax.ShapeDtypeStruct(w.shape, BF16)] * 2,
        compiler_params=_params("parallel"),
        name="split_weight",
    )(w))


def _mm(a, b, spec, passes):
    dg = lambda x, y: jnp.einsum(spec, x, y, preferred_element_type=F32)
    if passes == 1:
        return dg(a.astype(BF16), b.astype(BF16))
    ah, al = _split(a)
    bh, bl = _split(b)
    return dg(ah, bh) + (dg(ah, bl) + dg(al, bh))


def _mm_w(a, w_refs):
    dg = lambda x, y: jnp.dot(x, y, preferred_element_type=F32)
    if len(w_refs) == 1:
        return dg(a.astype(BF16), w_refs[0][...])
    ah, al = _split(a)
    return dg(ah, w_refs[0][...]) + (dg(ah, w_refs[1][...]) + dg(al, w_refs[0][...]))


def _dot_hi(a, b):
    return jnp.dot(a, b, precision=HI, preferred_element_type=F32)


def _sigmoid(x):
    return 1.0 / (1.0 + jnp.exp(-x))


def _layer_norm(x, g, b):
    mu = jnp.mean(x, -1, keepdims=True)
    d = x - mu
    var = jnp.mean(d * d, -1, keepdims=True)
    return d * lax.rsqrt(var + LN_EPS) * g + b


def _matmul_kernel(x_ref, *refs):
    o_ref = refs[-1]
    o_ref[...] = _mm_w(x_ref[...], refs[:-1])


def _matmul(x, w, tm, tn, passes):
    m, k = x.shape
    n = w.shape[1]
    ws = _split_weight(w, passes)
    return pl.pallas_call(
        _matmul_kernel,
        grid=(n // tn, m // tm),
        in_specs=[pl.BlockSpec((tm, k), lambda j, i: (i, 0))] + [pl.BlockSpec((k, tn), lambda j, i: (0, j))] * len(ws),
        out_specs=pl.BlockSpec((tm, tn), lambda j, i: (i, j)),
        out_shape=jax.ShapeDtypeStruct((m, n), F32),
        compiler_params=_params("parallel", "parallel"),
        name="in_proj",
    )(x, *ws)


def _ret_kernel(has_state, passes, q_ref, k_ref, v_ref, g_ref, cos_ref, sin_ref, dm_ref, qd_ref, kd_ref, cd_ref,
                gng_ref, gnb_ref, *rest):
    if has_state:
        s0_ref, y_ref, so_ref, s_scr = rest
    else:
        y_ref, so_ref, s_scr = rest
    c = pl.program_id(2)

    @pl.when(c == 0)
    def _():
        if has_state:
            s_scr[...] = s0_ref[:, 0]
        else:
            s_scr[...] = jnp.zeros_like(s_scr)

    bb, cl, _ = q_ref.shape
    cos = cos_ref[...]
    sin = sin_ref[...]
    mm = functools.partial(_mm, passes=passes)

    def rope(x):
        x2 = x.reshape(bb * cl, RET_DK)
        rot = pltpu.roll(x2, RET_DK // 2, axis=1).reshape(bb, cl, RET_DK)
        return x * cos + rot * sin

    q = rope(q_ref[...])
    k = rope(k_ref[...]) * (RET_DK ** -0.5)
    v = v_ref[...]
    s = s_scr[...]
    sc = mm(q, k, 'bid,bjd->bij') * dm_ref[0]
    intra = mm(sc, v, 'bij,bje->bie')
    cross = mm(q, s, 'bid,bde->bie') * qd_ref[0]
    s_new = s * cd_ref[0] + mm(k * kd_ref[0], v, 'bjd,bje->bde')
    s_scr[...] = s_new

    y = intra + cross
    mu = jnp.mean(y, -1, keepdims=True)
    d = y - mu
    var = jnp.mean(d * d, -1, keepdims=True)
    yn = d * lax.rsqrt(var + RET_GN_EPS) * gng_ref[...] + gnb_ref[...]
    rg = g_ref[...]
    y_ref[...] = yn * (rg * _sigmoid(rg))

    @pl.when(c == pl.num_programs(2) - 1)
    def _():
        so_ref[:, 0] = s_new


def _retention(z, pos, t_real, s0, gn_g, gn_b, bb, passes, n_seq=None):
    flat = n_seq is not None
    tp = pos.shape[0]
    b = n_seq if flat else z.shape[0]
    cl = RET_CHUNK if t_real % RET_CHUNK == 0 else tp
    cr = min(cl, t_real)
    nc = tp // cl
    at = (lambda bi, c: (bi * nc + c, 0)) if flat else (lambda bi, c: (bi, c))
    half = RET_DK // 2
    inv = ROPE_BASE ** (-jnp.arange(half, dtype=F32) / half)
    ang = pos[:, None] * inv[None, :]
    cos = jnp.concatenate([jnp.cos(ang), jnp.cos(ang)], -1)
    sin = jnp.concatenate([-jnp.sin(ang), jnp.sin(ang)], -1)
    lg = jnp.log1p(-jnp.exp2(-5.0 - jnp.arange(RET_HEADS, dtype=F32)))
    i = jnp.arange(cl, dtype=F32)
    real = i < cr
    diff = i[:, None] - i[None, :]
    ok = (diff >= 0) & real[:, None] & real[None, :]
    dmask = jnp.exp(jnp.where(ok[None], diff[None] * lg[:, None, None], -jnp.inf))
    q_dec = jnp.exp((i[None, :] + 1.0) * lg[:, None])[..., None]
    k_dec = jnp.where(real[None, :], jnp.exp((cr - 1.0 - i)[None, :] * lg[:, None]), 0.0)[..., None]
    c_dec = jnp.exp(cr * lg)[:, None, None]

    has_state = s0 is not None
    qk_spec = lambda off: pl.BlockSpec((bb, cl, RET_DK), lambda bi, h, c: at(bi, c) + (off + h,))
    v_spec = lambda off: pl.BlockSpec((bb, cl, RET_DV), lambda bi, h, c: at(bi, c) + (off + h,))
    head_spec = lambda shape: pl.BlockSpec((1,) + shape, lambda bi, h, c: (h, 0, 0))
    s_spec = pl.BlockSpec((bb, 1, RET_DK, RET_DV), lambda bi, h, c: (bi, h, 0, 0))
    in_specs = [
        qk_spec(0), qk_spec(RET_HEADS), v_spec(RET_HEADS), v_spec(2 * RET_HEADS),
        pl.BlockSpec((cl, RET_DK), lambda bi, h, c: (c, 0)), pl.BlockSpec((cl, RET_DK), lambda bi, h, c: (c, 0)),
        head_spec((cl, cl)), head_spec((cl, 1)), head_spec((cl, 1)), head_spec((1, 1)),
        pl.BlockSpec((1, RET_DV), lambda bi, h, c: (0, h)), pl.BlockSpec((1, RET_DV), lambda bi, h, c: (0, h)),
    ]
    args = [z, z, z, z, cos, sin, dmask, q_dec, k_dec, c_dec, gn_g.reshape(1, -1), gn_b.reshape(1, -1)]
    if has_state:
        in_specs.append(s_spec)
        args.append(s0)
    return pl.pallas_call(
        functools.partial(_ret_kernel, has_state, passes),
        grid=(b // bb, RET_HEADS, nc),
        in_specs=in_specs,
        out_specs=[pl.BlockSpec((bb, cl, RET_DV), lambda bi, h, c: at(bi, c) + (h,)), s_spec],
        out_shape=[jax.ShapeDtypeStruct(z.shape[:2] + (RET_V_W,), F32),
                   jax.ShapeDtypeStruct((b, RET_HEADS, RET_DK, RET_DV), F32)],
        scratch_shapes=[pltpu.VMEM((bb, RET_DK, RET_DV), F32)],
        compiler_params=_params("parallel", "parallel", "arbitrary"),
        name="retention",
    )(*args)


def _head_sum(x, bd):
    hi, lo = _split(x)
    return jnp.dot(hi, bd, preferred_element_type=F32) + jnp.dot(lo, bd, preferred_element_type=F32)


def _pair_shape(n_pairs, chunks, w):
    return (n_pairs + 1, chunks, 2, RWKV_CHUNK, w)


def _pair_tile_spec(tiles_per_seq, w):
    rows = TOKEN_TILE // RWKV_CHUNK
    return pl.BlockSpec((1, rows, 1, RWKV_CHUNK, w),
                        lambda i: (i // (2 * tiles_per_seq), i % tiles_per_seq, (i // tiles_per_seq) % 2, 0, 0))


def _to_pair_tile(ref, x):
    ref[0, :, 0] = x.reshape(TOKEN_TILE // RWKV_CHUNK, RWKV_CHUNK, x.shape[-1])


def _from_pair_tile(ref):
    return ref[0, :, 0].reshape(TOKEN_TILE, ref.shape[-1])


def _rwkv_pre_kernel(has_vres, n_prompt_tiles, tiles_per_seq, u_ref, tail_ref, shift_ref, mu_ref, w0_ref, wup_ref,
                     a0_ref, aup_ref, gup_ref, kk_ref, ka_ref, bd_ref, *rest):
    if has_vres:
        vf_ref, v0_ref, vd_ref, vu_ref, r_o, lw_o, k_o, v_o, a_o, b_o, g_o = rest
    else:
        r_o, lw_o, k_o, v_o, a_o, b_o, g_o = rest
    i = pl.program_id(0)
    u = u_ref[...]
    tm = u.shape[0]
    row = lax.broadcasted_iota(jnp.int32, (tm, 1), 0)
    before = jnp.where(i % tiles_per_seq == 0, 0.0, tail_ref[tail_ref.shape[0] - 1:, :])
    prev_prompt = jnp.where(row == 0, before, pltpu.roll(u, 1, axis=0))
    n_seq = shift_ref.shape[0]
    prev_sample = jnp.concatenate([shift_ref[...], u[:tm - n_seq]], 0)
    prev = jnp.where(i < n_prompt_tiles, prev_prompt, prev_sample)
    um = u + (prev - u) * mu_ref[...]
    w1, w2, w3 = RWKV_W, 2 * RWKV_W, 3 * RWKV_W
    r = um[:, :w1]
    kw = um[:, w1:w2]
    vw = um[:, w2:w3]
    wd = um[:, w3:w3 + LORA_W]
    ad = um[:, w3 + LORA_W:w3 + LORA_W + LORA_A]
    gd = um[:, w3 + LORA_W + LORA_A:]
    xw = w0_ref[...] + _dot_hi(jnp.tanh(wd), wup_ref[...])
    softplus = jnp.maximum(-xw, 0.0) + jnp.log1p(jnp.exp(-jnp.abs(xw)))
    _to_pair_tile(lw_o, -jnp.exp(-softplus - 0.5))
    a = _sigmoid(a0_ref[...] + _dot_hi(ad, aup_ref[...]))
    _to_pair_tile(g_o, _dot_hi(_sigmoid(gd), gup_ref[...]))
    if has_vres:
        gate = _sigmoid(v0_ref[...] + _dot_hi(_dot_hi(vw, vd_ref[...]), vu_ref[...]))
        vw = vw + (_from_pair_tile(vf_ref) - vw) * gate
    kk = kw * kk_ref[...]
    norm = jnp.sqrt(_head_sum(kk * kk, bd_ref[...]))
    kk = kk / jnp.maximum(norm, 1e-12)
    _to_pair_tile(r_o, r)
    _to_pair_tile(k_o, kw * (1.0 + (a - 1.0) * ka_ref[...]))
    _to_pair_tile(v_o, vw)
    _to_pair_tile(a_o, -kk)
    _to_pair_tile(b_o, kk * a)


def _head_blockdiag():
    h = jnp.arange(RWKV_W) // RWKV_N
    return (h[:, None] == h[None, :]).astype(BF16)


def _rwkv_pre(u, shift_state, n_prompt, seq_len, p, l, v_first, vres):
    n = u.shape[0]
    tm = TOKEN_TILE
    assert n == n_prompt + tm and seq_len % tm == 0
    tiles_per_seq = seq_len // tm
    tail_rows = 8
    row = lambda a: a.reshape(1, -1)
    full = lambda a: pl.BlockSpec(a.shape, lambda i: (0,) * a.ndim)
    pair = _pair_tile_spec(tiles_per_seq, RWKV_W)
    has_vres = vres is not None
    args = [u, u, shift_state, row(p['rw_mu'][l]), row(p['rw_w0'][l]), p['rw_w_up'][l], row(p['rw_a0'][l]),
            p['rw_a_up'][l], p['rw_g_up'][l], row(p['rw_k_k'][l]), row(p['rw_k_a'][l]), _head_blockdiag()]
    in_specs = [pl.BlockSpec((tm, SHIFT_W), lambda i: (i, 0)),
                pl.BlockSpec((tail_rows, SHIFT_W), lambda i: (jnp.maximum(i * (tm // tail_rows) - 1, 0), 0))]
    in_specs += [full(a) for a in args[2:]]
    if has_vres:
        extra = [v_first, row(vres[0]), vres[1], vres[2]]
        in_specs += [pair] + [full(a) for a in extra[1:]]
        args += extra
    shape = _pair_shape(n_prompt // seq_len // 2, seq_len // RWKV_CHUNK, RWKV_W)
    return pl.pallas_call(
        functools.partial(_rwkv_pre_kernel, has_vres, n_prompt // tm, tiles_per_seq),
        grid=(n // tm,),
        in_specs=in_specs,
        out_specs=[pair] * 7,
        out_shape=[jax.ShapeDtypeStruct(shape, F32)] * 7,
        compiler_params=_params("parallel"),
        name="rwkv_pre",
    )(*args)


def _rwkv_chunk_kernel(passes, r_ref, lw_ref, k_ref, v_ref, a_ref, b_ref, tri_ref, y_ref, so_ref, s_scr):
    c = pl.program_id(1)

    @pl.when(c == 0)
    def _():
        s_scr[...] = jnp.zeros_like(s_scr)

    bb, cl = r_ref.shape[2], r_ref.shape[3]
    r_ref, lw_ref, k_ref, v_ref, a_ref, b_ref, y_ref = (ref.at[0, 0] for ref in
                                                        (r_ref, lw_ref, k_ref, v_ref, a_ref, b_ref, y_ref))
    mm = functools.partial(_mm, passes=passes)
    ti = lax.broadcasted_iota(jnp.int32, (cl, cl), 0)
    si = lax.broadcasted_iota(jnp.int32, (cl, cl), 1)
    strict = (ti > si).astype(F32)
    incl = (ti >= si).astype(F32)
    eye = (ti == si).astype(F32)

    def heads(x):
        return jnp.stack([x[bi][:, h * RWKV_N:(h + 1) * RWKV_N] for bi in range(bb) for h in range(RWKV_HEADS)])

    lw = lw_ref[...]
    cum = jnp.stack([_dot_hi(tri_ref[...], lw[bi]) for bi in range(bb)])
    last = cum[:, cl - 1:cl, :]
    e_neg = jnp.exp(-cum)
    e_end = jnp.exp(last - cum)
    at = heads(a_ref[...] * jnp.exp(cum - lw))
    rt = heads(r_ref[...] * jnp.exp(cum))
    bt = heads(b_ref[...] * e_neg)
    kt = heads(k_ref[...] * e_neg)
    bw = heads(b_ref[...] * e_end)
    kw = heads(k_ref[...] * e_end)
    wc = heads(jnp.exp(last))
    vh = heads(v_ref[...])

    lhs = jnp.concatenate([at, rt], 1)
    gram = mm(lhs, jnp.concatenate([bt, kt], 1), 'gik,gjk->gij')
    a_ab = gram[:, :cl, :cl] * strict
    a_ak = gram[:, :cl, cl:] * strict
    a_rb = gram[:, cl:, :cl] * incl
    a_rk = gram[:, cl:, cl:] * incl
    inv = eye + a_ab
    pw = a_ab
    for _ in range(cl.bit_length() - 2):
        pw = mm(pw, pw, 'gij,gjk->gik')
        inv = inv + mm(inv, pw, 'gij,gjk->gik')
    s0 = s_scr[...].reshape(bb * RWKV_HEADS, RWKV_N, RWKV_N)
    xs = mm(lhs, s0, 'gtj,gij->gti')
    av = mm(jnp.concatenate([a_ak, a_rk], 1), vh, 'gts,gsi->gti')
    u = mm(inv, xs[:, :cl] + av[:, :cl], 'gts,gsi->gti')
    y = xs[:, cl:] + av[:, cl:] + mm(a_rb, u, 'gts,gsi->gti')
    s_new = s0 * wc + mm(jnp.concatenate([u, vh], 1), jnp.concatenate([bw, kw], 1), 'gti,gtj->gij')
    s_scr[...] = s_new.reshape(bb, RWKV_HEADS, RWKV_N, RWKV_N)
    for bi in range(bb):
        for h in range(RWKV_HEADS):
            y_ref[bi, :, h * RWKV_N:(h + 1) * RWKV_N] = y[bi * RWKV_HEADS + h]

    @pl.when(c == pl.num_programs(1) - 1)
    def _():
        so_ref[...] = s_scr[...]


def _rwkv_chunks(r, lw, k, v, a, b, passes):
    n_pairs, chunks, bb, cl, _ = r.shape
    n_pairs -= 1
    seq = pl.BlockSpec((1, 1, bb, cl, RWKV_W), lambda pi, c: (pi, c, 0, 0, 0))
    s_spec = pl.BlockSpec((bb, RWKV_HEADS, RWKV_N, RWKV_N), lambda pi, c: (pi, 0, 0, 0))
    tri = (jnp.arange(cl)[:, None] >= jnp.arange(cl)[None, :]).astype(F32)
    return pl.pallas_call(
        functools.partial(_rwkv_chunk_kernel, passes),
        grid=(n_pairs, chunks),
        in_specs=[seq] * 6 + [pl.BlockSpec((cl, cl), lambda pi, c: (0, 0))],
        out_specs=[seq, s_spec],
        out_shape=[jax.ShapeDtypeStruct(r.shape, F32),
                   jax.ShapeDtypeStruct((n_pairs * bb, RWKV_HEADS, RWKV_N, RWKV_N), F32)],
        scratch_shapes=[pltpu.VMEM((bb, RWKV_HEADS, RWKV_N, RWKV_N), F32)],
        compiler_params=_params("parallel", "arbitrary"),
        name="rwkv_chunks",
    )(r, lw, k, v, a, b, tri)


def _rwkv_step_kernel(r_ref, lw_ref, k_ref, a_ref, b_ref, vt_ref, s0_ref, yt_ref, so_ref):
    s = s0_ref[...]
    for t in range(r_ref.shape[2]):
        row = lambda ref: ref[:, :, t:t + 1, :]
        sa = jnp.sum(s * row(a_ref), -1, keepdims=True)
        s = s * jnp.exp(row(lw_ref)) + sa * row(b_ref) + vt_ref[:, :, :, t:t + 1] * row(k_ref)
        yt_ref[:, :, :, t:t + 1] = jnp.sum(s * row(r_ref), -1, keepdims=True)
    so_ref[...] = s


def _rwkv_steps(r, lw, k, v, a, b, s0, y_pairs):
    bsz = s0.shape[0]
    t = TOKEN_TILE // bsz
    bb = RWKV_STEP_BATCH_BLOCK
    tile_rows = TOKEN_TILE // RWKV_CHUNK
    steps = lambda x: x[-1, :tile_rows, 0].reshape(t, bsz, RWKV_HEADS, RWKV_N)
    rows = lambda x: steps(x).transpose(1, 2, 0, 3)
    vt = steps(v).transpose(1, 2, 3, 0)
    row_spec = pl.BlockSpec((bb, RWKV_HEADS, t, RWKV_N), lambda i: (i, 0, 0, 0))
    col_spec = pl.BlockSpec((bb, RWKV_HEADS, RWKV_N, t), lambda i: (i, 0, 0, 0))
    s_spec = pl.BlockSpec((bb, RWKV_HEADS, RWKV_N, RWKV_N), lambda i: (i, 0, 0, 0))
    yt, s_new = pl.pallas_call(
        _rwkv_step_kernel,
        grid=(bsz // bb,),
        in_specs=[row_spec] * 5 + [col_spec, s_spec],
        out_specs=[col_spec, s_spec],
        out_shape=[jax.ShapeDtypeStruct((bsz, RWKV_HEADS, RWKV_N, t), F32),
                   jax.ShapeDtypeStruct((bsz, RWKV_HEADS, RWKV_N, RWKV_N), F32)],
        compiler_params=_params("parallel"),
        name="rwkv_steps",
    )(rows(r), rows(lw), rows(k), rows(a), rows(b), vt, s0)
    y_tile = yt.transpose(3, 0, 1, 2).reshape(tile_rows, RWKV_CHUNK, RWKV_W)
    return y_pairs.at[-1, :tile_rows, 0].set(y_tile), s_new


def _post_kernel(n_w, x_ref, yrw_ref, r_ref, k_ref, v_ref, g_ref, yret_ref, ga_ref, gb_ref, bd_ref, rk_ref, gng_ref,
                 gnb_ref, ln_g_ref, ln_b_ref, wr_ref, br_ref, *rest):
    wret, wrw, wo = rest[:n_w], rest[n_w:2 * n_w], rest[2 * n_w:3 * n_w]
    x1_ref, xb_ref, ids_ref, probs_ref = rest[3 * n_w:]
    bd = bd_ref[...]
    y = _from_pair_tile(yrw_ref)
    mu = _head_sum(y, bd) * (1.0 / RWKV_N)
    d = y - mu
    var = _head_sum(d * d, bd) * (1.0 / RWKV_N)
    yn = d * lax.rsqrt(var + RWKV_GN_EPS) * gng_ref[...] + gnb_ref[...]
    bonus = _head_sum(_from_pair_tile(r_ref) * _from_pair_tile(k_ref) * rk_ref[...], bd) * _from_pair_tile(v_ref)
    yb = (yn + bonus) * _from_pair_tile(g_ref)
    merged = _sigmoid(ga_ref[...]) * _mm_w(yret_ref[...], wret) + _sigmoid(gb_ref[...]) * _mm_w(yb, wrw)
    out = _mm_w(merged, wo)
    x1 = _layer_norm(DN_ALPHA * x_ref[...] + out, ln_g_ref[...], ln_b_ref[...])
    x1_ref[...] = x1
    xb_ref[...] = x1.astype(BF16)

    logits = _dot_hi(x1, wr_ref[...]) + br_ref[...]
    lane = lax.broadcasted_iota(jnp.int32, logits.shape, 1)
    work = logits
    ids, vals = [], []
    for _ in range(TOP_K):
        m = jnp.max(work, -1, keepdims=True)
        idx = jnp.min(jnp.where(work == m, lane, N_EXPERTS), -1, keepdims=True)
        ids.append(idx)
        vals.append(m)
        work = jnp.where(lane == idx, -jnp.inf, work)
    exps = [jnp.exp(m - vals[0]) for m in vals]
    inv_den = 1.0 / sum(exps)
    slot = lax.broadcasted_iota(jnp.int32, ids_ref.shape, 1)
    ids_out = jnp.zeros(ids_ref.shape, jnp.int32)
    probs_out = jnp.zeros(probs_ref.shape, F32)
    for j in range(TOP_K):
        ids_out = jnp.where(slot == j, ids[j], ids_out)
        probs_out = jnp.where(slot == j, exps[j] * inv_den, probs_out)
    ids_ref[...] = ids_out
    probs_ref[...] = probs_out


def _post(x, yrw, r, k, v, g, yret, gates, seq_len, p, l, passes):
    n = x.shape[0]
    tm = TOKEN_TILE
    row = lambda a: a.reshape(1, -1)
    tile = lambda w, j=0: pl.BlockSpec((tm, w), lambda i: (i, j))
    full = lambda a: pl.BlockSpec(a.shape, lambda i: (0,) * a.ndim)
    pair = _pair_tile_spec(seq_len // tm, RWKV_W)
    weights = (_split_weight(p['w_ret_out'][l], passes) + _split_weight(p['w_rwkv_out'][l], passes)
               + _split_weight(p['w_o'][l], passes))
    consts = [_head_blockdiag(), row(p['rw_r_k'][l]), row(p['rw_gn_g'][l]), row(p['rw_gn_b'][l]),
              row(p['ln1_g'][l]), row(p['ln1_b'][l]), p['w_router'][l], row(p['b_router'][l]), *weights]
    in_specs = ([tile(D_MODEL)] + [pair] * 5 + [tile(RET_V_W), tile(D_MODEL, 0), tile(D_MODEL, 1)]
                + [full(a) for a in consts])
    return pl.pallas_call(
        functools.partial(_post_kernel, len(weights) // 3),
        grid=(n // tm,),
        in_specs=in_specs,
        out_specs=[tile(D_MODEL), tile(D_MODEL), tile(TOP_K), tile(TOP_K)],
        out_shape=[jax.ShapeDtypeStruct((n, D_MODEL), F32), jax.ShapeDtypeStruct((n, D_MODEL), BF16),
                   jax.ShapeDtypeStruct((n, TOP_K), jnp.int32), jax.ShapeDtypeStruct((n, TOP_K), F32)],
        compiler_params=_params("parallel"),
        name="merge_ln_router",
    )(x, yrw, r, k, v, g, yret, gates, gates, *consts)


SEG_ALIGN = 8
ROUTE_TILE = 512
SEG_BITS = (ROUTE_TILE // SEG_ALIGN).bit_length()
LOCAL_ROWS = 2304
PROB_LANES = 128
ROW_W = D_MODEL + PROB_LANES


def _segment_dmas(i, cnt_ref, lst_ref, gst_ref, make_copy, act):
    def per_expert(e, carry):
        j = i * N_EXPERTS + e
        groups = cnt_ref[j] // SEG_ALIGN
        done = jnp.int32(0)
        for bit in reversed(range(SEG_BITS)):
            size = SEG_ALIGN << bit
            has = ((groups >> bit) & 1) == 1
            lo = pl.multiple_of(lst_ref[j] + done, SEG_ALIGN)
            go = pl.multiple_of(gst_ref[j] + done, SEG_ALIGN)

            @pl.when(has)
            def _():
                act(make_copy(lo, go, size))

            done = done + jnp.where(has, size, 0)
        return carry

    lax.fori_loop(0, N_EXPERTS, per_expert, 0)


def _dispatch_kernel(cnt_ref, lst_ref, gst_ref, ids_ref, probs_ref, lstart_ref, xb_ref, xs_hbm, buf, sem):
    i = pl.program_id(0)
    tm = xb_ref.shape[0]
    ids = ids_ref[0]
    probs = probs_ref[0]
    expert = lax.broadcasted_iota(jnp.int32, (N_EXPERTS, tm), 0)
    picks = [ids[k:k + 1, :] == expert for k in range(TOP_K)]
    picked = sum(pk.astype(F32) for pk in picks)
    m = lax.broadcasted_iota(jnp.int32, (tm, tm), 0)
    n = lax.broadcasted_iota(jnp.int32, (tm, tm), 1)
    earlier = (m < n).astype(BF16)
    rank = jnp.dot(picked.astype(BF16), earlier, preferred_element_type=F32)
    pos = lstart_ref[0] + rank
    row = lax.broadcasted_iota(jnp.int32, (LOCAL_ROWS, tm), 0)
    perm = jnp.zeros((LOCAL_ROWS, tm), F32)
    weight = jnp.zeros((LOCAL_ROWS, tm), F32)
    for k in range(TOP_K):
        lpos = jnp.sum(jnp.where(picks[k], pos, 0.0), 0, keepdims=True).astype(jnp.int32)
        hit = row == lpos
        perm = jnp.where(hit, 1.0, perm)
        weight = jnp.where(hit, probs[k:k + 1, :], weight)
    buf[:, :D_MODEL] = jnp.dot(perm.astype(BF16), xb_ref[...], preferred_element_type=F32)
    buf[:, D_MODEL:] = jnp.broadcast_to(jnp.sum(weight, 1, keepdims=True), (LOCAL_ROWS, PROB_LANES))

    make_copy = lambda lo, go, size: pltpu.make_async_copy(buf.at[pl.ds(lo, size)], xs_hbm.at[pl.ds(go, size)], sem)
    _segment_dmas(i, cnt_ref, lst_ref, gst_ref, make_copy, lambda c: c.start())
    _segment_dmas(i, cnt_ref, lst_ref, gst_ref, make_copy, lambda c: c.wait())


def _routing_tables(ids):
    n = ids.shape[0]
    nt = n // ROUTE_TILE
    picked = jnp.sum(ids[:, :, None] == jnp.arange(N_EXPERTS, dtype=jnp.int32)[None, None, :], 1)
    cnt = jnp.sum(picked.reshape(nt, ROUTE_TILE, N_EXPERTS), 1).astype(jnp.int32)
    cnt = (cnt + SEG_ALIGN - 1) // SEG_ALIGN * SEG_ALIGN
    lstart = jnp.cumsum(cnt, 1) - cnt
    per_expert = jnp.sum(cnt, 0)
    gstart = (jnp.cumsum(per_expert) - per_expert)[None, :] + jnp.cumsum(cnt, 0) - cnt
    return cnt, lstart.astype(jnp.int32), gstart.astype(jnp.int32), per_expert


def _dispatch(xb, ids, probs, tables, n_rows):
    n = xb.shape[0]
    tm = ROUTE_TILE
    nt = n // tm
    cnt, lstart, gstart, _ = tables
    to_lanes = lambda a: a.reshape(nt, tm, TOP_K).transpose(0, 2, 1)
    grid_spec = pltpu.PrefetchScalarGridSpec(
        num_scalar_prefetch=3,
        grid=(nt,),
        in_specs=[pl.BlockSpec((1, TOP_K, tm), lambda i, *_: (i, 0, 0)), pl.BlockSpec((1, TOP_K, tm), lambda i, *_: (i, 0, 0)),
                  pl.BlockSpec((1, N_EXPERTS, 1), lambda i, *_: (i, 0, 0)), pl.BlockSpec((tm, D_MODEL), lambda i, *_: (i, 0))],
        out_specs=pl.BlockSpec(memory_space=pl.ANY),
        scratch_shapes=[pltpu.VMEM((LOCAL_ROWS, ROW_W), F32), pltpu.SemaphoreType.DMA],
    )
    return pl.pallas_call(
        _dispatch_kernel,
        grid_spec=grid_spec,
        out_shape=jax.ShapeDtypeStruct((n_rows, ROW_W), F32),
        compiler_params=_params("arbitrary"),
        name="dispatch",
    )(cnt.reshape(-1), lstart.reshape(-1), gstart.reshape(-1), to_lanes(ids), to_lanes(probs),
      lstart.astype(F32)[:, :, None], xb)


def _expert_kernel(tile_ref, exp_ref, first_ref, valid_ref, lo_ref, hi_ref, x_ref, wgu_ref, bgu_ref, wd_ref,
                   bd_ref, o_ref, wgu_s, wd_s):
    i = pl.program_id(0)
    tm = x_ref.shape[0]

    @pl.when(valid_ref[i] == 1)
    def _():
        @pl.when((i == 0) | (exp_ref[jnp.maximum(i - 1, 0)] != exp_ref[i]))
        def _():
            wgu_s[...] = wgu_ref[0, 0].astype(BF16)
            wd_s[...] = wd_ref[0, 0].astype(BF16)

        h = jnp.dot(x_ref[:, :D_MODEL].astype(BF16), wgu_s[...], preferred_element_type=F32) + bgu_ref[0, 0]
        gate = jnp.minimum(h[:, :D_FF], SWIGLU_LIMIT)
        up = jnp.clip(h[:, D_FF:], -SWIGLU_LIMIT, SWIGLU_LIMIT)
        act = gate * _sigmoid(SWIGLU_ALPHA * gate) * (up + 1.0)
        y = jnp.dot(act.astype(BF16), wd_s[...], preferred_element_type=F32) + bd_ref[0, 0]
        y = y * x_ref[:, D_MODEL:D_MODEL + 1]
        rows = tile_ref[i] * tm + lax.broadcasted_iota(jnp.int32, (tm, 1), 0)
        mine = (rows >= lo_ref[i]) & (rows < hi_ref[i])

        @pl.when(first_ref[i] == 1)
        def _():
            o_ref[...] = jnp.where(mine, y, 0.0)

        @pl.when(first_ref[i] == 0)
        def _():
            o_ref[...] = jnp.where(mine, y, o_ref[...])


def _expert_schedule(counts, n_rows, tm):
    n_tiles = n_rows // tm
    n_entries = n_tiles + N_EXPERTS - 1
    end = jnp.cumsum(counts)
    off = end - counts
    first_tile = off // tm
    n_t = jnp.where(counts > 0, (end - 1) // tm - first_tile + 1, 0)
    entry_end = jnp.cumsum(n_t)
    entry_off = entry_end - n_t
    total = entry_end[-1]
    i = jnp.minimum(jnp.arange(n_entries), total - 1)
    e = jnp.sum(entry_end[None, :] <= i[:, None], -1).astype(jnp.int32)
    t = (first_tile[e] + i - entry_off[e]).astype(jnp.int32)
    valid = (jnp.arange(n_entries) < total).astype(jnp.int32)
    first = jnp.concatenate([jnp.ones((1,), jnp.int32), (t[1:] != t[:-1]).astype(jnp.int32)])
    return t, e, first, valid, off[e].astype(jnp.int32), end[e].astype(jnp.int32)


def _experts(xs, sched, p, l):
    n_rows = xs.shape[0]
    tm = EXPERT_ROW_TILE
    bgu = p['b_gate_up'].reshape(DEPTH, N_EXPERTS, 1, 2 * D_FF)
    bdn = p['b_down'].reshape(DEPTH, N_EXPERTS, 1, D_MODEL)
    by_tile = lambda w: pl.BlockSpec((tm, w), lambda i, t, e, *_: (t[i], 0))
    by_expert = lambda a, b: pl.BlockSpec((1, 1, a, b), lambda i, t, e, *_: (l, e[i], 0, 0))
    grid_spec = pltpu.PrefetchScalarGridSpec(
        num_scalar_prefetch=6,
        grid=(sched[0].shape[0],),
        in_specs=[by_tile(ROW_W), by_expert(D_MODEL, 2 * D_FF), by_expert(1, 2 * D_FF),
                  by_expert(D_FF, D_MODEL), by_expert(1, D_MODEL)],
        out_specs=by_tile(D_MODEL),
        scratch_shapes=[pltpu.VMEM((D_MODEL, 2 * D_FF), BF16), pltpu.VMEM((D_FF, D_MODEL), BF16)],
    )
    return pl.pallas_call(
        _expert_kernel,
        grid_spec=grid_spec,
        out_shape=jax.ShapeDtypeStruct((n_rows, D_MODEL), F32),
        compiler_params=_params("arbitrary"),
        name="experts",
    )(*sched, xs, p['w_gate_up'], bgu, p['w_down'], bdn)


def _combine_kernel(cnt_ref, lst_ref, gst_ref, ids_ref, lstart_ref, x_ref, ln_g_ref, ln_b_ref, ys_hbm, o_ref, buf, sem):
    i = pl.program_id(0)
    tm = x_ref.shape[0]
    buf[ROUTE_TILE * TOP_K:, :] = jnp.zeros((LOCAL_ROWS - ROUTE_TILE * TOP_K, D_MODEL), F32)
    make_copy = lambda lo, go, size: pltpu.make_async_copy(ys_hbm.at[pl.ds(go, size)], buf.at[pl.ds(lo, size)], sem)
    _segment_dmas(i, cnt_ref, lst_ref, gst_ref, make_copy, lambda c: c.start())

    ids = ids_ref[...]
    expert = lax.broadcasted_iota(jnp.int32, (tm, N_EXPERTS), 1)
    picks = [ids[:, k:k + 1] == expert for k in range(TOP_K)]
    picked = sum(pk.astype(F32) for pk in picks)
    m = lax.broadcasted_iota(jnp.int32, (tm, tm), 0)
    n = lax.broadcasted_iota(jnp.int32, (tm, tm), 1)
    earlier = (n < m).astype(BF16)
    rank = jnp.dot(earlier, picked.astype(BF16), preferred_element_type=F32)
    pos = lstart_ref[0] + rank
    col = lax.broadcasted_iota(jnp.int32, (tm, LOCAL_ROWS), 1)
    perm = jnp.zeros((tm, LOCAL_ROWS), F32)
    for k in range(TOP_K):
        lpos = jnp.sum(jnp.where(picks[k], pos, 0.0), 1, keepdims=True).astype(jnp.int32)
        perm = jnp.where(col == lpos, 1.0, perm)

    _segment_dmas(i, cnt_ref, lst_ref, gst_ref, make_copy, lambda c: c.wait())
    moe = jnp.dot(perm.astype(BF16), buf[...].astype(BF16), preferred_element_type=F32)
    o_ref[...] = _layer_norm(DN_ALPHA * x_ref[...] + moe, ln_g_ref[...], ln_b_ref[...])


def _combine(x1, ys, ids, tables, ln_g, ln_b):
    n = x1.shape[0]
    tm = ROUTE_TILE
    nt = n // tm
    cnt, lstart, gstart, _ = tables
    grid_spec = pltpu.PrefetchScalarGridSpec(
        num_scalar_prefetch=3,
        grid=(nt,),
        in_specs=[pl.BlockSpec((tm, TOP_K), lambda i, *_: (i, 0)), pl.BlockSpec((1, 1, N_EXPERTS), lambda i, *_: (i, 0, 0)),
                  pl.BlockSpec((tm, D_MODEL), lambda i, *_: (i, 0)), pl.BlockSpec((1, D_MODEL), lambda i, *_: (0, 0)),
                  pl.BlockSpec((1, D_MODEL), lambda i, *_: (0, 0)), pl.BlockSpec(memory_space=pl.ANY)],
        out_specs=pl.BlockSpec((tm, D_MODEL), lambda i, *_: (i, 0)),
        scratch_shapes=[pltpu.VMEM((LOCAL_ROWS, D_MODEL), F32), pltpu.SemaphoreType.DMA],
    )
    return pl.pallas_call(
        _combine_kernel,
        grid_spec=grid_spec,
        out_shape=jax.ShapeDtypeStruct((n, D_MODEL), F32),
        compiler_params=_params("arbitrary"),
        name="combine_ln",
    )(cnt.reshape(-1), lstart.reshape(-1), gstart.reshape(-1), ids, lstart.astype(F32)[:, None, :], x1,
      ln_g.reshape(1, -1), ln_b.reshape(1, -1), ys)


def _moe(x1, xb, ids, probs, p, l):
    n = x1.shape[0]
    tm = EXPERT_ROW_TILE
    max_rows = n * TOP_K + (n // ROUTE_TILE) * N_EXPERTS * (SEG_ALIGN - 1)
    n_rows = -(-max_rows // tm) * tm
    tables = _routing_tables(ids)
    xs = _dispatch(xb, ids, probs, tables, n_rows)
    ys = _experts(xs, _expert_schedule(tables[3], n_rows, tm), p, l)
    return _combine(x1, ys, ids, tables, p['ln2_g'][l], p['ln2_b'][l])


def _pad_time(a, tp):
    return jnp.pad(a, ((0, 0), (0, tp - a.shape[1]), (0, 0)))


def kernel(x_prompt, x_sample, state_ret, state_rwkv, state_shift, w_in, ret_gn_g, ret_gn_b, w_ret_out, rw_mu, rw_w0, rw_w_up, rw_a0, rw_a_up, rw_g_up, rw_k_k, rw_k_a, rw_r_k, rw_gn_g, rw_gn_b, rw_v0, rw_vres_down, rw_vres_up, w_rwkv_out, w_o, ln1_g, ln1_b, w_router, b_router, w_gate_up, b_gate_up, w_down, b_down, ln2_g, ln2_b):
    p = dict(w_ret_out=w_ret_out, rw_mu=rw_mu, rw_w0=rw_w0, rw_w_up=rw_w_up, rw_a0=rw_a0, rw_a_up=rw_a_up,
             rw_g_up=rw_g_up, rw_k_k=rw_k_k, rw_k_a=rw_k_a, rw_r_k=rw_r_k, rw_gn_g=rw_gn_g, rw_gn_b=rw_gn_b,
             w_rwkv_out=w_rwkv_out, w_o=w_o, ln1_g=ln1_g, ln1_b=ln1_b, w_router=w_router, b_router=b_router,
             w_gate_up=w_gate_up, b_gate_up=b_gate_up, w_down=w_down, b_down=b_down, ln2_g=ln2_g, ln2_b=ln2_b)
    bp, tp, _ = x_prompt.shape
    bs, ts, _ = x_sample.shape
    np_, ns = bp * tp, bs * ts
    pos_p = jnp.arange(tp, dtype=F32)
    ts_ret = 8
    pos_s = PAST_LEN + jnp.arange(ts_ret, dtype=F32)

    x = jnp.concatenate([x_prompt.reshape(np_, D_MODEL), x_sample.transpose(1, 0, 2).reshape(ns, D_MODEL)], 0)
    outs = {k: [] for k in ('ret_p', 'rw_p', 'sh_p', 'ret_s', 'rw_s', 'sh_s')}
    v_first = None
    u_off = RET_W
    g_off = RET_W + SHIFT_W
    for l in range(DEPTH):
        passes = EXACT_PASSES if l == 0 else 1
        z_ret = _matmul(x, w_in[l][:, :u_off], TOKEN_TILE, RET_W // 2, passes)
        u = _matmul(x, w_in[l][:, u_off:g_off], TOKEN_TILE, SHIFT_W, passes)
        gates = _matmul(x, w_in[l][:, g_off:], TOKEN_TILE, D_MODEL, passes)

        zr_s = _pad_time(z_ret[np_:].reshape(ts, bs, RET_W).transpose(1, 0, 2), ts_ret)
        yret, sret_p = _retention(z_ret.reshape(-1, RET_CHUNK, RET_W), pos_p, tp, None, ret_gn_g[l], ret_gn_b[l], 1,
                                  passes, n_seq=bp)
        yret_s, sret_s = _retention(zr_s, pos_s, ts, state_ret[l], ret_gn_g[l], ret_gn_b[l], 8, passes)
        yret = yret.reshape(-1, RET_V_W).at[np_:].set(yret_s[:, :ts].transpose(1, 0, 2).reshape(ns, RET_V_W))

        vres = None if l == 0 else (rw_v0[l - 1], rw_vres_down[l - 1], rw_vres_up[l - 1])
        r, lw, k, v, a, b, g = _rwkv_pre(u, state_shift[l], np_, tp, p, l, v_first, vres)
        if l == 0:
            v_first = v
        yrw, srw_p = _rwkv_chunks(r, lw, k, v, a, b, passes)
        yrw, srw_s = _rwkv_steps(r, lw, k, v, a, b, state_rwkv[l], yrw)

        x1, xb, ids, probs = _post(x, yrw, r, k, v, g, yret, gates, tp, p, l, passes)
        x = _moe(x1, xb, ids, probs, p, l)

        outs['ret_p'].append(sret_p)
        outs['ret_s'].append(sret_s)
        outs['rw_p'].append(srw_p)
        outs['rw_s'].append(srw_s)
        outs['sh_p'].append(u[tp - 1:np_:tp])
        outs['sh_s'].append(u[np_ + ns - bs:])

    y_prompt = x[:np_].reshape(bp, tp, D_MODEL)
    y_sample = x[np_:].reshape(ts, bs, D_MODEL).transpose(1, 0, 2)
    st = {k: jnp.stack(v) for k, v in outs.items()}
    return (y_prompt, y_sample, st['ret_p'], st['rw_p'], st['sh_p'], st['ret_s'], st['rw_s'], st['sh_s'])
```

```python
import functools

import jax
import jax.numpy as jnp
from jax import lax
from jax.experimental import pallas as pl
from jax.experimental.pallas import tpu as pltpu

F32 = jnp.float32
BF16 = jnp.bfloat16
HI = lax.Precision.HIGHEST

D_MODEL = 1024
DEPTH = 2
PAST_LEN = 16384
RET_HEADS = 4
RET_DK = 128
RET_DV = 256
RET_QK_W = RET_HEADS * RET_DK
RET_V_W = RET_HEADS * RET_DV
RET_W = 2 * RET_QK_W + 2 * RET_V_W
RET_CHUNK = 128
ROPE_BASE = 10000.0
RWKV_HEADS = 8
RWKV_N = 64
RWKV_W = RWKV_HEADS * RWKV_N
LORA_W = 64
LORA_A = 64
LORA_G = 128
SHIFT_W = 3 * RWKV_W + LORA_W + LORA_A + LORA_G
RWKV_CHUNK = 64
N_EXPERTS = 32
TOP_K = 4
D_FF = D_MODEL
SWIGLU_LIMIT = 7.0
SWIGLU_ALPHA = 1.702
DN_ALPHA = (2 * DEPTH) ** 0.25
LN_EPS = 1e-5
RET_GN_EPS = 1e-5
RWKV_GN_EPS = 64e-5

VMEM_LIMIT = 56 * 1024 * 1024
TOKEN_TILE = 512
EXPERT_ROW_TILE = 512
RWKV_STEP_BATCH_BLOCK = 8
EXACT_PASSES = 3


def _params(*sem):
    return pltpu.CompilerParams(dimension_semantics=sem, vmem_limit_bytes=VMEM_LIMIT)


def _split(x):
    hi = x.astype(BF16)
    lo = (x - hi.astype(F32)).astype(BF16)
    return hi, lo


def _split_kernel(w_ref, hi_ref, lo_ref):
    hi_ref[...], lo_ref[...] = _split(w_ref[...])


def _split_weight(w, passes):
    if passes == 1:
        return (w.astype(BF16),)
    rows = 256
    spec = pl.BlockSpec((rows, w.shape[1]), lambda i: (i, 0))
    return tuple(pl.pallas_call(
        _split_kernel,
        grid=(w.shape[0] // rows,),
        in_specs=[spec],
        out_specs=[spec, spec],
        out_shape=[jax.ShapeDtypeStruct(w.shape, BF16)] * 2,
        compiler_params=_params("parallel"),
        name="split_weight",
    )(w))


def _mm(a, b, spec, passes):
    dg = lambda x, y: jnp.einsum(spec, x, y, preferred_element_type=F32)
    if passes == 1:
        return dg(a.astype(BF16), b.astype(BF16))
    ah, al = _split(a)
    bh, bl = _split(b)
    return dg(ah, bh) + (dg(ah, bl) + dg(al, bh))


def _mm_w(a, w_refs):
    dg = lambda x, y: jnp.dot(x, y, preferred_element_type=F32)
    if len(w_refs) == 1:
        return dg(a.astype(BF16), w_refs[0][...])
    ah, al = _split(a)
    return dg(ah, w_refs[0][...]) + (dg(ah, w_refs[1][...]) + dg(al, w_refs[0][...]))


def _dot_hi(a, b):
    return jnp.dot(a, b, precision=HI, preferred_element_type=F32)


def _sigmoid(x):
    return 1.0 / (1.0 + jnp.exp(-x))


def _layer_norm(x, g, b):
    mu = jnp.mean(x, -1, keepdims=True)
    d = x - mu
    var = jnp.mean(d * d, -1, keepdims=True)
    return d * lax.rsqrt(var + LN_EPS) * g + b


def _matmul_kernel(x_ref, *refs):
    o_ref = refs[-1]
    o_ref[...] = _mm_w(x_ref[...], refs[:-1])


def _matmul(x, w, tm, tn, passes):
    m, k = x.shape
    n = w.shape[1]
    ws = _split_weight(w, passes)
    return pl.pallas_call(
        _matmul_kernel,
        grid=(n // tn, m // tm),
        in_specs=[pl.BlockSpec((tm, k), lambda j, i: (i, 0))] + [pl.BlockSpec((k, tn), lambda j, i: (0, j))] * len(ws),
        out_specs=pl.BlockSpec((tm, tn), lambda j, i: (i, j)),
        out_shape=jax.ShapeDtypeStruct((m, n), F32),
        compiler_params=_params("parallel", "parallel"),
        name="in_proj",
    )(x, *ws)


def _ret_kernel(has_state, passes, q_ref, k_ref, v_ref, g_ref, cos_ref, sin_ref, dm_ref, qd_ref, kd_ref, cd_ref,
                gng_ref, gnb_ref, *rest):
    if has_state:
        s0_ref, y_ref, so_ref, s_scr = rest
    else:
        y_ref, so_ref, s_scr = rest
    c = pl.program_id(1)
    bb, cl, _ = q_ref.shape
    nh = RET_HEADS

    def heads(x, w):
        return jnp.concatenate([x[:, :, h * w:(h + 1) * w] for h in range(nh)], 0)

    def per_head(ref):
        return jnp.concatenate([jnp.broadcast_to(ref[h], (bb,) + ref.shape[1:]) for h in range(nh)], 0)

    @pl.when(c == 0)
    def _():
        if has_state:
            s_scr[...] = jnp.concatenate([s0_ref[:, h] for h in range(nh)], 0)
        else:
            s_scr[...] = jnp.zeros_like(s_scr)

    cos = cos_ref[...]
    sin = sin_ref[...]
    mm = functools.partial(_mm, passes=passes)

    def rope(x):
        x2 = x.reshape(nh * bb * cl, RET_DK)
        rot = pltpu.roll(x2, RET_DK // 2, axis=1).reshape(nh * bb, cl, RET_DK)
        return x * cos + rot * sin

    q = rope(heads(q_ref[...], RET_DK))
    k = rope(heads(k_ref[...], RET_DK)) * (RET_DK ** -0.5)
    v = heads(v_ref[...], RET_DV)
    s = s_scr[...]
    sc = mm(q, k, 'bid,bjd->bij') * per_head(dm_ref)
    intra = mm(sc, v, 'bij,bje->bie')
    cross = mm(q, s, 'bid,bde->bie') * per_head(qd_ref)
    s_new = s * per_head(cd_ref) + mm(k * per_head(kd_ref), v, 'bjd,bje->bde')
    s_scr[...] = s_new

    y = intra + cross
    mu = jnp.mean(y, -1, keepdims=True)
    d = y - mu
    var = jnp.mean(d * d, -1, keepdims=True)
    yn = d * lax.rsqrt(var + RET_GN_EPS)
    for h in range(nh):
        cols = slice(h * RET_DV, (h + 1) * RET_DV)
        rg = g_ref[:, :, cols]
        y_ref[:, :, cols] = (yn[h * bb:(h + 1) * bb] * gng_ref[:, cols] + gnb_ref[:, cols]) * (rg * _sigmoid(rg))

    @pl.when(c == pl.num_programs(1) - 1)
    def _():
        for h in range(nh):
            so_ref[:, h] = s_new[h * bb:(h + 1) * bb]


def _retention(z, pos, t_real, s0, gn_g, gn_b, bb, passes, n_seq=None):
    flat = n_seq is not None
    tp = pos.shape[0]
    b = n_seq if flat else z.shape[0]
    cl = RET_CHUNK if t_real % RET_CHUNK == 0 else tp
    cr = min(cl, t_real)
    nc = tp // cl
    at = (lambda bi, c: (bi * nc + c, 0)) if flat else (lambda bi, c: (bi, c))
    half = RET_DK // 2
    inv = ROPE_BASE ** (-jnp.arange(half, dtype=F32) / half)
    ang = pos[:, None] * inv[None, :]
    cos = jnp.concatenate([jnp.cos(ang), jnp.cos(ang)], -1)
    sin = jnp.concatenate([-jnp.sin(ang), jnp.sin(ang)], -1)
    lg = jnp.log1p(-jnp.exp2(-5.0 - jnp.arange(RET_HEADS, dtype=F32)))
    i = jnp.arange(cl, dtype=F32)
    real = i < cr
    diff = i[:, None] - i[None, :]
    ok = (diff >= 0) & real[:, None] & real[None, :]
    dmask = jnp.exp(jnp.where(ok[None], diff[None] * lg[:, None, None], -jnp.inf))
    q_dec = jnp.exp((i[None, :] + 1.0) * lg[:, None])[..., None]
    k_dec = jnp.where(real[None, :], jnp.exp((cr - 1.0 - i)[None, :] * lg[:, None]), 0.0)[..., None]
    c_dec = jnp.exp(cr * lg)[:, None, None]

    has_state = s0 is not None
    cols = lambda w, j: pl.BlockSpec((bb, cl, w), lambda bi, c: at(bi, c) + (j,))
    full = lambda a: pl.BlockSpec(a.shape, lambda bi, c: (0,) * a.ndim)
    s_spec = pl.BlockSpec((bb, RET_HEADS, RET_DK, RET_DV), lambda bi, c: (bi, 0, 0, 0))
    consts = [dmask, q_dec, k_dec, c_dec, gn_g.reshape(1, -1), gn_b.reshape(1, -1)]
    in_specs = [cols(RET_QK_W, 0), cols(RET_QK_W, 1), cols(RET_V_W, 1), cols(RET_V_W, 2),
                pl.BlockSpec((cl, RET_DK), lambda bi, c: (c, 0)), pl.BlockSpec((cl, RET_DK), lambda bi, c: (c, 0))]
    in_specs += [full(a) for a in consts]
    args = [z, z, z, z, cos, sin] + consts
    if has_state:
        in_specs.append(s_spec)
        args.append(s0)
    return pl.pallas_call(
        functools.partial(_ret_kernel, has_state, passes),
        grid=(b // bb, nc),
        in_specs=in_specs,
        out_specs=[cols(RET_V_W, 0), s_spec],
        out_shape=[jax.ShapeDtypeStruct(z.shape[:2] + (RET_V_W,), F32),
                   jax.ShapeDtypeStruct((b, RET_HEADS, RET_DK, RET_DV), F32)],
        scratch_shapes=[pltpu.VMEM((RET_HEADS * bb, RET_DK, RET_DV), F32)],
        compiler_params=_params("parallel", "arbitrary"),
        name="retention",
    )(*args)


def _head_sum(x, bd):
    hi, lo = _split(x)
    return jnp.dot(hi, bd, preferred_element_type=F32) + jnp.dot(lo, bd, preferred_element_type=F32)


def _pair_shape(n_pairs, chunks, w):
    return (n_pairs + 1, chunks, 2, RWKV_CHUNK, w)


def _pair_tile_spec(tiles_per_seq, w):
    rows = TOKEN_TILE // RWKV_CHUNK
    return pl.BlockSpec((1, rows, 1, RWKV_CHUNK, w),
                        lambda i: (i // (2 * tiles_per_seq), i % tiles_per_seq, (i // tiles_per_seq) % 2, 0, 0))


def _to_pair_tile(ref, x):
    ref[0, :, 0] = x.reshape(TOKEN_TILE // RWKV_CHUNK, RWKV_CHUNK, x.shape[-1])


def _from_pair_tile(ref):
    return ref[0, :, 0].reshape(TOKEN_TILE, ref.shape[-1])


def _rwkv_pre_kernel(has_vres, n_prompt_tiles, tiles_per_seq, u_ref, tail_ref, shift_ref, mu_ref, w0_ref, wup_ref,
                     a0_ref, aup_ref, gup_ref, kk_ref, ka_ref, bd_ref, *rest):
    if has_vres:
        vf_ref, v0_ref, vd_ref, vu_ref, r_o, lw_o, k_o, v_o, a_o, b_o, g_o = rest
    else:
        r_o, lw_o, k_o, v_o, a_o, b_o, g_o = rest
    i = pl.program_id(0)
    u = u_ref[...]
    tm = u.shape[0]
    row = lax.broadcasted_iota(jnp.int32, (tm, 1), 0)
    before = jnp.where(i % tiles_per_seq == 0, 0.0, tail_ref[tail_ref.shape[0] - 1:, :])
    prev_prompt = jnp.where(row == 0, before, pltpu.roll(u, 1, axis=0))
    n_seq = shift_ref.shape[0]
    prev_sample = jnp.concatenate([shift_ref[...], u[:tm - n_seq]], 0)
    prev = jnp.where(i < n_prompt_tiles, prev_prompt, prev_sample)
    um = u + (prev - u) * mu_ref[...]
    w1, w2, w3 = RWKV_W, 2 * RWKV_W, 3 * RWKV_W
    r = um[:, :w1]
    kw = um[:, w1:w2]
    vw = um[:, w2:w3]
    wd = um[:, w3:w3 + LORA_W]
    ad = um[:, w3 + LORA_W:w3 + LORA_W + LORA_A]
    gd = um[:, w3 + LORA_W + LORA_A:]
    lora = lambda x, w_ref: _mm(x, w_ref[...], 'ik,kj->ij', EXACT_PASSES)
    xw = w0_ref[...] + lora(jnp.tanh(wd), wup_ref)
    softplus = jnp.maximum(-xw, 0.0) + jnp.log1p(jnp.exp(-jnp.abs(xw)))
    _to_pair_tile(lw_o, -jnp.exp(-softplus - 0.5))
    a = _sigmoid(a0_ref[...] + lora(ad, aup_ref))
    _to_pair_tile(g_o, lora(_sigmoid(gd), gup_ref))
    if has_vres:
        gate = _sigmoid(v0_ref[...] + lora(lora(vw, vd_ref), vu_ref))
        vw = vw + (_from_pair_tile(vf_ref) - vw) * gate
    kk = kw * kk_ref[...]
    norm = jnp.sqrt(_head_sum(kk * kk, bd_ref[...]))
    kk = kk / jnp.maximum(norm, 1e-12)
    _to_pair_tile(r_o, r)
    _to_pair_tile(k_o, kw * (1.0 + (a - 1.0) * ka_ref[...]))
    _to_pair_tile(v_o, vw)
    _to_pair_tile(a_o, -kk)
    _to_pair_tile(b_o, kk * a)


def _head_blockdiag():
    h = jnp.arange(RWKV_W) // RWKV_N
    return (h[:, None] == h[None, :]).astype(BF16)


def _rwkv_pre(u, shift_state, n_prompt, seq_len, p, l, v_first, vres):
    n = u.shape[0]
    tm = TOKEN_TILE
    assert n == n_prompt + tm and seq_len % tm == 0
    tiles_per_seq = seq_len // tm
    tail_rows = 8
    row = lambda a: a.reshape(1, -1)
    full = lambda a: pl.BlockSpec(a.shape, lambda i: (0,) * a.ndim)
    pair = _pair_tile_spec(tiles_per_seq, RWKV_W)
    has_vres = vres is not None
    args = [u, u, shift_state, row(p['rw_mu'][l]), row(p['rw_w0'][l]), p['rw_w_up'][l], row(p['rw_a0'][l]),
            p['rw_a_up'][l], p['rw_g_up'][l], row(p['rw_k_k'][l]), row(p['rw_k_a'][l]), _head_blockdiag()]
    in_specs = [pl.BlockSpec((tm, SHIFT_W), lambda i: (i, 0)),
                pl.BlockSpec((tail_rows, SHIFT_W), lambda i: (jnp.maximum(i * (tm // tail_rows) - 1, 0), 0))]
    in_specs += [full(a) for a in args[2:]]
    if has_vres:
        extra = [v_first, row(vres[0]), vres[1], vres[2]]
        in_specs += [pair] + [full(a) for a in extra[1:]]
        args += extra
    shape = _pair_shape(n_prompt // seq_len // 2, seq_len // RWKV_CHUNK, RWKV_W)
    return pl.pallas_call(
        functools.partial(_rwkv_pre_kernel, has_vres, n_prompt // tm, tiles_per_seq),
        grid=(n // tm,),
        in_specs=in_specs,
        out_specs=[pair] * 7,
        out_shape=[jax.ShapeDtypeStruct(shape, F32)] * 7,
        compiler_params=_params("parallel"),
        name="rwkv_pre",
    )(*args)


def _rwkv_chunk_kernel(passes, r_ref, lw_ref, k_ref, v_ref, a_ref, b_ref, tri_ref, y_ref, so_ref, s_scr):
    c = pl.program_id(1)

    @pl.when(c == 0)
    def _():
        s_scr[...] = jnp.zeros_like(s_scr)

    bb, cl = r_ref.shape[2], r_ref.shape[3]
    r_ref, lw_ref, k_ref, v_ref, a_ref, b_ref, y_ref = (ref.at[0, 0] for ref in
                                                        (r_ref, lw_ref, k_ref, v_ref, a_ref, b_ref, y_ref))
    mm = functools.partial(_mm, passes=passes)
    ti = lax.broadcasted_iota(jnp.int32, (cl, cl), 0)
    si = lax.broadcasted_iota(jnp.int32, (cl, cl), 1)
    strict = (ti > si).astype(F32)
    incl = (ti >= si).astype(F32)
    eye = (ti == si).astype(F32)

    def heads(x):
        return jnp.stack([x[bi][:, h * RWKV_N:(h + 1) * RWKV_N] for bi in range(bb) for h in range(RWKV_HEADS)])

    lw = lw_ref[...]
    cum = jnp.stack([_dot_hi(tri_ref[...], lw[bi]) for bi in range(bb)])
    last = cum[:, cl - 1:cl, :]
    e_neg = jnp.exp(-cum)
    e_end = jnp.exp(last - cum)
    at = heads(a_ref[...] * jnp.exp(cum - lw))
    rt = heads(r_ref[...] * jnp.exp(cum))
    bt = heads(b_ref[...] * e_neg)
    kt = heads(k_ref[...] * e_neg)
    bw = heads(b_ref[...] * e_end)
    kw = heads(k_ref[...] * e_end)
    wc = heads(jnp.exp(last))
    vh = heads(v_ref[...])

    lhs = jnp.concatenate([at, rt], 1)
    gram = mm(lhs, jnp.concatenate([bt, kt], 1), 'gik,gjk->gij')
    a_ab = gram[:, :cl, :cl] * strict
    a_ak = gram[:, :cl, cl:] * strict
    a_rb = gram[:, cl:, :cl] * incl
    a_rk = gram[:, cl:, cl:] * incl
    inv = eye + a_ab
    pw = a_ab
    for _ in range(cl.bit_length() - 2):
        pw = mm(pw, pw, 'gij,gjk->gik')
        inv = inv + mm(inv, pw, 'gij,gjk->gik')
    s0 = s_scr[...].reshape(bb * RWKV_HEADS, RWKV_N, RWKV_N)
    xs = mm(lhs, s0, 'gtj,gij->gti')
    av = mm(jnp.concatenate([a_ak, a_rk], 1), vh, 'gts,gsi->gti')
    u = mm(inv, xs[:, :cl] + av[:, :cl], 'gts,gsi->gti')
    y = xs[:, cl:] + av[:, cl:] + mm(a_rb, u, 'gts,gsi->gti')
    s_new = s0 * wc + mm(jnp.concatenate([u, vh], 1), jnp.concatenate([bw, kw], 1), 'gti,gtj->gij')
    s_scr[...] = s_new.reshape(bb, RWKV_HEADS, RWKV_N, RWKV_N)
    for bi in range(bb):
        for h in range(RWKV_HEADS):
            y_ref[bi, :, h * RWKV_N:(h + 1) * RWKV_N] = y[bi * RWKV_HEADS + h]

    @pl.when(c == pl.num_programs(1) - 1)
    def _():
        so_ref[...] = s_scr[...]


def _rwkv_chunks(r, lw, k, v, a, b, passes):
    n_pairs, chunks, bb, cl, _ = r.shape
    n_pairs -= 1
    seq = pl.BlockSpec((1, 1, bb, cl, RWKV_W), lambda pi, c: (pi, c, 0, 0, 0))
    s_spec = pl.BlockSpec((bb, RWKV_HEADS, RWKV_N, RWKV_N), lambda pi, c: (pi, 0, 0, 0))
    tri = (jnp.arange(cl)[:, None] >= jnp.arange(cl)[None, :]).astype(F32)
    return pl.pallas_call(
        functools.partial(_rwkv_chunk_kernel, passes),
        grid=(n_pairs, chunks),
        in_specs=[seq] * 6 + [pl.BlockSpec((cl, cl), lambda pi, c: (0, 0))],
        out_specs=[seq, s_spec],
        out_shape=[jax.ShapeDtypeStruct(r.shape, F32),
                   jax.ShapeDtypeStruct((n_pairs * bb, RWKV_HEADS, RWKV_N, RWKV_N), F32)],
        scratch_shapes=[pltpu.VMEM((bb, RWKV_HEADS, RWKV_N, RWKV_N), F32)],
        compiler_params=_params("parallel", "arbitrary"),
        name="rwkv_chunks",
    )(r, lw, k, v, a, b, tri)


def _rwkv_step_kernel(r_ref, lw_ref, k_ref, a_ref, b_ref, vt_ref, s0_ref, yt_ref, so_ref):
    s = s0_ref[...]
    for t in range(r_ref.shape[2]):
        row = lambda ref: ref[:, :, t:t + 1, :]
        sa = jnp.sum(s * row(a_ref), -1, keepdims=True)
        s = s * jnp.exp(row(lw_ref)) + sa * row(b_ref) + vt_ref[:, :, :, t:t + 1] * row(k_ref)
        yt_ref[:, :, :, t:t + 1] = jnp.sum(s * row(r_ref), -1, keepdims=True)
    so_ref[...] = s


def _rwkv_steps(r, lw, k, v, a, b, s0, y_pairs):
    bsz = s0.shape[0]
    t = TOKEN_TILE // bsz
    bb = RWKV_STEP_BATCH_BLOCK
    tile_rows = TOKEN_TILE // RWKV_CHUNK
    steps = lambda x: x[-1, :tile_rows, 0].reshape(t, bsz, RWKV_HEADS, RWKV_N)
    rows = lambda x: steps(x).transpose(1, 2, 0, 3)
    vt = steps(v).transpose(1, 2, 3, 0)
    row_spec = pl.BlockSpec((bb, RWKV_HEADS, t, RWKV_N), lambda i: (i, 0, 0, 0))
    col_spec = pl.BlockSpec((bb, RWKV_HEADS, RWKV_N, t), lambda i: (i, 0, 0, 0))
    s_spec = pl.BlockSpec((bb, RWKV_HEADS, RWKV_N, RWKV_N), lambda i: (i, 0, 0, 0))
    yt, s_new = pl.pallas_call(
        _rwkv_step_kernel,
        grid=(bsz // bb,),
        in_specs=[row_spec] * 5 + [col_spec, s_spec],
        out_specs=[col_spec, s_spec],
        out_shape=[jax.ShapeDtypeStruct((bsz, RWKV_HEADS, RWKV_N, t), F32),
                   jax.ShapeDtypeStruct((bsz, RWKV_HEADS, RWKV_N, RWKV_N), F32)],
        compiler_params=_params("parallel"),
        name="rwkv_steps",
    )(rows(r), rows(lw), rows(k), rows(a), rows(b), vt, s0)
    y_tile = yt.transpose(3, 0, 1, 2).reshape(tile_rows, RWKV_CHUNK, RWKV_W)
    return y_pairs.at[-1, :tile_rows, 0].set(y_tile), s_new


def _post_kernel(n_w, x_ref, yrw_ref, r_ref, k_ref, v_ref, g_ref, yret_ref, ga_ref, gb_ref, bd_ref, rk_ref, gng_ref,
                 gnb_ref, ln_g_ref, ln_b_ref, wr_ref, br_ref, *rest):
    wret, wrw, wo = rest[:n_w], rest[n_w:2 * n_w], rest[2 * n_w:3 * n_w]
    x1_ref, xb_ref, ids_ref, probs_ref = rest[3 * n_w:]
    bd = bd_ref[...]
    y = _from_pair_tile(yrw_ref)
    mu = _head_sum(y, bd) * (1.0 / RWKV_N)
    d = y - mu
    var = _head_sum(d * d, bd) * (1.0 / RWKV_N)
    yn = d * lax.rsqrt(var + RWKV_GN_EPS) * gng_ref[...] + gnb_ref[...]
    bonus = _head_sum(_from_pair_tile(r_ref) * _from_pair_tile(k_ref) * rk_ref[...], bd) * _from_pair_tile(v_ref)
    yb = (yn + bonus) * _from_pair_tile(g_ref)
    merged = _sigmoid(ga_ref[...]) * _mm_w(yret_ref[...], wret) + _sigmoid(gb_ref[...]) * _mm_w(yb, wrw)
    out = _mm_w(merged, wo)
    x1 = _layer_norm(DN_ALPHA * x_ref[...] + out, ln_g_ref[...], ln_b_ref[...])
    x1_ref[...] = x1
    xb_ref[...] = x1.astype(BF16)

    logits = _mm(x1, wr_ref[...], 'ik,kj->ij', EXACT_PASSES) + br_ref[...]
    lane = lax.broadcasted_iota(jnp.int32, logits.shape, 1)
    work = logits
    ids, vals = [], []
    for _ in range(TOP_K):
        m = jnp.max(work, -1, keepdims=True)
        idx = jnp.min(jnp.where(work == m, lane, N_EXPERTS), -1, keepdims=True)
        ids.append(idx)
        vals.append(m)
        work = jnp.where(lane == idx, -jnp.inf, work)
    exps = [jnp.exp(m - vals[0]) for m in vals]
    inv_den = 1.0 / sum(exps)
    slot = lax.broadcasted_iota(jnp.int32, ids_ref.shape, 1)
    ids_out = jnp.zeros(ids_ref.shape, jnp.int32)
    probs_out = jnp.zeros(probs_ref.shape, F32)
    for j in range(TOP_K):
        ids_out = jnp.where(slot == j, ids[j], ids_out)
        probs_out = jnp.where(slot == j, exps[j] * inv_den, probs_out)
    ids_ref[...] = ids_out
    probs_ref[...] = probs_out


def _post(x, yrw, r, k, v, g, yret, gates, seq_len, p, l, passes):
    n = x.shape[0]
    tm = TOKEN_TILE
    row = lambda a: a.reshape(1, -1)
    tile = lambda w, j=0: pl.BlockSpec((tm, w), lambda i: (i, j))
    full = lambda a: pl.BlockSpec(a.shape, lambda i: (0,) * a.ndim)
    pair = _pair_tile_spec(seq_len // tm, RWKV_W)
    weights = (_split_weight(p['w_ret_out'][l], passes) + _split_weight(p['w_rwkv_out'][l], passes)
               + _split_weight(p['w_o'][l], passes))
    consts = [_head_blockdiag(), row(p['rw_r_k'][l]), row(p['rw_gn_g'][l]), row(p['rw_gn_b'][l]),
              row(p['ln1_g'][l]), row(p['ln1_b'][l]), p['w_router'][l], row(p['b_router'][l]), *weights]
    in_specs = ([tile(D_MODEL)] + [pair] * 5 + [tile(RET_V_W), tile(D_MODEL, 0), tile(D_MODEL, 1)]
                + [full(a) for a in consts])
    return pl.pallas_call(
        functools.partial(_post_kernel, len(weights) // 3),
        grid=(n // tm,),
        in_specs=in_specs,
        out_specs=[tile(D_MODEL), tile(D_MODEL), tile(TOP_K), tile(TOP_K)],
        out_shape=[jax.ShapeDtypeStruct((n, D_MODEL), F32), jax.ShapeDtypeStruct((n, D_MODEL), BF16),
                   jax.ShapeDtypeStruct((n, TOP_K), jnp.int32), jax.ShapeDtypeStruct((n, TOP_K), F32)],
        compiler_params=_params("parallel"),
        name="merge_ln_router",
    )(x, yrw, r, k, v, g, yret, gates, gates, *consts)


SEG_ALIGN = 8
ROUTE_TILE = 512
SEG_BITS = (ROUTE_TILE // SEG_ALIGN).bit_length()
LOCAL_ROWS = 2304
PROB_LANES = 128
ROW_W = D_MODEL + PROB_LANES


def _segment_dmas(i, cnt_ref, lst_ref, gst_ref, make_copy, act):
    def per_expert(e, carry):
        j = i * N_EXPERTS + e
        groups = cnt_ref[j] // SEG_ALIGN
        done = jnp.int32(0)
        for bit in reversed(range(SEG_BITS)):
            size = SEG_ALIGN << bit
            has = ((groups >> bit) & 1) == 1
            lo = pl.multiple_of(lst_ref[j] + done, SEG_ALIGN)
            go = pl.multiple_of(gst_ref[j] + done, SEG_ALIGN)

            @pl.when(has)
            def _():
                act(make_copy(lo, go, size))

            done = done + jnp.where(has, size, 0)
        return carry

    lax.fori_loop(0, N_EXPERTS, per_expert, 0)


def _dispatch_kernel(cnt_ref, lst_ref, gst_ref, ids_ref, probs_ref, lstart_ref, xb_ref, xs_hbm, buf, sem):
    i = pl.program_id(0)
    tm = xb_ref.shape[0]
    ids = ids_ref[0]
    probs = probs_ref[0]
    expert = lax.broadcasted_iota(jnp.int32, (N_EXPERTS, tm), 0)
    picks = [ids[k:k + 1, :] == expert for k in range(TOP_K)]
    picked = sum(pk.astype(F32) for pk in picks)
    m = lax.broadcasted_iota(jnp.int32, (tm, tm), 0)
    n = lax.broadcasted_iota(jnp.int32, (tm, tm), 1)
    earlier = (m < n).astype(BF16)
    rank = jnp.dot(picked.astype(BF16), earlier, preferred_element_type=F32)
    pos = lstart_ref[0] + rank
    row = lax.broadcasted_iota(jnp.int32, (LOCAL_ROWS, tm), 0)
    perm = jnp.zeros((LOCAL_ROWS, tm), F32)
    weight = jnp.zeros((LOCAL_ROWS, tm), F32)
    for k in range(TOP_K):
        lpos = jnp.sum(jnp.where(picks[k], pos, 0.0), 0, keepdims=True).astype(jnp.int32)
        hit = row == lpos
        perm = jnp.where(hit, 1.0, perm)
        weight = jnp.where(hit, probs[k:k + 1, :], weight)
    rows = jnp.dot(perm.astype(BF16), xb_ref[...], preferred_element_type=F32)
    row_weight = jnp.broadcast_to(jnp.sum(weight, 1, keepdims=True), (LOCAL_ROWS, PROB_LANES))

    def copies_of(slot):
        return lambda lo, go, size: pltpu.make_async_copy(buf.at[slot, pl.ds(lo, size)], xs_hbm.at[pl.ds(go, size)],
                                                          sem.at[slot])

    tables = (cnt_ref, lst_ref, gst_ref)
    slot = i % 2

    @pl.when(i >= 2)
    def _():
        _segment_dmas(i - 2, *tables, copies_of(slot), lambda c: c.wait())

    buf[slot, :, :D_MODEL] = rows
    buf[slot, :, D_MODEL:] = row_weight
    _segment_dmas(i, *tables, copies_of(slot), lambda c: c.start())

    @pl.when(i == pl.num_programs(0) - 1)
    def _():
        @pl.when(i >= 1)
        def _():
            _segment_dmas(i - 1, *tables, copies_of(1 - slot), lambda c: c.wait())

        _segment_dmas(i, *tables, copies_of(slot), lambda c: c.wait())


def _round_up(x, m):
    return (x + m - 1) // m * m


def _routing_tables(ids):
    n = ids.shape[0]
    nt = n // ROUTE_TILE
    picked = jnp.sum(ids[:, :, None] == jnp.arange(N_EXPERTS, dtype=jnp.int32)[None, None, :], 1)
    cnt = jnp.sum(picked.reshape(nt, ROUTE_TILE, N_EXPERTS), 1).astype(jnp.int32)
    cnt = _round_up(cnt, SEG_ALIGN)
    lstart = jnp.cumsum(cnt, 1) - cnt
    per_expert = jnp.sum(cnt, 0)
    region = _round_up(per_expert, EXPERT_ROW_TILE)
    gstart = (jnp.cumsum(region) - region)[None, :] + jnp.cumsum(cnt, 0) - cnt
    return cnt, lstart.astype(jnp.int32), gstart.astype(jnp.int32), region


def _dispatch(xb, ids, probs, tables, n_rows):
    n = xb.shape[0]
    tm = ROUTE_TILE
    nt = n // tm
    cnt, lstart, gstart, _ = tables
    to_lanes = lambda a: a.reshape(nt, tm, TOP_K).transpose(0, 2, 1)
    grid_spec = pltpu.PrefetchScalarGridSpec(
        num_scalar_prefetch=3,
        grid=(nt,),
        in_specs=[pl.BlockSpec((1, TOP_K, tm), lambda i, *_: (i, 0, 0)), pl.BlockSpec((1, TOP_K, tm), lambda i, *_: (i, 0, 0)),
                  pl.BlockSpec((1, N_EXPERTS, 1), lambda i, *_: (i, 0, 0)), pl.BlockSpec((tm, D_MODEL), lambda i, *_: (i, 0))],
        out_specs=pl.BlockSpec(memory_space=pl.ANY),
        scratch_shapes=[pltpu.VMEM((2, LOCAL_ROWS, ROW_W), F32), pltpu.SemaphoreType.DMA((2,))],
    )
    return pl.pallas_call(
        _dispatch_kernel,
        grid_spec=grid_spec,
        out_shape=jax.ShapeDtypeStruct((n_rows, ROW_W), F32),
        compiler_params=_params("arbitrary"),
        name="dispatch",
    )(cnt.reshape(-1), lstart.reshape(-1), gstart.reshape(-1), to_lanes(ids), to_lanes(probs),
      lstart.astype(F32)[:, :, None], xb)


def _expert_kernel(tile_ref, exp_ref, valid_ref, x_ref, wgu_ref, bgu_ref, wd_ref, bd_ref, o_ref, wgu_s, wd_s):
    i = pl.program_id(0)

    @pl.when(valid_ref[i] == 1)
    def _():
        @pl.when((i == 0) | (exp_ref[jnp.maximum(i - 1, 0)] != exp_ref[i]))
        def _():
            wgu_s[...] = wgu_ref[0, 0].astype(BF16)
            wd_s[...] = wd_ref[0, 0].astype(BF16)

        h = jnp.dot(x_ref[:, :D_MODEL].astype(BF16), wgu_s[...], preferred_element_type=F32) + bgu_ref[0, 0]
        gate = jnp.minimum(h[:, :D_FF], SWIGLU_LIMIT)
        up = jnp.clip(h[:, D_FF:], -SWIGLU_LIMIT, SWIGLU_LIMIT)
        act = gate * _sigmoid(SWIGLU_ALPHA * gate) * (up + 1.0)
        y = jnp.dot(act.astype(BF16), wd_s[...], preferred_element_type=F32) + bd_ref[0, 0]
        o_ref[...] = y * x_ref[:, D_MODEL:D_MODEL + 1]


def _expert_schedule(region, n_rows, tm):
    n_entries = n_rows // tm
    tile_end = jnp.cumsum(region // tm)
    total = tile_end[-1]
    t = jnp.minimum(jnp.arange(n_entries), total - 1).astype(jnp.int32)
    e = jnp.sum(tile_end[None, :] <= t[:, None], -1).astype(jnp.int32)
    valid = (jnp.arange(n_entries) < total).astype(jnp.int32)
    return t, e, valid


def _experts(xs, sched, p, l):
    n_rows = xs.shape[0]
    tm = EXPERT_ROW_TILE
    bgu = p['b_gate_up'].reshape(DEPTH, N_EXPERTS, 1, 2 * D_FF)
    bdn = p['b_down'].reshape(DEPTH, N_EXPERTS, 1, D_MODEL)
    by_tile = lambda w: pl.BlockSpec((tm, w), lambda i, t, e, *_: (t[i], 0))
    by_expert = lambda a, b: pl.BlockSpec((1, 1, a, b), lambda i, t, e, *_: (l, e[i], 0, 0))
    grid_spec = pltpu.PrefetchScalarGridSpec(
        num_scalar_prefetch=3,
        grid=(sched[0].shape[0],),
        in_specs=[by_tile(ROW_W), by_expert(D_MODEL, 2 * D_FF), by_expert(1, 2 * D_FF),
                  by_expert(D_FF, D_MODEL), by_expert(1, D_MODEL)],
        out_specs=by_tile(D_MODEL),
        scratch_shapes=[pltpu.VMEM((D_MODEL, 2 * D_FF), BF16), pltpu.VMEM((D_FF, D_MODEL), BF16)],
    )
    return pl.pallas_call(
        _expert_kernel,
        grid_spec=grid_spec,
        out_shape=jax.ShapeDtypeStruct((n_rows, D_MODEL), F32),
        compiler_params=_params("arbitrary"),
        name="experts",
    )(*sched, xs, p['w_gate_up'], bgu, p['w_down'], bdn)


def _combine_kernel(cnt_ref, lst_ref, gst_ref, ids_ref, lstart_ref, x_ref, ln_g_ref, ln_b_ref, ys_hbm, o_ref, buf, sem):
    i = pl.program_id(0)
    tm = x_ref.shape[0]
    tables = (cnt_ref, lst_ref, gst_ref)
    slot = i % 2

    def copies_of(slot):
        return lambda lo, go, size: pltpu.make_async_copy(ys_hbm.at[pl.ds(go, size)], buf.at[slot, pl.ds(lo, size)],
                                                          sem.at[slot])

    def fetch(tile, slot):
        buf[slot, ROUTE_TILE * TOP_K:, :] = jnp.zeros((LOCAL_ROWS - ROUTE_TILE * TOP_K, D_MODEL), F32)
        _segment_dmas(tile, *tables, copies_of(slot), lambda c: c.start())

    @pl.when(i == 0)
    def _():
        fetch(i, slot)

    @pl.when(i + 1 < pl.num_programs(0))
    def _():
        fetch(i + 1, 1 - slot)

    ids = ids_ref[...]
    expert = lax.broadcasted_iota(jnp.int32, (tm, N_EXPERTS), 1)
    picks = [ids[:, k:k + 1] == expert for k in range(TOP_K)]
    picked = sum(pk.astype(F32) for pk in picks)
    m = lax.broadcasted_iota(jnp.int32, (tm, tm), 0)
    n = lax.broadcasted_iota(jnp.int32, (tm, tm), 1)
    earlier = (n < m).astype(BF16)
    rank = jnp.dot(earlier, picked.astype(BF16), preferred_element_type=F32)
    pos = lstart_ref[0] + rank
    col = lax.broadcasted_iota(jnp.int32, (tm, LOCAL_ROWS), 1)
    perm = jnp.zeros((tm, LOCAL_ROWS), F32)
    for k in range(TOP_K):
        lpos = jnp.sum(jnp.where(picks[k], pos, 0.0), 1, keepdims=True).astype(jnp.int32)
        perm = jnp.where(col == lpos, 1.0, perm)

    _segment_dmas(i, *tables, copies_of(slot), lambda c: c.wait())
    moe = jnp.dot(perm.astype(BF16), buf[slot].astype(BF16), preferred_element_type=F32)
    o_ref[...] = _layer_norm(DN_ALPHA * x_ref[...] + moe, ln_g_ref[...], ln_b_ref[...])


def _combine(x1, ys, ids, tables, ln_g, ln_b):
    n = x1.shape[0]
    tm = ROUTE_TILE
    nt = n // tm
    cnt, lstart, gstart, _ = tables
    grid_spec = pltpu.PrefetchScalarGridSpec(
        num_scalar_prefetch=3,
        grid=(nt,),
        in_specs=[pl.BlockSpec((tm, TOP_K), lambda i, *_: (i, 0)), pl.BlockSpec((1, 1, N_EXPERTS), lambda i, *_: (i, 0, 0)),
                  pl.BlockSpec((tm, D_MODEL), lambda i, *_: (i, 0)), pl.BlockSpec((1, D_MODEL), lambda i, *_: (0, 0)),
                  pl.BlockSpec((1, D_MODEL), lambda i, *_: (0, 0)), pl.BlockSpec(memory_space=pl.ANY)],
        out_specs=pl.BlockSpec((tm, D_MODEL), lambda i, *_: (i, 0)),
        scratch_shapes=[pltpu.VMEM((2, LOCAL_ROWS, D_MODEL), F32), pltpu.SemaphoreType.DMA((2,))],
    )
    return pl.pallas_call(
        _combine_kernel,
        grid_spec=grid_spec,
        out_shape=jax.ShapeDtypeStruct((n, D_MODEL), F32),
        compiler_params=_params("arbitrary"),
        name="combine_ln",
    )(cnt.reshape(-1), lstart.reshape(-1), gstart.reshape(-1), ids, lstart.astype(F32)[:, None, :], x1,
      ln_g.reshape(1, -1), ln_b.reshape(1, -1), ys)


def _moe(x1, xb, ids, probs, p, l):
    n = x1.shape[0]
    tm = EXPERT_ROW_TILE
    seg_rows = n * TOP_K + (n // ROUTE_TILE) * N_EXPERTS * (SEG_ALIGN - 1)
    n_rows = _round_up(seg_rows + N_EXPERTS * (tm - SEG_ALIGN), tm)
    tables = _routing_tables(ids)
    xs = _dispatch(xb, ids, probs, tables, n_rows)
    ys = _experts(xs, _expert_schedule(tables[3], n_rows, tm), p, l)
    return _combine(x1, ys, ids, tables, p['ln2_g'][l], p['ln2_b'][l])


def _pad_time(a, tp):
    return jnp.pad(a, ((0, 0), (0, tp - a.shape[1]), (0, 0)))


def kernel(x_prompt, x_sample, state_ret, state_rwkv, state_shift, w_in, ret_gn_g, ret_gn_b, w_ret_out, rw_mu, rw_w0, rw_w_up, rw_a0, rw_a_up, rw_g_up, rw_k_k, rw_k_a, rw_r_k, rw_gn_g, rw_gn_b, rw_v0, rw_vres_down, rw_vres_up, w_rwkv_out, w_o, ln1_g, ln1_b, w_router, b_router, w_gate_up, b_gate_up, w_down, b_down, ln2_g, ln2_b):
    p = dict(w_ret_out=w_ret_out, rw_mu=rw_mu, rw_w0=rw_w0, rw_w_up=rw_w_up, rw_a0=rw_a0, rw_a_up=rw_a_up,
             rw_g_up=rw_g_up, rw_k_k=rw_k_k, rw_k_a=rw_k_a, rw_r_k=rw_r_k, rw_gn_g=rw_gn_g, rw_gn_b=rw_gn_b,
             w_rwkv_out=w_rwkv_out, w_o=w_o, ln1_g=ln1_g, ln1_b=ln1_b, w_router=w_router, b_router=b_router,
             w_gate_up=w_gate_up, b_gate_up=b_gate_up, w_down=w_down, b_down=b_down, ln2_g=ln2_g, ln2_b=ln2_b)
    bp, tp, _ = x_prompt.shape
    bs, ts, _ = x_sample.shape
    np_, ns = bp * tp, bs * ts
    pos_p = jnp.arange(tp, dtype=F32)
    ts_ret = 8
    pos_s = PAST_LEN + jnp.arange(ts_ret, dtype=F32)

    x = jnp.concatenate([x_prompt.reshape(np_, D_MODEL), x_sample.transpose(1, 0, 2).reshape(ns, D_MODEL)], 0)
    outs = {k: [] for k in ('ret_p', 'rw_p', 'sh_p', 'ret_s', 'rw_s', 'sh_s')}
    v_first = None
    u_off = RET_W
    g_off = RET_W + SHIFT_W
    for l in range(DEPTH):
        passes = EXACT_PASSES if l == 0 else 1
        z_ret = _matmul(x, w_in[l][:, :u_off], TOKEN_TILE, RET_W // 2, passes)
        u = _matmul(x, w_in[l][:, u_off:g_off], TOKEN_TILE, SHIFT_W, passes)
        gates = _matmul(x, w_in[l][:, g_off:], TOKEN_TILE, D_MODEL, passes)

        zr_s = _pad_time(z_ret[np_:].reshape(ts, bs, RET_W).transpose(1, 0, 2), ts_ret)
        yret, sret_p = _retention(z_ret.reshape(-1, RET_CHUNK, RET_W), pos_p, tp, None, ret_gn_g[l], ret_gn_b[l], 1,
                                  passes, n_seq=bp)
        yret_s, sret_s = _retention(zr_s, pos_s, ts, state_ret[l], ret_gn_g[l], ret_gn_b[l], 8, passes)
        yret = yret.reshape(-1, RET_V_W).at[np_:].set(yret_s[:, :ts].transpose(1, 0, 2).reshape(ns, RET_V_W))

        vres = None if l == 0 else (rw_v0[l - 1], rw_vres_down[l - 1], rw_vres_up[l - 1])
        r, lw, k, v, a, b, g = _rwkv_pre(u, state_shift[l], np_, tp, p, l, v_first, vres)
        if l == 0:
            v_first = v
        yrw, srw_p = _rwkv_chunks(r, lw, k, v, a, b, passes)
        yrw, srw_s = _rwkv_steps(r, lw, k, v, a, b, state_rwkv[l], yrw)

        x1, xb, ids, probs = _post(x, yrw, r, k, v, g, yret, gates, tp, p, l, passes)
        x = _moe(x1, xb, ids, probs, p, l)

        outs['ret_p'].append(sret_p)
        outs['ret_s'].append(sret_s)
        outs['rw_p'].append(srw_p)
        outs['rw_s'].append(srw_s)
        outs['sh_p'].append(u[tp - 1:np_:tp])
        outs['sh_s'].append(u[np_ + ns - bs:])

    y_prompt = x[:np_].reshape(bp, tp, D_MODEL)
    y_sample = x[np_:].reshape(ts, bs, D_MODEL).transpose(1, 0, 2)
    st = {k: jnp.stack(v) for k, v in outs.items()}
    return (y_prompt, y_sample, st['ret_p'], st['rw_p'], st['sh_p'], st['ret_s'], st['rw_s'], st['sh_s'])
```

```python
import functools

import jax
import jax.numpy as jnp
from jax import lax
from jax.experimental import pallas as pl
from jax.experimental.pallas import tpu as pltpu

F32 = jnp.float32
BF16 = jnp.bfloat16
HI = lax.Precision.HIGHEST

D_MODEL = 1024
DEPTH = 2
PAST_LEN = 16384
RET_HEADS = 4
RET_DK = 128
RET_DV = 256
RET_QK_W = RET_HEADS * RET_DK
RET_V_W = RET_HEADS * RET_DV
RET_W = 2 * RET_QK_W + 2 * RET_V_W
RET_CHUNK = 128
ROPE_BASE = 10000.0
RWKV_HEADS = 8
RWKV_N = 64
RWKV_W = RWKV_HEADS * RWKV_N
LORA_W = 64
LORA_A = 64
LORA_G = 128
SHIFT_W = 3 * RWKV_W + LORA_W + LORA_A + LORA_G
RWKV_CHUNK = 64
N_EXPERTS = 32
TOP_K = 4
D_FF = D_MODEL
SWIGLU_LIMIT = 7.0
SWIGLU_ALPHA = 1.702
DN_ALPHA = (2 * DEPTH) ** 0.25
LN_EPS = 1e-5
RET_GN_EPS = 1e-5
RWKV_GN_EPS = 64e-5

VMEM_LIMIT = 56 * 1024 * 1024
TOKEN_TILE = 512
EXPERT_ROW_TILE = 512
RWKV_STEP_BATCH_BLOCK = 8
EXACT_PASSES = 3


def _params(*sem):
    return pltpu.CompilerParams(dimension_semantics=sem, vmem_limit_bytes=VMEM_LIMIT)


def _split(x):
    hi = x.astype(BF16)
    lo = (x - hi.astype(F32)).astype(BF16)
    return hi, lo


def _split_kernel(w_ref, hi_ref, lo_ref):
    hi_ref[...], lo_ref[...] = _split(w_ref[...])


def _split_weight(w, passes):
    if passes == 1:
        return (w.astype(BF16),)
    rows = 256
    spec = pl.BlockSpec((rows, w.shape[1]), lambda i: (i, 0))
    return tuple(pl.pallas_call(
        _split_kernel,
        grid=(w.shape[0] // rows,),
        in_specs=[spec],
        out_specs=[spec, spec],
        out_shape=[jax.ShapeDtypeStruct(w.shape, BF16)] * 2,
        compiler_params=_params("parallel"),
        name="split_weight",
    )(w))


def _mm(a, b, spec, passes):
    dg = lambda x, y: jnp.einsum(spec, x, y, preferred_element_type=F32)
    if passes == 1:
        return dg(a.astype(BF16), b.astype(BF16))
    ah, al = _split(a)
    bh, bl = _split(b)
    return dg(ah, bh) + (dg(ah, bl) + dg(al, bh))


def _mm_w(a, w_refs):
    dg = lambda x, y: jnp.dot(x, y, preferred_element_type=F32)
    if len(w_refs) == 1:
        return dg(a.astype(BF16), w_refs[0][...])
    ah, al = _split(a)
    return dg(ah, w_refs[0][...]) + (dg(ah, w_refs[1][...]) + dg(al, w_refs[0][...]))


def _dot_hi(a, b):
    return jnp.dot(a, b, precision=HI, preferred_element_type=F32)


def _sigmoid(x):
    return 1.0 / (1.0 + jnp.exp(-x))


def _layer_norm(x, g, b):
    mu = jnp.mean(x, -1, keepdims=True)
    d = x - mu
    var = jnp.mean(d * d, -1, keepdims=True)
    return d * lax.rsqrt(var + LN_EPS) * g + b


def _token_specs(tokens, tm, w, tile_of):
    _, _, b_block, n_a = tokens
    return [pl.BlockSpec((tm, w), lambda *g: (jnp.minimum(tile_of(*g), n_a - 1), 0)),
            pl.BlockSpec((tm, w), lambda *g: (b_block, 0))]


def _token_tile(tokens, i, a_ref, b_ref):
    return jnp.where(i < tokens[3], a_ref[...], b_ref[...])


def _matmul_kernel(tokens, xa_ref, xb_ref, *refs):
    o_ref = refs[-1]
    o_ref[...] = _mm_w(_token_tile(tokens, pl.program_id(1), xa_ref, xb_ref), refs[:-1])


def _matmul(tokens, w, tm, tn, passes):
    k, n = w.shape
    n_tiles = tokens[3] + 1
    ws = _split_weight(w, passes)
    return pl.pallas_call(
        functools.partial(_matmul_kernel, (None, None) + tokens[2:]),
        grid=(n // tn, n_tiles),
        in_specs=_token_specs(tokens, tm, k, lambda j, i: i) + [pl.BlockSpec((k, tn), lambda j, i: (0, j))] * len(ws),
        out_specs=pl.BlockSpec((tm, tn), lambda j, i: (i, j)),
        out_shape=jax.ShapeDtypeStruct((n_tiles * tm, n), F32),
        compiler_params=_params("parallel", "parallel"),
        name="in_proj",
    )(tokens[0], tokens[1], *ws)


def _ret_kernel(has_state, passes, q_ref, k_ref, v_ref, g_ref, cos_ref, sin_ref, dm_ref, qd_ref, kd_ref, cd_ref,
                gng_ref, gnb_ref, *rest):
    if has_state:
        s0_ref, y_ref, so_ref, s_scr = rest
    else:
        y_ref, so_ref, s_scr = rest
    c = pl.program_id(1)
    bb, cl, _ = q_ref.shape
    nh = RET_HEADS

    def heads(x, w):
        return jnp.concatenate([x[:, :, h * w:(h + 1) * w] for h in range(nh)], 0)

    def per_head(ref):
        return jnp.concatenate([jnp.broadcast_to(ref[h], (bb,) + ref.shape[1:]) for h in range(nh)], 0)

    @pl.when(c == 0)
    def _():
        if has_state:
            s_scr[...] = jnp.concatenate([s0_ref[:, h] for h in range(nh)], 0)
        else:
            s_scr[...] = jnp.zeros_like(s_scr)

    cos = cos_ref[...]
    sin = sin_ref[...]
    mm = functools.partial(_mm, passes=passes)

    def rope(x):
        x2 = x.reshape(nh * bb * cl, RET_DK)
        rot = pltpu.roll(x2, RET_DK // 2, axis=1).reshape(nh * bb, cl, RET_DK)
        return x * cos + rot * sin

    q = rope(heads(q_ref[...], RET_DK))
    k = rope(heads(k_ref[...], RET_DK)) * (RET_DK ** -0.5)
    v = heads(v_ref[...], RET_DV)
    s = s_scr[...]
    sc = mm(q, k, 'bid,bjd->bij') * per_head(dm_ref)
    intra = mm(sc, v, 'bij,bje->bie')
    cross = mm(q, s, 'bid,bde->bie') * per_head(qd_ref)
    s_new = s * per_head(cd_ref) + mm(k * per_head(kd_ref), v, 'bjd,bje->bde')
    s_scr[...] = s_new

    y = intra + cross
    mu = jnp.mean(y, -1, keepdims=True)
    d = y - mu
    var = jnp.mean(d * d, -1, keepdims=True)
    yn = d * lax.rsqrt(var + RET_GN_EPS)
    for h in range(nh):
        cols = slice(h * RET_DV, (h + 1) * RET_DV)
        rg = g_ref[:, :, cols]
        y_ref[:, :, cols] = (yn[h * bb:(h + 1) * bb] * gng_ref[:, cols] + gnb_ref[:, cols]) * (rg * _sigmoid(rg))

    @pl.when(c == pl.num_programs(1) - 1)
    def _():
        for h in range(nh):
            so_ref[:, h] = s_new[h * bb:(h + 1) * bb]


def _retention(z, pos, t_real, s0, gn_g, gn_b, bb, passes, n_seq=None):
    flat = n_seq is not None
    tp = pos.shape[0]
    b = n_seq if flat else z.shape[0]
    cl = RET_CHUNK if t_real % RET_CHUNK == 0 else tp
    cr = min(cl, t_real)
    nc = tp // cl
    at = (lambda bi, c: (bi * nc + c, 0)) if flat else (lambda bi, c: (bi, c))
    half = RET_DK // 2
    inv = ROPE_BASE ** (-jnp.arange(half, dtype=F32) / half)
    ang = pos[:, None] * inv[None, :]
    cos = jnp.concatenate([jnp.cos(ang), jnp.cos(ang)], -1)
    sin = jnp.concatenate([-jnp.sin(ang), jnp.sin(ang)], -1)
    lg = jnp.log1p(-jnp.exp2(-5.0 - jnp.arange(RET_HEADS, dtype=F32)))
    i = jnp.arange(cl, dtype=F32)
    real = i < cr
    diff = i[:, None] - i[None, :]
    ok = (diff >= 0) & real[:, None] & real[None, :]
    dmask = jnp.exp(jnp.where(ok[None], diff[None] * lg[:, None, None], -jnp.inf))
    q_dec = jnp.exp((i[None, :] + 1.0) * lg[:, None])[..., None]
    k_dec = jnp.where(real[None, :], jnp.exp((cr - 1.0 - i)[None, :] * lg[:, None]), 0.0)[..., None]
    c_dec = jnp.exp(cr * lg)[:, None, None]

    has_state = s0 is not None
    cols = lambda w, j: pl.BlockSpec((bb, cl, w), lambda bi, c: at(bi, c) + (j,))
    full = lambda a: pl.BlockSpec(a.shape, lambda bi, c: (0,) * a.ndim)
    s_spec = pl.BlockSpec((bb, RET_HEADS, RET_DK, RET_DV), lambda bi, c: (bi, 0, 0, 0))
    consts = [dmask, q_dec, k_dec, c_dec, gn_g.reshape(1, -1), gn_b.reshape(1, -1)]
    in_specs = [cols(RET_QK_W, 0), cols(RET_QK_W, 1), cols(RET_V_W, 1), cols(RET_V_W, 2),
                pl.BlockSpec((cl, RET_DK), lambda bi, c: (c, 0)), pl.BlockSpec((cl, RET_DK), lambda bi, c: (c, 0))]
    in_specs += [full(a) for a in consts]
    args = [z, z, z, z, cos, sin] + consts
    if has_state:
        in_specs.append(s_spec)
        args.append(s0)
    return pl.pallas_call(
        functools.partial(_ret_kernel, has_state, passes),
        grid=(b // bb, nc),
        in_specs=in_specs,
        out_specs=[cols(RET_V_W, 0), s_spec],
        out_shape=[jax.ShapeDtypeStruct(z.shape[:2] + (RET_V_W,), F32),
                   jax.ShapeDtypeStruct((b, RET_HEADS, RET_DK, RET_DV), F32)],
        scratch_shapes=[pltpu.VMEM((RET_HEADS * bb, RET_DK, RET_DV), F32)],
        compiler_params=_params("parallel", "arbitrary"),
        name="retention",
    )(*args)


def _head_sum(x, bd):
    hi, lo = _split(x)
    return jnp.dot(hi, bd, preferred_element_type=F32) + jnp.dot(lo, bd, preferred_element_type=F32)


def _pair_shape(n_pairs, chunks, w):
    return (n_pairs + 1, chunks, 2, RWKV_CHUNK, w)


def _pair_tile_spec(tiles_per_seq, w):
    rows = TOKEN_TILE // RWKV_CHUNK
    return pl.BlockSpec((1, rows, 1, RWKV_CHUNK, w),
                        lambda i: (i // (2 * tiles_per_seq), i % tiles_per_seq, (i // tiles_per_seq) % 2, 0, 0))


def _to_pair_tile(ref, x):
    ref[0, :, 0] = x.reshape(TOKEN_TILE // RWKV_CHUNK, RWKV_CHUNK, x.shape[-1])


def _from_pair_tile(ref):
    return ref[0, :, 0].reshape(TOKEN_TILE, ref.shape[-1])


def _rwkv_pre_kernel(has_vres, n_prompt_tiles, tiles_per_seq, u_ref, tail_ref, shift_ref, mu_ref, w0_ref, wup_ref,
                     a0_ref, aup_ref, gup_ref, kk_ref, ka_ref, bd_ref, *rest):
    if has_vres:
        vf_ref, v0_ref, vd_ref, vu_ref, r_o, lw_o, k_o, v_o, a_o, b_o, g_o = rest
    else:
        r_o, lw_o, k_o, v_o, a_o, b_o, g_o = rest
    i = pl.program_id(0)
    u = u_ref[...]
    tm = u.shape[0]
    row = lax.broadcasted_iota(jnp.int32, (tm, 1), 0)
    before = jnp.where(i % tiles_per_seq == 0, 0.0, tail_ref[tail_ref.shape[0] - 1:, :])
    prev_prompt = jnp.where(row == 0, before, pltpu.roll(u, 1, axis=0))
    n_seq = shift_ref.shape[0]
    prev_sample = jnp.concatenate([shift_ref[...], u[:tm - n_seq]], 0)
    prev = jnp.where(i < n_prompt_tiles, prev_prompt, prev_sample)
    um = u + (prev - u) * mu_ref[...]
    w1, w2, w3 = RWKV_W, 2 * RWKV_W, 3 * RWKV_W
    r = um[:, :w1]
    kw = um[:, w1:w2]
    vw = um[:, w2:w3]
    wd = um[:, w3:w3 + LORA_W]
    ad = um[:, w3 + LORA_W:w3 + LORA_W + LORA_A]
    gd = um[:, w3 + LORA_W + LORA_A:]
    lora = lambda x, w_ref: _mm(x, w_ref[...], 'ik,kj->ij', EXACT_PASSES)
    xw = w0_ref[...] + lora(jnp.tanh(wd), wup_ref)
    softplus = jnp.maximum(-xw, 0.0) + jnp.log1p(jnp.exp(-jnp.abs(xw)))
    _to_pair_tile(lw_o, -jnp.exp(-softplus - 0.5))
    a = _sigmoid(a0_ref[...] + lora(ad, aup_ref))
    _to_pair_tile(g_o, lora(_sigmoid(gd), gup_ref))
    if has_vres:
        gate = _sigmoid(v0_ref[...] + lora(lora(vw, vd_ref), vu_ref))
        vw = vw + (_from_pair_tile(vf_ref) - vw) * gate
    kk = kw * kk_ref[...]
    norm = jnp.sqrt(_head_sum(kk * kk, bd_ref[...]))
    kk = kk / jnp.maximum(norm, 1e-12)
    _to_pair_tile(r_o, r)
    _to_pair_tile(k_o, kw * (1.0 + (a - 1.0) * ka_ref[...]))
    _to_pair_tile(v_o, vw)
    _to_pair_tile(a_o, -kk)
    _to_pair_tile(b_o, kk * a)


def _head_blockdiag():
    h = jnp.arange(RWKV_W) // RWKV_N
    return (h[:, None] == h[None, :]).astype(BF16)


def _rwkv_pre(u, shift_state, n_prompt, seq_len, p, l, v_first, vres):
    n = u.shape[0]
    tm = TOKEN_TILE
    assert n == n_prompt + tm and seq_len % tm == 0
    tiles_per_seq = seq_len // tm
    tail_rows = 8
    row = lambda a: a.reshape(1, -1)
    full = lambda a: pl.BlockSpec(a.shape, lambda i: (0,) * a.ndim)
    pair = _pair_tile_spec(tiles_per_seq, RWKV_W)
    has_vres = vres is not None
    args = [u, u, shift_state, row(p['rw_mu'][l]), row(p['rw_w0'][l]), p['rw_w_up'][l], row(p['rw_a0'][l]),
            p['rw_a_up'][l], p['rw_g_up'][l], row(p['rw_k_k'][l]), row(p['rw_k_a'][l]), _head_blockdiag()]
    in_specs = [pl.BlockSpec((tm, SHIFT_W), lambda i: (i, 0)),
                pl.BlockSpec((tail_rows, SHIFT_W), lambda i: (jnp.maximum(i * (tm // tail_rows) - 1, 0), 0))]
    in_specs += [full(a) for a in args[2:]]
    if has_vres:
        extra = [v_first, row(vres[0]), vres[1], vres[2]]
        in_specs += [pair] + [full(a) for a in extra[1:]]
        args += extra
    shape = _pair_shape(n_prompt // seq_len // 2, seq_len // RWKV_CHUNK, RWKV_W)
    return pl.pallas_call(
        functools.partial(_rwkv_pre_kernel, has_vres, n_prompt // tm, tiles_per_seq),
        grid=(n // tm,),
        in_specs=in_specs,
        out_specs=[pair] * 7,
        out_shape=[jax.ShapeDtypeStruct(shape, F32)] * 7,
        compiler_params=_params("parallel"),
        name="rwkv_pre",
    )(*args)


def _rwkv_chunk_kernel(passes, r_ref, lw_ref, k_ref, v_ref, a_ref, b_ref, tri_ref, y_ref, so_ref, s_scr):
    c = pl.program_id(1)

    @pl.when(c == 0)
    def _():
        s_scr[...] = jnp.zeros_like(s_scr)

    bb, cl = r_ref.shape[2], r_ref.shape[3]
    r_ref, lw_ref, k_ref, v_ref, a_ref, b_ref, y_ref = (ref.at[0, 0] for ref in
                                                        (r_ref, lw_ref, k_ref, v_ref, a_ref, b_ref, y_ref))
    mm = functools.partial(_mm, passes=passes)
    ti = lax.broadcasted_iota(jnp.int32, (cl, cl), 0)
    si = lax.broadcasted_iota(jnp.int32, (cl, cl), 1)
    strict = (ti > si).astype(F32)
    incl = (ti >= si).astype(F32)
    eye = (ti == si).astype(F32)

    def heads(x):
        return jnp.stack([x[bi][:, h * RWKV_N:(h + 1) * RWKV_N] for bi in range(bb) for h in range(RWKV_HEADS)])

    lw = lw_ref[...]
    cum = jnp.stack([_dot_hi(tri_ref[...], lw[bi]) for bi in range(bb)])
    last = cum[:, cl - 1:cl, :]
    e_neg = jnp.exp(-cum)
    e_end = jnp.exp(last - cum)
    at = heads(a_ref[...] * jnp.exp(cum - lw))
    rt = heads(r_ref[...] * jnp.exp(cum))
    bt = heads(b_ref[...] * e_neg)
    kt = heads(k_ref[...] * e_neg)
    bw = heads(b_ref[...] * e_end)
    kw = heads(k_ref[...] * e_end)
    wc = heads(jnp.exp(last))
    vh = heads(v_ref[...])

    lhs = jnp.concatenate([at, rt], 1)
    gram = mm(lhs, jnp.concatenate([bt, kt], 1), 'gik,gjk->gij')
    a_ab = gram[:, :cl, :cl] * strict
    a_ak = gram[:, :cl, cl:] * strict
    a_rb = gram[:, cl:, :cl] * incl
    a_rk = gram[:, cl:, cl:] * incl
    inv = eye + a_ab
    pw = a_ab
    for _ in range(cl.bit_length() - 2):
        pw = mm(pw, pw, 'gij,gjk->gik')
        inv = inv + mm(inv, pw, 'gij,gjk->gik')
    s0 = s_scr[...].reshape(bb * RWKV_HEADS, RWKV_N, RWKV_N)
    xs = mm(lhs, s0, 'gtj,gij->gti')
    av = mm(jnp.concatenate([a_ak, a_rk], 1), vh, 'gts,gsi->gti')
    u = mm(inv, xs[:, :cl] + av[:, :cl], 'gts,gsi->gti')
    y = xs[:, cl:] + av[:, cl:] + mm(a_rb, u, 'gts,gsi->gti')
    s_new = s0 * wc + mm(jnp.concatenate([u, vh], 1), jnp.concatenate([bw, kw], 1), 'gti,gtj->gij')
    s_scr[...] = s_new.reshape(bb, RWKV_HEADS, RWKV_N, RWKV_N)
    for bi in range(bb):
        for h in range(RWKV_HEADS):
            y_ref[bi, :, h * RWKV_N:(h + 1) * RWKV_N] = y[bi * RWKV_HEADS + h]

    @pl.when(c == pl.num_programs(1) - 1)
    def _():
        so_ref[...] = s_scr[...]


def _rwkv_chunks(r, lw, k, v, a, b, passes):
    n_pairs, chunks, bb, cl, _ = r.shape
    n_pairs -= 1
    seq = pl.BlockSpec((1, 1, bb, cl, RWKV_W), lambda pi, c: (pi, c, 0, 0, 0))
    s_spec = pl.BlockSpec((bb, RWKV_HEADS, RWKV_N, RWKV_N), lambda pi, c: (pi, 0, 0, 0))
    tri = (jnp.arange(cl)[:, None] >= jnp.arange(cl)[None, :]).astype(F32)
    return pl.pallas_call(
        functools.partial(_rwkv_chunk_kernel, passes),
        grid=(n_pairs, chunks),
        in_specs=[seq] * 6 + [pl.BlockSpec((cl, cl), lambda pi, c: (0, 0))],
        out_specs=[seq, s_spec],
        out_shape=[jax.ShapeDtypeStruct(r.shape, F32),
                   jax.ShapeDtypeStruct((n_pairs * bb, RWKV_HEADS, RWKV_N, RWKV_N), F32)],
        scratch_shapes=[pltpu.VMEM((bb, RWKV_HEADS, RWKV_N, RWKV_N), F32)],
        compiler_params=_params("parallel", "arbitrary"),
        name="rwkv_chunks",
    )(r, lw, k, v, a, b, tri)


def _rwkv_step_kernel(r_ref, lw_ref, k_ref, a_ref, b_ref, vt_ref, s0_ref, yt_ref, so_ref):
    s = s0_ref[...]
    for t in range(r_ref.shape[2]):
        row = lambda ref: ref[:, :, t:t + 1, :]
        sa = jnp.sum(s * row(a_ref), -1, keepdims=True)
        s = s * jnp.exp(row(lw_ref)) + sa * row(b_ref) + vt_ref[:, :, :, t:t + 1] * row(k_ref)
        yt_ref[:, :, :, t:t + 1] = jnp.sum(s * row(r_ref), -1, keepdims=True)
    so_ref[...] = s


def _rwkv_steps(r, lw, k, v, a, b, s0, y_pairs):
    bsz = s0.shape[0]
    t = TOKEN_TILE // bsz
    bb = RWKV_STEP_BATCH_BLOCK
    tile_rows = TOKEN_TILE // RWKV_CHUNK
    steps = lambda x: x[-1, :tile_rows, 0].reshape(t, bsz, RWKV_HEADS, RWKV_N)
    rows = lambda x: steps(x).transpose(1, 2, 0, 3)
    vt = steps(v).transpose(1, 2, 3, 0)
    row_spec = pl.BlockSpec((bb, RWKV_HEADS, t, RWKV_N), lambda i: (i, 0, 0, 0))
    col_spec = pl.BlockSpec((bb, RWKV_HEADS, RWKV_N, t), lambda i: (i, 0, 0, 0))
    s_spec = pl.BlockSpec((bb, RWKV_HEADS, RWKV_N, RWKV_N), lambda i: (i, 0, 0, 0))
    yt, s_new = pl.pallas_call(
        _rwkv_step_kernel,
        grid=(bsz // bb,),
        in_specs=[row_spec] * 5 + [col_spec, s_spec],
        out_specs=[col_spec, s_spec],
        out_shape=[jax.ShapeDtypeStruct((bsz, RWKV_HEADS, RWKV_N, t), F32),
                   jax.ShapeDtypeStruct((bsz, RWKV_HEADS, RWKV_N, RWKV_N), F32)],
        compiler_params=_params("parallel"),
        name="rwkv_steps",
    )(rows(r), rows(lw), rows(k), rows(a), rows(b), vt, s0)
    y_tile = yt.transpose(3, 0, 1, 2).reshape(tile_rows, RWKV_CHUNK, RWKV_W)
    return y_pairs.at[-1, :tile_rows, 0].set(y_tile), s_new


def _post_kernel(n_w, tokens, xa_ref, xb_in_ref, yrw_ref, r_ref, k_ref, v_ref, g_ref, yret_ref, ga_ref, gb_ref, bd_ref,
                 rk_ref, gng_ref, gnb_ref, ln_g_ref, ln_b_ref, wr_ref, br_ref, *rest):
    wret, wrw, wo = rest[:n_w], rest[n_w:2 * n_w], rest[2 * n_w:3 * n_w]
    x1_ref, xb_ref, ids_ref, probs_ref = rest[3 * n_w:]
    x = _token_tile(tokens, pl.program_id(0), xa_ref, xb_in_ref)
    bd = bd_ref[...]
    y = _from_pair_tile(yrw_ref)
    mu = _head_sum(y, bd) * (1.0 / RWKV_N)
    d = y - mu
    var = _head_sum(d * d, bd) * (1.0 / RWKV_N)
    yn = d * lax.rsqrt(var + RWKV_GN_EPS) * gng_ref[...] + gnb_ref[...]
    bonus = _head_sum(_from_pair_tile(r_ref) * _from_pair_tile(k_ref) * rk_ref[...], bd) * _from_pair_tile(v_ref)
    yb = (yn + bonus) * _from_pair_tile(g_ref)
    merged = _sigmoid(ga_ref[...]) * _mm_w(yret_ref[...], wret) + _sigmoid(gb_ref[...]) * _mm_w(yb, wrw)
    out = _mm_w(merged, wo)
    x1 = _layer_norm(DN_ALPHA * x + out, ln_g_ref[...], ln_b_ref[...])
    x1_ref[...] = x1
    xb_ref[...] = x1.astype(BF16)

    logits = _mm(x1, wr_ref[...], 'ik,kj->ij', EXACT_PASSES) + br_ref[...]
    lane = lax.broadcasted_iota(jnp.int32, logits.shape, 1)
    work = logits
    ids, vals = [], []
    for _ in range(TOP_K):
        m = jnp.max(work, -1, keepdims=True)
        idx = jnp.min(jnp.where(work == m, lane, N_EXPERTS), -1, keepdims=True)
        ids.append(idx)
        vals.append(m)
        work = jnp.where(lane == idx, -jnp.inf, work)
    exps = [jnp.exp(m - vals[0]) for m in vals]
    inv_den = 1.0 / sum(exps)
    slot = lax.broadcasted_iota(jnp.int32, ids_ref.shape, 1)
    ids_out = jnp.zeros(ids_ref.shape, jnp.int32)
    probs_out = jnp.zeros(probs_ref.shape, F32)
    for j in range(TOP_K):
        ids_out = jnp.where(slot == j, ids[j], ids_out)
        probs_out = jnp.where(slot == j, exps[j] * inv_den, probs_out)
    ids_ref[...] = ids_out
    probs_ref[...] = probs_out


def _post(tokens, yrw, r, k, v, g, yret, gates, seq_len, p, l, passes):
    tm = TOKEN_TILE
    n = (tokens[3] + 1) * tm
    row = lambda a: a.reshape(1, -1)
    tile = lambda w, j=0: pl.BlockSpec((tm, w), lambda i: (i, j))
    full = lambda a: pl.BlockSpec(a.shape, lambda i: (0,) * a.ndim)
    pair = _pair_tile_spec(seq_len // tm, RWKV_W)
    weights = (_split_weight(p['w_ret_out'][l], passes) + _split_weight(p['w_rwkv_out'][l], passes)
               + _split_weight(p['w_o'][l], passes))
    consts = [_head_blockdiag(), row(p['rw_r_k'][l]), row(p['rw_gn_g'][l]), row(p['rw_gn_b'][l]),
              row(p['ln1_g'][l]), row(p['ln1_b'][l]), p['w_router'][l], row(p['b_router'][l]), *weights]
    in_specs = (_token_specs(tokens, tm, D_MODEL, lambda i: i) + [pair] * 5
                + [tile(RET_V_W), tile(D_MODEL, 0), tile(D_MODEL, 1)] + [full(a) for a in consts])
    return pl.pallas_call(
        functools.partial(_post_kernel, len(weights) // 3, (None, None) + tokens[2:]),
        grid=(n // tm,),
        in_specs=in_specs,
        out_specs=[tile(D_MODEL), tile(D_MODEL), tile(TOP_K), tile(TOP_K)],
        out_shape=[jax.ShapeDtypeStruct((n, D_MODEL), F32), jax.ShapeDtypeStruct((n, D_MODEL), BF16),
                   jax.ShapeDtypeStruct((n, TOP_K), jnp.int32), jax.ShapeDtypeStruct((n, TOP_K), F32)],
        compiler_params=_params("parallel"),
        name="merge_ln_router",
    )(tokens[0], tokens[1], yrw, r, k, v, g, yret, gates, gates, *consts)


SEG_ALIGN = 8
ROUTE_TILE = 512
LOCAL_ROWS = 2304
PROB_LANES = 128
ROW_W = D_MODEL + PROB_LANES


def _start_segments(i, cnt_ref, lst_ref, gst_ref, make_copy):
    def per_expert(e, carry):
        j = i * N_EXPERTS + e
        rows = pl.multiple_of(cnt_ref[j], SEG_ALIGN)

        @pl.when(rows > 0)
        def _():
            make_copy(pl.multiple_of(lst_ref[j], SEG_ALIGN), pl.multiple_of(gst_ref[j], SEG_ALIGN), rows).start()

        return carry

    lax.fori_loop(0, N_EXPERTS, per_expert, 0)


def _wait_segments(i, tot_ref, make_copy):
    make_copy(0, 0, pl.multiple_of(tot_ref[i], SEG_ALIGN)).wait()


def _dispatch_kernel(cnt_ref, lst_ref, gst_ref, tot_ref, ids_ref, probs_ref, lstart_ref, xb_ref, xs_hbm, buf, sem):
    i = pl.program_id(0)
    tm = xb_ref.shape[0]
    ids = ids_ref[0]
    probs = probs_ref[0]
    expert = lax.broadcasted_iota(jnp.int32, (N_EXPERTS, tm), 0)
    picks = [ids[k:k + 1, :] == expert for k in range(TOP_K)]
    picked = sum(pk.astype(F32) for pk in picks)
    m = lax.broadcasted_iota(jnp.int32, (tm, tm), 0)
    n = lax.broadcasted_iota(jnp.int32, (tm, tm), 1)
    earlier = (m < n).astype(BF16)
    rank = jnp.dot(picked.astype(BF16), earlier, preferred_element_type=F32)
    pos = lstart_ref[0] + rank
    row = lax.broadcasted_iota(jnp.int32, (LOCAL_ROWS, tm), 0)
    perm = jnp.zeros((LOCAL_ROWS, tm), F32)
    weight = jnp.zeros((LOCAL_ROWS, tm), F32)
    for k in range(TOP_K):
        lpos = jnp.sum(jnp.where(picks[k], pos, 0.0), 0, keepdims=True).astype(jnp.int32)
        hit = row == lpos
        perm = jnp.where(hit, 1.0, perm)
        weight = jnp.where(hit, probs[k:k + 1, :], weight)
    rows = jnp.dot(perm.astype(BF16), xb_ref[...], preferred_element_type=F32)
    row_weight = jnp.broadcast_to(jnp.sum(weight, 1, keepdims=True), (LOCAL_ROWS, PROB_LANES))

    def copies_of(slot):
        return lambda lo, go, size: pltpu.make_async_copy(buf.at[slot, pl.ds(lo, size)], xs_hbm.at[pl.ds(go, size)],
                                                          sem.at[slot])

    slot = i % 2

    @pl.when(i >= 2)
    def _():
        _wait_segments(i - 2, tot_ref, copies_of(slot))

    buf[slot, :, :D_MODEL] = rows
    buf[slot, :, D_MODEL:] = row_weight
    _start_segments(i, cnt_ref, lst_ref, gst_ref, copies_of(slot))

    @pl.when(i == pl.num_programs(0) - 1)
    def _():
        @pl.when(i >= 1)
        def _():
            _wait_segments(i - 1, tot_ref, copies_of(1 - slot))

        _wait_segments(i, tot_ref, copies_of(slot))


def _round_up(x, m):
    return (x + m - 1) // m * m


def _routing_tables(ids):
    n = ids.shape[0]
    nt = n // ROUTE_TILE
    picked = jnp.sum(ids[:, :, None] == jnp.arange(N_EXPERTS, dtype=jnp.int32)[None, None, :], 1)
    cnt = jnp.sum(picked.reshape(nt, ROUTE_TILE, N_EXPERTS), 1).astype(jnp.int32)
    cnt = _round_up(cnt, SEG_ALIGN)
    lstart = jnp.cumsum(cnt, 1) - cnt
    per_expert = jnp.sum(cnt, 0)
    region = _round_up(per_expert, EXPERT_ROW_TILE)
    gstart = (jnp.cumsum(region) - region)[None, :] + jnp.cumsum(cnt, 0) - cnt
    return cnt, lstart.astype(jnp.int32), gstart.astype(jnp.int32), jnp.sum(cnt, 1), region


def _table_args(tables):
    cnt, lstart, gstart, tile_rows, _ = tables
    return cnt.reshape(-1), lstart.reshape(-1), gstart.reshape(-1), tile_rows


def _dispatch(xb, ids, probs, tables, n_rows):
    n = xb.shape[0]
    tm = ROUTE_TILE
    nt = n // tm
    lstart = tables[1]
    to_lanes = lambda a: a.reshape(nt, tm, TOP_K).transpose(0, 2, 1)
    grid_spec = pltpu.PrefetchScalarGridSpec(
        num_scalar_prefetch=4,
        grid=(nt,),
        in_specs=[pl.BlockSpec((1, TOP_K, tm), lambda i, *_: (i, 0, 0)), pl.BlockSpec((1, TOP_K, tm), lambda i, *_: (i, 0, 0)),
                  pl.BlockSpec((1, N_EXPERTS, 1), lambda i, *_: (i, 0, 0)), pl.BlockSpec((tm, D_MODEL), lambda i, *_: (i, 0))],
        out_specs=pl.BlockSpec(memory_space=pl.ANY),
        scratch_shapes=[pltpu.VMEM((2, LOCAL_ROWS, ROW_W), F32), pltpu.SemaphoreType.DMA((2,))],
    )
    return pl.pallas_call(
        _dispatch_kernel,
        grid_spec=grid_spec,
        out_shape=jax.ShapeDtypeStruct((n_rows, ROW_W), F32),
        compiler_params=_params("arbitrary"),
        name="dispatch",
    )(*_table_args(tables), to_lanes(ids), to_lanes(probs), lstart.astype(F32)[:, :, None], xb)


def _expert_kernel(tile_ref, exp_ref, valid_ref, x_ref, wgu_ref, bgu_ref, wd_ref, bd_ref, o_ref, wgu_s, wd_s):
    i = pl.program_id(0)

    @pl.when(valid_ref[i] == 1)
    def _():
        @pl.when((i == 0) | (exp_ref[jnp.maximum(i - 1, 0)] != exp_ref[i]))
        def _():
            wgu_s[...] = wgu_ref[0, 0].astype(BF16)
            wd_s[...] = wd_ref[0, 0].astype(BF16)

        h = jnp.dot(x_ref[:, :D_MODEL].astype(BF16), wgu_s[...], preferred_element_type=F32) + bgu_ref[0, 0]
        gate = jnp.minimum(h[:, :D_FF], SWIGLU_LIMIT)
        up = jnp.clip(h[:, D_FF:], -SWIGLU_LIMIT, SWIGLU_LIMIT)
        act = gate * _sigmoid(SWIGLU_ALPHA * gate) * (up + 1.0)
        y = jnp.dot(act.astype(BF16), wd_s[...], preferred_element_type=F32) + bd_ref[0, 0]
        o_ref[...] = y * x_ref[:, D_MODEL:D_MODEL + 1]


def _expert_schedule(region, n_rows, tm):
    n_entries = n_rows // tm
    tile_end = jnp.cumsum(region // tm)
    total = tile_end[-1]
    t = jnp.minimum(jnp.arange(n_entries), total - 1).astype(jnp.int32)
    e = jnp.sum(tile_end[None, :] <= t[:, None], -1).astype(jnp.int32)
    valid = (jnp.arange(n_entries) < total).astype(jnp.int32)
    return t, e, valid


def _experts(xs, sched, p, l):
    n_rows = xs.shape[0]
    tm = EXPERT_ROW_TILE
    bgu = p['b_gate_up'].reshape(DEPTH, N_EXPERTS, 1, 2 * D_FF)
    bdn = p['b_down'].reshape(DEPTH, N_EXPERTS, 1, D_MODEL)
    by_tile = lambda w: pl.BlockSpec((tm, w), lambda i, t, e, *_: (t[i], 0))
    by_expert = lambda a, b: pl.BlockSpec((1, 1, a, b), lambda i, t, e, *_: (l, e[i], 0, 0))
    grid_spec = pltpu.PrefetchScalarGridSpec(
        num_scalar_prefetch=3,
        grid=(sched[0].shape[0],),
        in_specs=[by_tile(ROW_W), by_expert(D_MODEL, 2 * D_FF), by_expert(1, 2 * D_FF),
                  by_expert(D_FF, D_MODEL), by_expert(1, D_MODEL)],
        out_specs=by_tile(D_MODEL),
        scratch_shapes=[pltpu.VMEM((D_MODEL, 2 * D_FF), BF16), pltpu.VMEM((D_FF, D_MODEL), BF16)],
    )
    return pl.pallas_call(
        _expert_kernel,
        grid_spec=grid_spec,
        out_shape=jax.ShapeDtypeStruct((n_rows, D_MODEL), F32),
        compiler_params=_params("arbitrary"),
        name="experts",
    )(*sched, xs, p['w_gate_up'], bgu, p['w_down'], bdn)


def _combine_kernel(tile0, cnt_ref, lst_ref, gst_ref, tot_ref, ids_ref, lstart_ref, x_ref, ln_g_ref, ln_b_ref, ys_hbm,
                    o_ref, buf, sem):
    step = pl.program_id(0)
    i = step + tile0
    tm = x_ref.shape[0]
    slot = step % 2

    def copies_of(slot):
        return lambda lo, go, size: pltpu.make_async_copy(ys_hbm.at[pl.ds(go, size)], buf.at[slot, pl.ds(lo, size)],
                                                          sem.at[slot])

    def fetch(tile, slot):
        buf[slot, ROUTE_TILE * TOP_K:, :] = jnp.zeros((LOCAL_ROWS - ROUTE_TILE * TOP_K, D_MODEL), F32)
        _start_segments(tile, cnt_ref, lst_ref, gst_ref, copies_of(slot))

    @pl.when(step == 0)
    def _():
        fetch(i, slot)

    @pl.when(step + 1 < pl.num_programs(0))
    def _():
        fetch(i + 1, 1 - slot)

    ids = ids_ref[...]
    expert = lax.broadcasted_iota(jnp.int32, (tm, N_EXPERTS), 1)
    picks = [ids[:, k:k + 1] == expert for k in range(TOP_K)]
    picked = sum(pk.astype(F32) for pk in picks)
    m = lax.broadcasted_iota(jnp.int32, (tm, tm), 0)
    n = lax.broadcasted_iota(jnp.int32, (tm, tm), 1)
    earlier = (n < m).astype(BF16)
    rank = jnp.dot(earlier, picked.astype(BF16), preferred_element_type=F32)
    pos = lstart_ref[0] + rank
    col = lax.broadcasted_iota(jnp.int32, (tm, LOCAL_ROWS), 1)
    perm = jnp.zeros((tm, LOCAL_ROWS), F32)
    for k in range(TOP_K):
        lpos = jnp.sum(jnp.where(picks[k], pos, 0.0), 1, keepdims=True).astype(jnp.int32)
        perm = jnp.where(col == lpos, 1.0, perm)

    _wait_segments(i, tot_ref, copies_of(slot))
    moe = jnp.dot(perm.astype(BF16), buf[slot].astype(BF16), preferred_element_type=F32)
    o_ref[...] = _layer_norm(DN_ALPHA * x_ref[...] + moe, ln_g_ref[...], ln_b_ref[...])


def _combine(x1, ys, ids, tables, ln_g, ln_b, tile0, n_tiles):
    tm = ROUTE_TILE
    lstart = tables[1]
    at = lambda i, *_: (i + tile0, 0)
    grid_spec = pltpu.PrefetchScalarGridSpec(
        num_scalar_prefetch=4,
        grid=(n_tiles,),
        in_specs=[pl.BlockSpec((tm, TOP_K), at), pl.BlockSpec((1, 1, N_EXPERTS), lambda i, *_: (i + tile0, 0, 0)),
                  pl.BlockSpec((tm, D_MODEL), at), pl.BlockSpec((1, D_MODEL), lambda i, *_: (0, 0)),
                  pl.BlockSpec((1, D_MODEL), lambda i, *_: (0, 0)), pl.BlockSpec(memory_space=pl.ANY)],
        out_specs=pl.BlockSpec((tm, D_MODEL), lambda i, *_: (i, 0)),
        scratch_shapes=[pltpu.VMEM((2, LOCAL_ROWS, D_MODEL), F32), pltpu.SemaphoreType.DMA((2,))],
    )
    return pl.pallas_call(
        functools.partial(_combine_kernel, tile0),
        grid_spec=grid_spec,
        out_shape=jax.ShapeDtypeStruct((n_tiles * tm, D_MODEL), F32),
        compiler_params=_params("arbitrary"),
        name="combine_ln",
    )(*_table_args(tables), ids, lstart.astype(F32)[:, None, :], x1, ln_g.reshape(1, -1), ln_b.reshape(1, -1), ys)


def _moe(x1, xb, ids, probs, p, l, parts):
    n = x1.shape[0]
    tm = EXPERT_ROW_TILE
    seg_rows = n * TOP_K + (n // ROUTE_TILE) * N_EXPERTS * (SEG_ALIGN - 1)
    n_rows = _round_up(seg_rows + N_EXPERTS * (tm - SEG_ALIGN), tm)
    tables = _routing_tables(ids)
    xs = _dispatch(xb, ids, probs, tables, n_rows)
    ys = _experts(xs, _expert_schedule(tables[4], n_rows, tm), p, l)
    return [_combine(x1, ys, ids, tables, p['ln2_g'][l], p['ln2_b'][l], *part) for part in parts]


def _pad_time(a, tp):
    return jnp.pad(a, ((0, 0), (0, tp - a.shape[1]), (0, 0)))


def kernel(x_prompt, x_sample, state_ret, state_rwkv, state_shift, w_in, ret_gn_g, ret_gn_b, w_ret_out, rw_mu, rw_w0, rw_w_up, rw_a0, rw_a_up, rw_g_up, rw_k_k, rw_k_a, rw_r_k, rw_gn_g, rw_gn_b, rw_v0, rw_vres_down, rw_vres_up, w_rwkv_out, w_o, ln1_g, ln1_b, w_router, b_router, w_gate_up, b_gate_up, w_down, b_down, ln2_g, ln2_b):
    p = dict(w_ret_out=w_ret_out, rw_mu=rw_mu, rw_w0=rw_w0, rw_w_up=rw_w_up, rw_a0=rw_a0, rw_a_up=rw_a_up,
             rw_g_up=rw_g_up, rw_k_k=rw_k_k, rw_k_a=rw_k_a, rw_r_k=rw_r_k, rw_gn_g=rw_gn_g, rw_gn_b=rw_gn_b,
             w_rwkv_out=w_rwkv_out, w_o=w_o, ln1_g=ln1_g, ln1_b=ln1_b, w_router=w_router, b_router=b_router,
             w_gate_up=w_gate_up, b_gate_up=b_gate_up, w_down=w_down, b_down=b_down, ln2_g=ln2_g, ln2_b=ln2_b)
    bp, tp, _ = x_prompt.shape
    bs, ts, _ = x_sample.shape
    np_, ns = bp * tp, bs * ts
    pos_p = jnp.arange(tp, dtype=F32)
    ts_ret = 8
    pos_s = PAST_LEN + jnp.arange(ts_ret, dtype=F32)

    assert ns == TOKEN_TILE and np_ % TOKEN_TILE == 0
    n_prompt_tiles = np_ // TOKEN_TILE
    tokens = (x_prompt.reshape(np_, D_MODEL), x_sample.transpose(1, 0, 2).reshape(ns, D_MODEL), 0, n_prompt_tiles)
    outs = {k: [] for k in ('ret_p', 'rw_p', 'sh_p', 'ret_s', 'rw_s', 'sh_s')}
    v_first = None
    u_off = RET_W
    g_off = RET_W + SHIFT_W
    for l in range(DEPTH):
        passes = EXACT_PASSES if l == 0 else 1
        z_ret = _matmul(tokens, w_in[l][:, :u_off], TOKEN_TILE, RET_W if passes == 1 else RET_W // 2, passes)
        u = _matmul(tokens, w_in[l][:, u_off:g_off], TOKEN_TILE, SHIFT_W, passes)
        gates = _matmul(tokens, w_in[l][:, g_off:], TOKEN_TILE, D_MODEL, passes)

        zr_s = _pad_time(z_ret[np_:].reshape(ts, bs, RET_W).transpose(1, 0, 2), ts_ret)
        yret, sret_p = _retention(z_ret.reshape(-1, RET_CHUNK, RET_W), pos_p, tp, None, ret_gn_g[l], ret_gn_b[l], 1,
                                  passes, n_seq=bp)
        yret_s, sret_s = _retention(zr_s, pos_s, ts, state_ret[l], ret_gn_g[l], ret_gn_b[l], 8, passes)
        yret = yret.reshape(-1, RET_V_W).at[np_:].set(yret_s[:, :ts].transpose(1, 0, 2).reshape(ns, RET_V_W))

        vres = None if l == 0 else (rw_v0[l - 1], rw_vres_down[l - 1], rw_vres_up[l - 1])
        r, lw, k, v, a, b, g = _rwkv_pre(u, state_shift[l], np_, tp, p, l, v_first, vres)
        if l == 0:
            v_first = v
        yrw, srw_p = _rwkv_chunks(r, lw, k, v, a, b, passes)
        yrw, srw_s = _rwkv_steps(r, lw, k, v, a, b, state_rwkv[l], yrw)

        x1, xb, ids, probs = _post(tokens, yrw, r, k, v, g, yret, gates, tp, p, l, passes)
        if l + 1 < DEPTH:
            x, = _moe(x1, xb, ids, probs, p, l, [(0, n_prompt_tiles + 1)])
            tokens = (x, x, n_prompt_tiles, n_prompt_tiles)
        else:
            y_p, y_s = _moe(x1, xb, ids, probs, p, l, [(0, n_prompt_tiles), (n_prompt_tiles, 1)])

        outs['ret_p'].append(sret_p)
        outs['ret_s'].append(sret_s)
        outs['rw_p'].append(srw_p)
        outs['rw_s'].append(srw_s)
        outs['sh_p'].append(u[tp - 1:np_:tp])
        outs['sh_s'].append(u[np_ + ns - bs:])

    y_prompt = y_p.reshape(bp, tp, D_MODEL)
    y_sample = y_s.reshape(ts, bs, D_MODEL).transpose(1, 0, 2)
    st = {k: jnp.stack(v) for k, v in outs.items()}
    return (y_prompt, y_sample, st['ret_p'], st['rw_p'], st['sh_p'], st['ret_s'], st['rw_s'], st['sh_s'])
```

```python
import functools

import jax
import jax.numpy as jnp
from jax import lax
from jax.experimental import pallas as pl
from jax.experimental.pallas import tpu as pltpu

F32 = jnp.float32
BF16 = jnp.bfloat16
HI = lax.Precision.HIGHEST

D_MODEL = 1024
DEPTH = 2
PAST_LEN = 16384
RET_HEADS = 4
RET_DK = 128
RET_DV = 256
RET_QK_W = RET_HEADS * RET_DK
RET_V_W = RET_HEADS * RET_DV
RET_W = 2 * RET_QK_W + 2 * RET_V_W
RET_CHUNK = 128
ROPE_BASE = 10000.0
RWKV_HEADS = 8
RWKV_N = 64
RWKV_W = RWKV_HEADS * RWKV_N
LORA_W = 64
LORA_A = 64
LORA_G = 128
SHIFT_W = 3 * RWKV_W + LORA_W + LORA_A + LORA_G
RWKV_CHUNK = 64
N_EXPERTS = 32
TOP_K = 4
D_FF = D_MODEL
SWIGLU_LIMIT = 7.0
SWIGLU_ALPHA = 1.702
DN_ALPHA = (2 * DEPTH) ** 0.25
LN_EPS = 1e-5
RET_GN_EPS = 1e-5
RWKV_GN_EPS = 64e-5

VMEM_LIMIT = 56 * 1024 * 1024
TOKEN_TILE = 512
EXPERT_ROW_TILE = 512
RWKV_STEP_BATCH_BLOCK = 8
EXACT_PASSES = 3
RWKV_REFINE_STEPS = 1


def _params(*sem):
    return pltpu.CompilerParams(dimension_semantics=sem, vmem_limit_bytes=VMEM_LIMIT)


def _split(x):
    hi = x.astype(BF16)
    lo = (x - hi.astype(F32)).astype(BF16)
    return hi, lo


def _split_kernel(w_ref, hi_ref, lo_ref):
    hi_ref[...], lo_ref[...] = _split(w_ref[...])


def _split_weight(w, passes):
    if passes == 1:
        return (w.astype(BF16),)
    rows = 256
    spec = pl.BlockSpec((rows, w.shape[1]), lambda i: (i, 0))
    return tuple(pl.pallas_call(
        _split_kernel,
        grid=(w.shape[0] // rows,),
        in_specs=[spec],
        out_specs=[spec, spec],
        out_shape=[jax.ShapeDtypeStruct(w.shape, BF16)] * 2,
        compiler_params=_params("parallel"),
        name="split_weight",
    )(w))


def _mm(a, b, spec, passes):
    dg = lambda x, y: jnp.einsum(spec, x, y, preferred_element_type=F32)
    if passes == 1:
        return dg(a.astype(BF16), b.astype(BF16))
    ah, al = _split(a)
    bh, bl = _split(b)
    return dg(ah, bh) + (dg(ah, bl) + dg(al, bh))


def _mm_w(a, w_refs):
    dg = lambda x, y: jnp.dot(x, y, preferred_element_type=F32)
    if len(w_refs) == 1:
        return dg(a.astype(BF16), w_refs[0][...])
    ah, al = _split(a)
    return dg(ah, w_refs[0][...]) + (dg(ah, w_refs[1][...]) + dg(al, w_refs[0][...]))


def _dot_hi(a, b):
    return jnp.dot(a, b, precision=HI, preferred_element_type=F32)


def _sigmoid(x):
    return 1.0 / (1.0 + jnp.exp(-x))


def _layer_norm(x, g, b):
    mu = jnp.mean(x, -1, keepdims=True)
    d = x - mu
    var = jnp.mean(d * d, -1, keepdims=True)
    return d * lax.rsqrt(var + LN_EPS) * g + b


def _token_specs(tokens, tm, w, tile_of):
    _, _, b_block, n_a = tokens
    return [pl.BlockSpec((tm, w), lambda *g: (jnp.minimum(tile_of(*g), n_a - 1), 0)),
            pl.BlockSpec((tm, w), lambda *g: (b_block, 0))]


def _token_tile(tokens, i, a_ref, b_ref):
    return jnp.where(i < tokens[3], a_ref[...], b_ref[...])


def _matmul_kernel(tokens, xa_ref, xb_ref, *refs):
    o_ref = refs[-1]
    o_ref[...] = _mm_w(_token_tile(tokens, pl.program_id(1), xa_ref, xb_ref), refs[:-1])


def _matmul(tokens, w, tm, tn, passes):
    k, n = w.shape
    n_tiles = tokens[3] + 1
    ws = _split_weight(w, passes)
    return pl.pallas_call(
        functools.partial(_matmul_kernel, (None, None) + tokens[2:]),
        grid=(n // tn, n_tiles),
        in_specs=_token_specs(tokens, tm, k, lambda j, i: i) + [pl.BlockSpec((k, tn), lambda j, i: (0, j))] * len(ws),
        out_specs=pl.BlockSpec((tm, tn), lambda j, i: (i, j)),
        out_shape=jax.ShapeDtypeStruct((n_tiles * tm, n), F32),
        compiler_params=_params("parallel", "parallel"),
        name="in_proj",
    )(tokens[0], tokens[1], *ws)


def _ret_kernel(has_state, passes, q_ref, k_ref, v_ref, g_ref, cos_ref, sin_ref, dm_ref, qd_ref, kd_ref, cd_ref,
                gng_ref, gnb_ref, *rest):
    if has_state:
        s0_ref, y_ref, so_ref, s_scr = rest[0], *rest[-3:]
    else:
        y_ref, so_ref, s_scr = rest
    c = pl.program_id(1)
    bb, cl, _ = q_ref.shape
    nh = RET_HEADS

    def heads(x, w):
        return jnp.concatenate([x[:, :, h * w:(h + 1) * w] for h in range(nh)], 0)

    def per_head(ref):
        return jnp.concatenate([jnp.broadcast_to(ref[h], (bb,) + ref.shape[1:]) for h in range(nh)], 0)

    @pl.when(c == 0)
    def _():
        if has_state:
            s_scr[...] = jnp.concatenate([s0_ref[:, h] for h in range(nh)], 0)
        else:
            s_scr[...] = jnp.zeros_like(s_scr)

    cos = cos_ref[...]
    sin = sin_ref[...]
    mm = functools.partial(_mm, passes=passes)

    def rope(x):
        x2 = x.reshape(nh * bb * cl, RET_DK)
        rot = pltpu.roll(x2, RET_DK // 2, axis=1).reshape(nh * bb, cl, RET_DK)
        return x * cos + rot * sin

    q = rope(heads(q_ref[...], RET_DK))
    k = rope(heads(k_ref[...], RET_DK)) * (RET_DK ** -0.5)
    v = heads(v_ref[...], RET_DV)
    s = s_scr[...]
    sc = mm(q, k, 'bid,bjd->bij') * per_head(dm_ref)
    intra = mm(sc, v, 'bij,bje->bie')
    cross = mm(q, s, 'bid,bde->bie') * per_head(qd_ref)
    s_new = s * per_head(cd_ref) + mm(k * per_head(kd_ref), v, 'bjd,bje->bde')
    s_scr[...] = s_new

    y = intra + cross
    mu = jnp.mean(y, -1, keepdims=True)
    d = y - mu
    var = jnp.mean(d * d, -1, keepdims=True)
    yn = d * lax.rsqrt(var + RET_GN_EPS)
    for h in range(nh):
        cols = slice(h * RET_DV, (h + 1) * RET_DV)
        rg = g_ref[:, :, cols]
        y_ref[:, :, cols] = (yn[h * bb:(h + 1) * bb] * gng_ref[:, cols] + gnb_ref[:, cols]) * (rg * _sigmoid(rg))

    @pl.when(c == pl.num_programs(1) - 1)
    def _():
        for h in range(nh):
            so_ref[:, h] = s_new[h * bb:(h + 1) * bb]


def _retention(z, pos, t_real, s0, gn_g, gn_b, bb, passes, n_seq=None):
    flat = n_seq is not None
    tp = pos.shape[0]
    b = n_seq if flat else z.shape[0]
    cl = RET_CHUNK if t_real % RET_CHUNK == 0 else tp
    cr = min(cl, t_real)
    nc = tp // cl
    at = (lambda bi, c: (bi * nc + c, 0)) if flat else (lambda bi, c: (bi, c))
    half = RET_DK // 2
    inv = ROPE_BASE ** (-jnp.arange(half, dtype=F32) / half)
    ang = pos[:, None] * inv[None, :]
    cos = jnp.concatenate([jnp.cos(ang), jnp.cos(ang)], -1)
    sin = jnp.concatenate([-jnp.sin(ang), jnp.sin(ang)], -1)
    lg = jnp.log1p(-jnp.exp2(-5.0 - jnp.arange(RET_HEADS, dtype=F32)))
    i = jnp.arange(cl, dtype=F32)
    real = i < cr
    diff = i[:, None] - i[None, :]
    ok = (diff >= 0) & real[:, None] & real[None, :]
    dmask = jnp.exp(jnp.where(ok[None], diff[None] * lg[:, None, None], -jnp.inf))
    q_dec = jnp.exp((i[None, :] + 1.0) * lg[:, None])[..., None]
    k_dec = jnp.where(real[None, :], jnp.exp((cr - 1.0 - i)[None, :] * lg[:, None]), 0.0)[..., None]
    c_dec = jnp.exp(cr * lg)[:, None, None]

    has_state = s0 is not None
    cols = lambda w, j: pl.BlockSpec((bb, cl, w), lambda bi, c: at(bi, c) + (j,))
    full = lambda a: pl.BlockSpec(a.shape, lambda bi, c: (0,) * a.ndim)
    consts = [dmask, q_dec, k_dec, c_dec, gn_g.reshape(1, -1), gn_b.reshape(1, -1)]
    in_specs = [cols(RET_QK_W, 0), cols(RET_QK_W, 1), cols(RET_V_W, 1), cols(RET_V_W, 2),
                pl.BlockSpec((cl, RET_DK), lambda bi, c: (c, 0)), pl.BlockSpec((cl, RET_DK), lambda bi, c: (c, 0))]
    in_specs += [full(a) for a in consts]
    args = [z, z, z, z, cos, sin] + consts
    aliases = {}
    if has_state:
        all_states, layer, earlier = s0
        s_spec = pl.BlockSpec((None, bb, RET_HEADS, RET_DK, RET_DV), lambda bi, c: (layer, bi, 0, 0, 0))
        s_shape = all_states.shape
        in_specs.append(s_spec)
        args.append(all_states)
        if earlier is not None:
            aliases = {len(args): 1}
            in_specs.append(pl.BlockSpec(memory_space=pl.ANY))
            args.append(earlier)
    else:
        s_spec = pl.BlockSpec((bb, RET_HEADS, RET_DK, RET_DV), lambda bi, c: (bi, 0, 0, 0))
        s_shape = (b, RET_HEADS, RET_DK, RET_DV)
    return pl.pallas_call(
        functools.partial(_ret_kernel, has_state, passes),
        grid=(b // bb, nc),
        in_specs=in_specs,
        out_specs=[cols(RET_V_W, 0), s_spec],
        out_shape=[jax.ShapeDtypeStruct(z.shape[:2] + (RET_V_W,), F32), jax.ShapeDtypeStruct(s_shape, F32)],
        scratch_shapes=[pltpu.VMEM((RET_HEADS * bb, RET_DK, RET_DV), F32)],
        input_output_aliases=aliases,
        compiler_params=_params("parallel", "arbitrary"),
        name="retention",
    )(*args)


def _head_sum(x, bd):
    hi, lo = _split(x)
    return jnp.dot(hi, bd, preferred_element_type=F32) + jnp.dot(lo, bd, preferred_element_type=F32)


def _pair_shape(n_pairs, chunks, w):
    return (n_pairs + 1, chunks, 2, RWKV_CHUNK, w)


def _pair_tile_spec(tiles_per_seq, w):
    rows = TOKEN_TILE // RWKV_CHUNK
    return pl.BlockSpec((1, rows, 1, RWKV_CHUNK, w),
                        lambda i: (i // (2 * tiles_per_seq), i % tiles_per_seq, (i // tiles_per_seq) % 2, 0, 0))


def _to_pair_tile(ref, x):
    ref[0, :, 0] = x.reshape(TOKEN_TILE // RWKV_CHUNK, RWKV_CHUNK, x.shape[-1])


def _from_pair_tile(ref):
    return ref[0, :, 0].reshape(TOKEN_TILE, ref.shape[-1])


def _rwkv_pre_kernel(has_vres, n_prompt_tiles, tiles_per_seq, u_ref, tail_ref, shift_ref, mu_ref, w0_ref, wup_ref,
                     a0_ref, aup_ref, gup_ref, kk_ref, ka_ref, bd_ref, *rest):
    if has_vres:
        vf_ref, v0_ref, vd_ref, vu_ref, r_o, lw_o, k_o, v_o, a_o, b_o, g_o = rest
    else:
        r_o, lw_o, k_o, v_o, a_o, b_o, g_o = rest
    i = pl.program_id(0)
    u = u_ref[...]
    tm = u.shape[0]
    row = lax.broadcasted_iota(jnp.int32, (tm, 1), 0)
    before = jnp.where(i % tiles_per_seq == 0, 0.0, tail_ref[tail_ref.shape[0] - 1:, :])
    prev_prompt = jnp.where(row == 0, before, pltpu.roll(u, 1, axis=0))
    n_seq = shift_ref.shape[0]
    prev_sample = jnp.concatenate([shift_ref[...], u[:tm - n_seq]], 0)
    prev = jnp.where(i < n_prompt_tiles, prev_prompt, prev_sample)
    um = u + (prev - u) * mu_ref[...]
    w1, w2, w3 = RWKV_W, 2 * RWKV_W, 3 * RWKV_W
    r = um[:, :w1]
    kw = um[:, w1:w2]
    vw = um[:, w2:w3]
    wd = um[:, w3:w3 + LORA_W]
    ad = um[:, w3 + LORA_W:w3 + LORA_W + LORA_A]
    gd = um[:, w3 + LORA_W + LORA_A:]
    lora = lambda x, w_ref: _mm(x, w_ref[...], 'ik,kj->ij', EXACT_PASSES)
    xw = w0_ref[...] + lora(jnp.tanh(wd), wup_ref)
    softplus = jnp.maximum(-xw, 0.0) + jnp.log1p(jnp.exp(-jnp.abs(xw)))
    _to_pair_tile(lw_o, -jnp.exp(-softplus - 0.5))
    a = _sigmoid(a0_ref[...] + lora(ad, aup_ref))
    _to_pair_tile(g_o, lora(_sigmoid(gd), gup_ref))
    if has_vres:
        gate = _sigmoid(v0_ref[...] + lora(lora(vw, vd_ref), vu_ref))
        vw = vw + (_from_pair_tile(vf_ref) - vw) * gate
    kk = kw * kk_ref[...]
    norm = jnp.sqrt(_head_sum(kk * kk, bd_ref[...]))
    kk = kk / jnp.maximum(norm, 1e-12)
    _to_pair_tile(r_o, r)
    _to_pair_tile(k_o, kw * (1.0 + (a - 1.0) * ka_ref[...]))
    _to_pair_tile(v_o, vw)
    _to_pair_tile(a_o, -kk)
    _to_pair_tile(b_o, kk * a)


def _head_blockdiag():
    h = jnp.arange(RWKV_W) // RWKV_N
    return (h[:, None] == h[None, :]).astype(BF16)


def _rwkv_pre(u, shift_state, n_prompt, seq_len, p, l, v_first, vres):
    n = u.shape[0]
    tm = TOKEN_TILE
    assert n == n_prompt + tm and seq_len % tm == 0
    tiles_per_seq = seq_len // tm
    tail_rows = 8
    row = lambda a: a.reshape(1, -1)
    full = lambda a: pl.BlockSpec(a.shape, lambda i: (0,) * a.ndim)
    pair = _pair_tile_spec(tiles_per_seq, RWKV_W)
    has_vres = vres is not None
    args = [u, u, shift_state, row(p['rw_mu'][l]), row(p['rw_w0'][l]), p['rw_w_up'][l], row(p['rw_a0'][l]),
            p['rw_a_up'][l], p['rw_g_up'][l], row(p['rw_k_k'][l]), row(p['rw_k_a'][l]), _head_blockdiag()]
    in_specs = [pl.BlockSpec((tm, SHIFT_W), lambda i: (i, 0)),
                pl.BlockSpec((tail_rows, SHIFT_W), lambda i: (jnp.maximum(i * (tm // tail_rows) - 1, 0), 0))]
    in_specs += [full(a) for a in args[2:]]
    if has_vres:
        extra = [v_first, row(vres[0]), vres[1], vres[2]]
        in_specs += [pair] + [full(a) for a in extra[1:]]
        args += extra
    shape = _pair_shape(n_prompt // seq_len // 2, seq_len // RWKV_CHUNK, RWKV_W)
    return pl.pallas_call(
        functools.partial(_rwkv_pre_kernel, has_vres, n_prompt // tm, tiles_per_seq),
        grid=(n // tm,),
        in_specs=in_specs,
        out_specs=[pair] * 7,
        out_shape=[jax.ShapeDtypeStruct(shape, F32)] * 7,
        compiler_params=_params("parallel"),
        name="rwkv_pre",
    )(*args)


def _rwkv_chunk_kernel(passes, r_ref, lw_ref, k_ref, v_ref, a_ref, b_ref, tri_ref, y_ref, so_ref, s_scr):
    c = pl.program_id(1)

    @pl.when(c == 0)
    def _():
        s_scr[...] = jnp.zeros_like(s_scr)

    bb, cl = r_ref.shape[2], r_ref.shape[3]
    r_ref, lw_ref, k_ref, v_ref, a_ref, b_ref, y_ref = (ref.at[0, 0] for ref in
                                                        (r_ref, lw_ref, k_ref, v_ref, a_ref, b_ref, y_ref))
    mm = functools.partial(_mm, passes=passes)
    ti = lax.broadcasted_iota(jnp.int32, (cl, cl), 0)
    si = lax.broadcasted_iota(jnp.int32, (cl, cl), 1)
    strict = (ti > si).astype(F32)
    incl = (ti >= si).astype(F32)
    eye = (ti == si).astype(F32)

    def heads(x):
        return jnp.stack([x[bi][:, h * RWKV_N:(h + 1) * RWKV_N] for bi in range(bb) for h in range(RWKV_HEADS)])

    lw = lw_ref[...]
    cum = jnp.stack([_dot_hi(tri_ref[...], lw[bi]) for bi in range(bb)])
    last = cum[:, cl - 1:cl, :]
    e_neg = jnp.exp(-cum)
    e_end = jnp.exp(last - cum)
    at = heads(a_ref[...] * jnp.exp(cum - lw))
    rt = heads(r_ref[...] * jnp.exp(cum))
    bt = heads(b_ref[...] * e_neg)
    kt = heads(k_ref[...] * e_neg)
    bw = heads(b_ref[...] * e_end)
    kw = heads(k_ref[...] * e_end)
    wc = heads(jnp.exp(last))
    vh = heads(v_ref[...])

    lhs = jnp.concatenate([at, rt], 1)
    gram = mm(lhs, jnp.concatenate([bt, kt], 1), 'gik,gjk->gij')
    a_ab = gram[:, :cl, :cl] * strict
    a_ak = gram[:, :cl, cl:] * strict
    a_rb = gram[:, cl:, :cl] * incl
    a_rk = gram[:, cl:, cl:] * incl
    refine = RWKV_REFINE_STEPS if passes > 1 else 0
    mm_inv = functools.partial(_mm, passes=1) if refine else mm
    inv = eye + a_ab
    pw = a_ab
    for _ in range(cl.bit_length() - 2):
        pw = mm_inv(pw, pw, 'gij,gjk->gik')
        inv = inv + mm_inv(inv, pw, 'gij,gjk->gik')
    s0 = s_scr[...].reshape(bb * RWKV_HEADS, RWKV_N, RWKV_N)
    xs = mm(lhs, s0, 'gtj,gij->gti')
    av = mm(jnp.concatenate([a_ak, a_rk], 1), vh, 'gts,gsi->gti')
    rhs = xs[:, :cl] + av[:, :cl]
    u = mm_inv(inv, rhs, 'gts,gsi->gti')
    for _ in range(refine):
        u = u + mm_inv(inv, rhs - u + mm(a_ab, u, 'gts,gsi->gti'), 'gts,gsi->gti')
    y = xs[:, cl:] + av[:, cl:] + mm(a_rb, u, 'gts,gsi->gti')
    s_new = s0 * wc + mm(jnp.concatenate([u, vh], 1), jnp.concatenate([bw, kw], 1), 'gti,gtj->gij')
    s_scr[...] = s_new.reshape(bb, RWKV_HEADS, RWKV_N, RWKV_N)
    for bi in range(bb):
        for h in range(RWKV_HEADS):
            y_ref[bi, :, h * RWKV_N:(h + 1) * RWKV_N] = y[bi * RWKV_HEADS + h]

    @pl.when(c == pl.num_programs(1) - 1)
    def _():
        so_ref[...] = s_scr[...]


def _rwkv_chunks(r, lw, k, v, a, b, passes):
    n_pairs, chunks, bb, cl, _ = r.shape
    n_pairs -= 1
    seq = pl.BlockSpec((1, 1, bb, cl, RWKV_W), lambda pi, c: (pi, c, 0, 0, 0))
    s_spec = pl.BlockSpec((bb, RWKV_HEADS, RWKV_N, RWKV_N), lambda pi, c: (pi, 0, 0, 0))
    tri = (jnp.arange(cl)[:, None] >= jnp.arange(cl)[None, :]).astype(F32)
    return pl.pallas_call(
        functools.partial(_rwkv_chunk_kernel, passes),
        grid=(n_pairs, chunks),
        in_specs=[seq] * 6 + [pl.BlockSpec((cl, cl), lambda pi, c: (0, 0))],
        out_specs=[seq, s_spec],
        out_shape=[jax.ShapeDtypeStruct(r.shape, F32),
                   jax.ShapeDtypeStruct((n_pairs * bb, RWKV_HEADS, RWKV_N, RWKV_N), F32)],
        scratch_shapes=[pltpu.VMEM((bb, RWKV_HEADS, RWKV_N, RWKV_N), F32)],
        compiler_params=_params("parallel", "arbitrary"),
        name="rwkv_chunks",
    )(r, lw, k, v, a, b, tri)


def _rwkv_step_kernel(r_ref, lw_ref, k_ref, a_ref, b_ref, vt_ref, s0_ref, *rest):
    yt_ref, so_ref = rest[-2:]
    s = s0_ref[...]
    for t in range(r_ref.shape[2]):
        row = lambda ref: ref[:, :, t:t + 1, :]
        sa = jnp.sum(s * row(a_ref), -1, keepdims=True)
        s = s * jnp.exp(row(lw_ref)) + sa * row(b_ref) + vt_ref[:, :, :, t:t + 1] * row(k_ref)
        yt_ref[:, :, :, t:t + 1] = jnp.sum(s * row(r_ref), -1, keepdims=True)
    so_ref[...] = s


def _rwkv_steps(r, lw, k, v, a, b, s0, y_pairs):
    all_states, layer, earlier = s0
    bsz = all_states.shape[1]
    t = TOKEN_TILE // bsz
    bb = RWKV_STEP_BATCH_BLOCK
    tile_rows = TOKEN_TILE // RWKV_CHUNK
    steps = lambda x: x[-1, :tile_rows, 0].reshape(t, bsz, RWKV_HEADS, RWKV_N)
    rows = lambda x: steps(x).transpose(1, 2, 0, 3)
    vt = steps(v).transpose(1, 2, 3, 0)
    row_spec = pl.BlockSpec((bb, RWKV_HEADS, t, RWKV_N), lambda i: (i, 0, 0, 0))
    col_spec = pl.BlockSpec((bb, RWKV_HEADS, RWKV_N, t), lambda i: (i, 0, 0, 0))
    s_spec = pl.BlockSpec((None, bb, RWKV_HEADS, RWKV_N, RWKV_N), lambda i: (layer, i, 0, 0, 0))
    args = [rows(r), rows(lw), rows(k), rows(a), rows(b), vt, all_states]
    in_specs = [row_spec] * 5 + [col_spec, s_spec]
    aliases = {}
    if earlier is not None:
        aliases = {len(args): 1}
        in_specs.append(pl.BlockSpec(memory_space=pl.ANY))
        args.append(earlier)
    yt, s_new = pl.pallas_call(
        _rwkv_step_kernel,
        grid=(bsz // bb,),
        in_specs=in_specs,
        out_specs=[col_spec, s_spec],
        out_shape=[jax.ShapeDtypeStruct((bsz, RWKV_HEADS, RWKV_N, t), F32),
                   jax.ShapeDtypeStruct(all_states.shape, F32)],
        input_output_aliases=aliases,
        compiler_params=_params("parallel"),
        name="rwkv_steps",
    )(*args)
    y_tile = yt.transpose(3, 0, 1, 2).reshape(tile_rows, RWKV_CHUNK, RWKV_W)
    return y_pairs.at[-1, :tile_rows, 0].set(y_tile), s_new


def _post_kernel(n_w, tokens, xa_ref, xb_in_ref, yrw_ref, r_ref, k_ref, v_ref, g_ref, yret_ref, ga_ref, gb_ref, bd_ref,
                 rk_ref, gng_ref, gnb_ref, ln_g_ref, ln_b_ref, wr_ref, br_ref, *rest):
    wret, wrw, wo = rest[:n_w], rest[n_w:2 * n_w], rest[2 * n_w:3 * n_w]
    x1_ref, xb_ref, ids_ref, probs_ref = rest[3 * n_w:]
    x = _token_tile(tokens, pl.program_id(0), xa_ref, xb_in_ref)
    bd = bd_ref[...]
    y = _from_pair_tile(yrw_ref)
    mu = _head_sum(y, bd) * (1.0 / RWKV_N)
    d = y - mu
    var = _head_sum(d * d, bd) * (1.0 / RWKV_N)
    yn = d * lax.rsqrt(var + RWKV_GN_EPS) * gng_ref[...] + gnb_ref[...]
    bonus = _head_sum(_from_pair_tile(r_ref) * _from_pair_tile(k_ref) * rk_ref[...], bd) * _from_pair_tile(v_ref)
    yb = (yn + bonus) * _from_pair_tile(g_ref)
    merged = _sigmoid(ga_ref[...]) * _mm_w(yret_ref[...], wret) + _sigmoid(gb_ref[...]) * _mm_w(yb, wrw)
    out = _mm_w(merged, wo)
    x1 = _layer_norm(DN_ALPHA * x + out, ln_g_ref[...], ln_b_ref[...])
    x1_ref[...] = x1
    xb_ref[...] = x1.astype(BF16)

    logits = _mm(x1, wr_ref[...], 'ik,kj->ij', EXACT_PASSES) + br_ref[...]
    lane = lax.broadcasted_iota(jnp.int32, logits.shape, 1)
    work = logits
    ids, vals = [], []
    for _ in range(TOP_K):
        m = jnp.max(work, -1, keepdims=True)
        idx = jnp.min(jnp.where(work == m, lane, N_EXPERTS), -1, keepdims=True)
        ids.append(idx)
        vals.append(m)
        work = jnp.where(lane == idx, -jnp.inf, work)
    exps = [jnp.exp(m - vals[0]) for m in vals]
    inv_den = 1.0 / sum(exps)
    slot = lax.broadcasted_iota(jnp.int32, ids_ref.shape, 1)
    ids_out = jnp.zeros(ids_ref.shape, jnp.int32)
    probs_out = jnp.zeros(probs_ref.shape, F32)
    for j in range(TOP_K):
        ids_out = jnp.where(slot == j, ids[j], ids_out)
        probs_out = jnp.where(slot == j, exps[j] * inv_den, probs_out)
    ids_ref[...] = ids_out
    probs_ref[...] = probs_out


def _post(tokens, yrw, r, k, v, g, yret, gates, seq_len, p, l, passes):
    tm = TOKEN_TILE
    n = (tokens[3] + 1) * tm
    row = lambda a: a.reshape(1, -1)
    tile = lambda w, j=0: pl.BlockSpec((tm, w), lambda i: (i, j))
    full = lambda a: pl.BlockSpec(a.shape, lambda i: (0,) * a.ndim)
    pair = _pair_tile_spec(seq_len // tm, RWKV_W)
    weights = (_split_weight(p['w_ret_out'][l], passes) + _split_weight(p['w_rwkv_out'][l], passes)
               + _split_weight(p['w_o'][l], passes))
    consts = [_head_blockdiag(), row(p['rw_r_k'][l]), row(p['rw_gn_g'][l]), row(p['rw_gn_b'][l]),
              row(p['ln1_g'][l]), row(p['ln1_b'][l]), p['w_router'][l], row(p['b_router'][l]), *weights]
    in_specs = (_token_specs(tokens, tm, D_MODEL, lambda i: i) + [pair] * 5
                + [tile(RET_V_W), tile(D_MODEL, 0), tile(D_MODEL, 1)] + [full(a) for a in consts])
    return pl.pallas_call(
        functools.partial(_post_kernel, len(weights) // 3, (None, None) + tokens[2:]),
        grid=(n // tm,),
        in_specs=in_specs,
        out_specs=[tile(D_MODEL), tile(D_MODEL), tile(TOP_K), tile(TOP_K)],
        out_shape=[jax.ShapeDtypeStruct((n, D_MODEL), F32), jax.ShapeDtypeStruct((n, D_MODEL), BF16),
                   jax.ShapeDtypeStruct((n, TOP_K), jnp.int32), jax.ShapeDtypeStruct((n, TOP_K), F32)],
        compiler_params=_params("parallel"),
        name="merge_ln_router",
    )(tokens[0], tokens[1], yrw, r, k, v, g, yret, gates, gates, *consts)


SEG_ALIGN = 8
ROUTE_TILE = 512
LOCAL_ROWS = 2304
PROB_LANES = 128
ROW_W = D_MODEL + PROB_LANES


def _start_segments(i, cnt_ref, lst_ref, gst_ref, make_copy):
    def per_expert(e, carry):
        j = i * N_EXPERTS + e
        rows = pl.multiple_of(cnt_ref[j], SEG_ALIGN)

        @pl.when(rows > 0)
        def _():
            make_copy(pl.multiple_of(lst_ref[j], SEG_ALIGN), pl.multiple_of(gst_ref[j], SEG_ALIGN), rows).start()

        return carry

    lax.fori_loop(0, N_EXPERTS, per_expert, 0)


def _wait_segments(i, tot_ref, make_copy):
    make_copy(0, 0, pl.multiple_of(tot_ref[i], SEG_ALIGN)).wait()


def _dispatch_kernel(cnt_ref, lst_ref, gst_ref, tot_ref, ids_ref, probs_ref, lstart_ref, xb_ref, xs_hbm, buf, sem):
    i = pl.program_id(0)
    tm = xb_ref.shape[0]
    ids = ids_ref[0]
    probs = probs_ref[0]
    expert = lax.broadcasted_iota(jnp.int32, (N_EXPERTS, tm), 0)
    picks = [ids[k:k + 1, :] == expert for k in range(TOP_K)]
    picked = sum(pk.astype(F32) for pk in picks)
    m = lax.broadcasted_iota(jnp.int32, (tm, tm), 0)
    n = lax.broadcasted_iota(jnp.int32, (tm, tm), 1)
    earlier = (m < n).astype(BF16)
    rank = jnp.dot(picked.astype(BF16), earlier, preferred_element_type=F32)
    pos = lstart_ref[0] + rank
    row = lax.broadcasted_iota(jnp.int32, (LOCAL_ROWS, tm), 0)
    perm = jnp.zeros((LOCAL_ROWS, tm), F32)
    weight = jnp.zeros((LOCAL_ROWS, tm), F32)
    for k in range(TOP_K):
        lpos = jnp.sum(jnp.where(picks[k], pos, 0.0), 0, keepdims=True).astype(jnp.int32)
        hit = row == lpos
        perm = jnp.where(hit, 1.0, perm)
        weight = jnp.where(hit, probs[k:k + 1, :], weight)
    rows = jnp.dot(perm.astype(BF16), xb_ref[...], preferred_element_type=F32)
    row_weight = jnp.broadcast_to(jnp.sum(weight, 1, keepdims=True), (LOCAL_ROWS, PROB_LANES))

    def copies_of(slot):
        return lambda lo, go, size: pltpu.make_async_copy(buf.at[slot, pl.ds(lo, size)], xs_hbm.at[pl.ds(go, size)],
                                                          sem.at[slot])

    slot = i % 2

    @pl.when(i >= 2)
    def _():
        _wait_segments(i - 2, tot_ref, copies_of(slot))

    buf[slot, :, :D_MODEL] = rows
    buf[slot, :, D_MODEL:] = row_weight
    _start_segments(i, cnt_ref, lst_ref, gst_ref, copies_of(slot))

    @pl.when(i == pl.num_programs(0) - 1)
    def _():
        @pl.when(i >= 1)
        def _():
            _wait_segments(i - 1, tot_ref, copies_of(1 - slot))

        _wait_segments(i, tot_ref, copies_of(slot))


def _round_up(x, m):
    return (x + m - 1) // m * m


def _routing_tables(ids):
    n = ids.shape[0]
    nt = n // ROUTE_TILE
    picked = jnp.sum(ids[:, :, None] == jnp.arange(N_EXPERTS, dtype=jnp.int32)[None, None, :], 1)
    cnt = jnp.sum(picked.reshape(nt, ROUTE_TILE, N_EXPERTS), 1).astype(jnp.int32)
    cnt = _round_up(cnt, SEG_ALIGN)
    lstart = jnp.cumsum(cnt, 1) - cnt
    per_expert = jnp.sum(cnt, 0)
    region = _round_up(per_expert, EXPERT_ROW_TILE)
    gstart = (jnp.cumsum(region) - region)[None, :] + jnp.cumsum(cnt, 0) - cnt
    return cnt, lstart.astype(jnp.int32), gstart.astype(jnp.int32), jnp.sum(cnt, 1), region


def _table_args(tables):
    cnt, lstart, gstart, tile_rows, _ = tables
    return cnt.reshape(-1), lstart.reshape(-1), gstart.reshape(-1), tile_rows


def _dispatch(xb, ids, probs, tables, n_rows):
    n = xb.shape[0]
    tm = ROUTE_TILE
    nt = n // tm
    lstart = tables[1]
    to_lanes = lambda a: a.reshape(nt, tm, TOP_K).transpose(0, 2, 1)
    grid_spec = pltpu.PrefetchScalarGridSpec(
        num_scalar_prefetch=4,
        grid=(nt,),
        in_specs=[pl.BlockSpec((1, TOP_K, tm), lambda i, *_: (i, 0, 0)), pl.BlockSpec((1, TOP_K, tm), lambda i, *_: (i, 0, 0)),
                  pl.BlockSpec((1, N_EXPERTS, 1), lambda i, *_: (i, 0, 0)), pl.BlockSpec((tm, D_MODEL), lambda i, *_: (i, 0))],
        out_specs=pl.BlockSpec(memory_space=pl.ANY),
        scratch_shapes=[pltpu.VMEM((2, LOCAL_ROWS, ROW_W), F32), pltpu.SemaphoreType.DMA((2,))],
    )
    return pl.pallas_call(
        _dispatch_kernel,
        grid_spec=grid_spec,
        out_shape=jax.ShapeDtypeStruct((n_rows, ROW_W), F32),
        compiler_params=_params("arbitrary"),
        name="dispatch",
    )(*_table_args(tables), to_lanes(ids), to_lanes(probs), lstart.astype(F32)[:, :, None], xb)


def _expert_kernel(tile_ref, exp_ref, valid_ref, x_ref, wgu_ref, bgu_ref, wd_ref, bd_ref, o_ref, wgu_s, wd_s):
    i = pl.program_id(0)

    @pl.when(valid_ref[i] == 1)
    def _():
        @pl.when((i == 0) | (exp_ref[jnp.maximum(i - 1, 0)] != exp_ref[i]))
        def _():
            wgu_s[...] = wgu_ref[0, 0].astype(BF16)
            wd_s[...] = wd_ref[0, 0].astype(BF16)

        h = jnp.dot(x_ref[:, :D_MODEL].astype(BF16), wgu_s[...], preferred_element_type=F32) + bgu_ref[0, 0]
        gate = jnp.minimum(h[:, :D_FF], SWIGLU_LIMIT)
        up = jnp.clip(h[:, D_FF:], -SWIGLU_LIMIT, SWIGLU_LIMIT)
        act = gate * _sigmoid(SWIGLU_ALPHA * gate) * (up + 1.0)
        y = jnp.dot(act.astype(BF16), wd_s[...], preferred_element_type=F32) + bd_ref[0, 0]
        o_ref[...] = y * x_ref[:, D_MODEL:D_MODEL + 1]


def _expert_schedule(region, n_rows, tm):
    n_entries = n_rows // tm
    tile_end = jnp.cumsum(region // tm)
    total = tile_end[-1]
    t = jnp.minimum(jnp.arange(n_entries), total - 1).astype(jnp.int32)
    e = jnp.sum(tile_end[None, :] <= t[:, None], -1).astype(jnp.int32)
    valid = (jnp.arange(n_entries) < total).astype(jnp.int32)
    return t, e, valid


def _experts(xs, sched, p, l):
    n_rows = xs.shape[0]
    tm = EXPERT_ROW_TILE
    bgu = p['b_gate_up'].reshape(DEPTH, N_EXPERTS, 1, 2 * D_FF)
    bdn = p['b_down'].reshape(DEPTH, N_EXPERTS, 1, D_MODEL)
    by_tile = lambda w: pl.BlockSpec((tm, w), lambda i, t, e, *_: (t[i], 0))
    by_expert = lambda a, b: pl.BlockSpec((1, 1, a, b), lambda i, t, e, *_: (l, e[i], 0, 0))
    grid_spec = pltpu.PrefetchScalarGridSpec(
        num_scalar_prefetch=3,
        grid=(sched[0].shape[0],),
        in_specs=[by_tile(ROW_W), by_expert(D_MODEL, 2 * D_FF), by_expert(1, 2 * D_FF),
                  by_expert(D_FF, D_MODEL), by_expert(1, D_MODEL)],
        out_specs=by_tile(D_MODEL),
        scratch_shapes=[pltpu.VMEM((D_MODEL, 2 * D_FF), BF16), pltpu.VMEM((D_FF, D_MODEL), BF16)],
    )
    return pl.pallas_call(
        _expert_kernel,
        grid_spec=grid_spec,
        out_shape=jax.ShapeDtypeStruct((n_rows, D_MODEL), F32),
        compiler_params=_params("arbitrary"),
        name="experts",
    )(*sched, xs, p['w_gate_up'], bgu, p['w_down'], bdn)


def _combine_kernel(tile0, cnt_ref, lst_ref, gst_ref, tot_ref, ids_ref, lstart_ref, x_ref, ln_g_ref, ln_b_ref, ys_hbm,
                    o_ref, buf, sem):
    step = pl.program_id(0)
    i = step + tile0
    tm = x_ref.shape[0]
    slot = step % 2

    def copies_of(slot):
        return lambda lo, go, size: pltpu.make_async_copy(ys_hbm.at[pl.ds(go, size)], buf.at[slot, pl.ds(lo, size)],
                                                          sem.at[slot])

    def fetch(tile, slot):
        buf[slot, ROUTE_TILE * TOP_K:, :] = jnp.zeros((LOCAL_ROWS - ROUTE_TILE * TOP_K, D_MODEL), F32)
        _start_segments(tile, cnt_ref, lst_ref, gst_ref, copies_of(slot))

    @pl.when(step == 0)
    def _():
        fetch(i, slot)

    @pl.when(step + 1 < pl.num_programs(0))
    def _():
        fetch(i + 1, 1 - slot)

    ids = ids_ref[...]
    expert = lax.broadcasted_iota(jnp.int32, (tm, N_EXPERTS), 1)
    picks = [ids[:, k:k + 1] == expert for k in range(TOP_K)]
    picked = sum(pk.astype(F32) for pk in picks)
    m = lax.broadcasted_iota(jnp.int32, (tm, tm), 0)
    n = lax.broadcasted_iota(jnp.int32, (tm, tm), 1)
    earlier = (n < m).astype(BF16)
    rank = jnp.dot(earlier, picked.astype(BF16), preferred_element_type=F32)
    pos = lstart_ref[0] + rank
    col = lax.broadcasted_iota(jnp.int32, (tm, LOCAL_ROWS), 1)
    perm = jnp.zeros((tm, LOCAL_ROWS), F32)
    for k in range(TOP_K):
        lpos = jnp.sum(jnp.where(picks[k], pos, 0.0), 1, keepdims=True).astype(jnp.int32)
        perm = jnp.where(col == lpos, 1.0, perm)

    _wait_segments(i, tot_ref, copies_of(slot))
    moe = jnp.dot(perm.astype(BF16), buf[slot].astype(BF16), preferred_element_type=F32)
    o_ref[...] = _layer_norm(DN_ALPHA * x_ref[...] + moe, ln_g_ref[...], ln_b_ref[...])


def _combine(x1, ys, ids, tables, ln_g, ln_b, tile0, n_tiles):
    tm = ROUTE_TILE
    lstart = tables[1]
    at = lambda i, *_: (i + tile0, 0)
    grid_spec = pltpu.PrefetchScalarGridSpec(
        num_scalar_prefetch=4,
        grid=(n_tiles,),
        in_specs=[pl.BlockSpec((tm, TOP_K), at), pl.BlockSpec((1, 1, N_EXPERTS), lambda i, *_: (i + tile0, 0, 0)),
                  pl.BlockSpec((tm, D_MODEL), at), pl.BlockSpec((1, D_MODEL), lambda i, *_: (0, 0)),
                  pl.BlockSpec((1, D_MODEL), lambda i, *_: (0, 0)), pl.BlockSpec(memory_space=pl.ANY)],
        out_specs=pl.BlockSpec((tm, D_MODEL), lambda i, *_: (i, 0)),
        scratch_shapes=[pltpu.VMEM((2, LOCAL_ROWS, D_MODEL), F32), pltpu.SemaphoreType.DMA((2,))],
    )
    return pl.pallas_call(
        functools.partial(_combine_kernel, tile0),
        grid_spec=grid_spec,
        out_shape=jax.ShapeDtypeStruct((n_tiles * tm, D_MODEL), F32),
        compiler_params=_params("arbitrary"),
        name="combine_ln",
    )(*_table_args(tables), ids, lstart.astype(F32)[:, None, :], x1, ln_g.reshape(1, -1), ln_b.reshape(1, -1), ys)


def _moe(x1, xb, ids, probs, p, l, parts):
    n = x1.shape[0]
    tm = EXPERT_ROW_TILE
    seg_rows = n * TOP_K + (n // ROUTE_TILE) * N_EXPERTS * (SEG_ALIGN - 1)
    n_rows = _round_up(seg_rows + N_EXPERTS * (tm - SEG_ALIGN), tm)
    tables = _routing_tables(ids)
    xs = _dispatch(xb, ids, probs, tables, n_rows)
    ys = _experts(xs, _expert_schedule(tables[4], n_rows, tm), p, l)
    return [_combine(x1, ys, ids, tables, p['ln2_g'][l], p['ln2_b'][l], *part) for part in parts]


def _pad_time(a, tp):
    return jnp.pad(a, ((0, 0), (0, tp - a.shape[1]), (0, 0)))


def kernel(x_prompt, x_sample, state_ret, state_rwkv, state_shift, w_in, ret_gn_g, ret_gn_b, w_ret_out, rw_mu, rw_w0, rw_w_up, rw_a0, rw_a_up, rw_g_up, rw_k_k, rw_k_a, rw_r_k, rw_gn_g, rw_gn_b, rw_v0, rw_vres_down, rw_vres_up, w_rwkv_out, w_o, ln1_g, ln1_b, w_router, b_router, w_gate_up, b_gate_up, w_down, b_down, ln2_g, ln2_b):
    p = dict(w_ret_out=w_ret_out, rw_mu=rw_mu, rw_w0=rw_w0, rw_w_up=rw_w_up, rw_a0=rw_a0, rw_a_up=rw_a_up,
             rw_g_up=rw_g_up, rw_k_k=rw_k_k, rw_k_a=rw_k_a, rw_r_k=rw_r_k, rw_gn_g=rw_gn_g, rw_gn_b=rw_gn_b,
             w_rwkv_out=w_rwkv_out, w_o=w_o, ln1_g=ln1_g, ln1_b=ln1_b, w_router=w_router, b_router=b_router,
             w_gate_up=w_gate_up, b_gate_up=b_gate_up, w_down=w_down, b_down=b_down, ln2_g=ln2_g, ln2_b=ln2_b)
    bp, tp, _ = x_prompt.shape
    bs, ts, _ = x_sample.shape
    np_, ns = bp * tp, bs * ts
    pos_p = jnp.arange(tp, dtype=F32)
    ts_ret = 8
    pos_s = PAST_LEN + jnp.arange(ts_ret, dtype=F32)

    assert ns == TOKEN_TILE and np_ % TOKEN_TILE == 0
    n_prompt_tiles = np_ // TOKEN_TILE
    tokens = (x_prompt.reshape(np_, D_MODEL), x_sample.transpose(1, 0, 2).reshape(ns, D_MODEL), 0, n_prompt_tiles)
    outs = {k: [] for k in ('ret_p', 'rw_p', 'sh_p', 'sh_s')}
    sret_s = srw_s = None
    v_first = None
    u_off = RET_W
    g_off = RET_W + SHIFT_W
    for l in range(DEPTH):
        passes = EXACT_PASSES if l == 0 else 1
        z_ret = _matmul(tokens, w_in[l][:, :u_off], TOKEN_TILE, RET_W if passes == 1 else RET_W // 2, passes)
        u = _matmul(tokens, w_in[l][:, u_off:g_off], TOKEN_TILE, SHIFT_W, passes)
        gates = _matmul(tokens, w_in[l][:, g_off:], TOKEN_TILE, D_MODEL, passes)

        zr_s = _pad_time(z_ret[np_:].reshape(ts, bs, RET_W).transpose(1, 0, 2), ts_ret)
        yret, sret_p = _retention(z_ret.reshape(-1, RET_CHUNK, RET_W), pos_p, tp, None, ret_gn_g[l], ret_gn_b[l], 1,
                                  passes, n_seq=bp)
        yret_s, sret_s = _retention(zr_s, pos_s, ts, (state_ret, l, sret_s), ret_gn_g[l], ret_gn_b[l], 8, passes)
        yret = yret.reshape(-1, RET_V_W).at[np_:].set(yret_s[:, :ts].transpose(1, 0, 2).reshape(ns, RET_V_W))

        vres = None if l == 0 else (rw_v0[l - 1], rw_vres_down[l - 1], rw_vres_up[l - 1])
        r, lw, k, v, a, b, g = _rwkv_pre(u, state_shift[l], np_, tp, p, l, v_first, vres)
        if l == 0:
            v_first = v
        yrw, srw_p = _rwkv_chunks(r, lw, k, v, a, b, passes)
        yrw, srw_s = _rwkv_steps(r, lw, k, v, a, b, (state_rwkv, l, srw_s), yrw)

        x1, xb, ids, probs = _post(tokens, yrw, r, k, v, g, yret, gates, tp, p, l, passes)
        if l + 1 < DEPTH:
            x, = _moe(x1, xb, ids, probs, p, l, [(0, n_prompt_tiles + 1)])
            tokens = (x, x, n_prompt_tiles, n_prompt_tiles)
        else:
            y_p, y_s = _moe(x1, xb, ids, probs, p, l, [(0, n_prompt_tiles), (n_prompt_tiles, 1)])

        outs['ret_p'].append(sret_p)
        outs['rw_p'].append(srw_p)
        outs['sh_p'].append(u[tp - 1:np_:tp])
        outs['sh_s'].append(u[np_ + ns - bs:])

    y_prompt = y_p.reshape(bp, tp, D_MODEL)
    y_sample = y_s.reshape(ts, bs, D_MODEL).transpose(1, 0, 2)
    st = {k: jnp.stack(v) for k, v in outs.items()}
    return (y_prompt, y_sample, st['ret_p'], st['rw_p'], st['sh_p'], sret_s, srw_s, st['sh_s'])
```

```python
import functools

import jax
import jax.numpy as jnp
from jax import lax
from jax.experimental import pallas as pl
from jax.experimental.pallas import tpu as pltpu

F32 = jnp.float32
BF16 = jnp.bfloat16
HI = lax.Precision.HIGHEST

D_MODEL = 1024
DEPTH = 2
PAST_LEN = 16384
RET_HEADS = 4
RET_DK = 128
RET_DV = 256
RET_QK_W = RET_HEADS * RET_DK
RET_V_W = RET_HEADS * RET_DV
RET_W = 2 * RET_QK_W + 2 * RET_V_W
RET_CHUNK = 128
ROPE_BASE = 10000.0
RWKV_HEADS = 8
RWKV_N = 64
RWKV_W = RWKV_HEADS * RWKV_N
LORA_W = 64
LORA_A = 64
LORA_G = 128
SHIFT_W = 3 * RWKV_W + LORA_W + LORA_A + LORA_G
RWKV_CHUNK = 64
N_EXPERTS = 32
TOP_K = 4
D_FF = D_MODEL
SWIGLU_LIMIT = 7.0
SWIGLU_ALPHA = 1.702
DN_ALPHA = (2 * DEPTH) ** 0.25
LN_EPS = 1e-5
RET_GN_EPS = 1e-5
RWKV_GN_EPS = 64e-5

VMEM_LIMIT = 56 * 1024 * 1024
TOKEN_TILE = 512
EXPERT_ROW_TILE = 512
RWKV_STEP_BATCH_BLOCK = 8
EXACT_PASSES = 3
RWKV_REFINE_STEPS = 1


def _params(*sem):
    return pltpu.CompilerParams(dimension_semantics=sem, vmem_limit_bytes=VMEM_LIMIT)


def _split(x):
    hi = x.astype(BF16)
    lo = (x - hi.astype(F32)).astype(BF16)
    return hi, lo


def _split_kernel(w_ref, hi_ref, lo_ref):
    hi_ref[...], lo_ref[...] = _split(w_ref[...])


def _split_weight(w, passes):
    if passes == 1:
        return (w.astype(BF16),)
    rows = 256
    spec = pl.BlockSpec((rows, w.shape[1]), lambda i: (i, 0))
    return tuple(pl.pallas_call(
        _split_kernel,
        grid=(w.shape[0] // rows,),
        in_specs=[spec],
        out_specs=[spec, spec],
        out_shape=[jax.ShapeDtypeStruct(w.shape, BF16)] * 2,
        compiler_params=_params("parallel"),
        name="split_weight",
    )(w))


def _mm(a, b, spec, passes):
    dg = lambda x, y: jnp.einsum(spec, x, y, preferred_element_type=F32)
    if passes == 1:
        return dg(a.astype(BF16), b.astype(BF16))
    ah, al = _split(a)
    bh, bl = _split(b)
    return dg(ah, bh) + (dg(ah, bl) + dg(al, bh))


def _mm_w(a, w_refs):
    dg = lambda x, y: jnp.dot(x, y, preferred_element_type=F32)
    if len(w_refs) == 1:
        return dg(a.astype(BF16), w_refs[0][...])
    ah, al = _split(a)
    return dg(ah, w_refs[0][...]) + (dg(ah, w_refs[1][...]) + dg(al, w_refs[0][...]))


def _dot_hi(a, b):
    return jnp.dot(a, b, precision=HI, preferred_element_type=F32)


def _sigmoid(x):
    return 1.0 / (1.0 + jnp.exp(-x))


def _layer_norm(x, g, b):
    mu = jnp.mean(x, -1, keepdims=True)
    d = x - mu
    var = jnp.mean(d * d, -1, keepdims=True)
    return d * lax.rsqrt(var + LN_EPS) * g + b


def _token_specs(tokens, tm, w, tile_of):
    _, _, b_block, n_a = tokens
    return [pl.BlockSpec((tm, w), lambda *g: (jnp.minimum(tile_of(*g), n_a - 1), 0)),
            pl.BlockSpec((tm, w), lambda *g: (b_block, 0))]


def _token_tile(tokens, i, a_ref, b_ref):
    return jnp.where(i < tokens[3], a_ref[...], b_ref[...])


def _matmul_kernel(tokens, xa_ref, xb_ref, *refs):
    o_ref = refs[-1]
    o_ref[...] = _mm_w(_token_tile(tokens, pl.program_id(1), xa_ref, xb_ref), refs[:-1])


def _matmul(tokens, w, tm, tn, passes):
    k, n = w.shape
    n_tiles = tokens[3] + 1
    ws = _split_weight(w, passes)
    return pl.pallas_call(
        functools.partial(_matmul_kernel, (None, None) + tokens[2:]),
        grid=(n // tn, n_tiles),
        in_specs=_token_specs(tokens, tm, k, lambda j, i: i) + [pl.BlockSpec((k, tn), lambda j, i: (0, j))] * len(ws),
        out_specs=pl.BlockSpec((tm, tn), lambda j, i: (i, j)),
        out_shape=jax.ShapeDtypeStruct((n_tiles * tm, n), F32),
        compiler_params=_params("parallel", "parallel"),
        name="in_proj",
    )(tokens[0], tokens[1], *ws)


def _ret_kernel(has_state, passes, q_ref, k_ref, v_ref, g_ref, cos_ref, sin_ref, dm_ref, qd_ref, kd_ref, cd_ref,
                gng_ref, gnb_ref, *rest):
    if has_state:
        s0_ref, y_ref, so_ref, s_scr = rest[0], *rest[-3:]
    else:
        y_ref, so_ref, s_scr = rest
    c = pl.program_id(1)
    bb, cl, _ = q_ref.shape
    nh = RET_HEADS

    def heads(x, w):
        return jnp.concatenate([x[:, :, h * w:(h + 1) * w] for h in range(nh)], 0)

    def per_head(ref):
        return jnp.concatenate([jnp.broadcast_to(ref[h], (bb,) + ref.shape[1:]) for h in range(nh)], 0)

    @pl.when(c == 0)
    def _():
        if has_state:
            s_scr[...] = jnp.concatenate([s0_ref[:, h] for h in range(nh)], 0)
        else:
            s_scr[...] = jnp.zeros_like(s_scr)

    cos = cos_ref[...]
    sin = sin_ref[...]
    mm = functools.partial(_mm, passes=passes)

    def rope(x):
        x2 = x.reshape(nh * bb * cl, RET_DK)
        rot = pltpu.roll(x2, RET_DK // 2, axis=1).reshape(nh * bb, cl, RET_DK)
        return x * cos + rot * sin

    q = rope(heads(q_ref[...], RET_DK))
    k = rope(heads(k_ref[...], RET_DK)) * (RET_DK ** -0.5)
    v = heads(v_ref[...], RET_DV)
    s = s_scr[...]
    sc = mm(q, k, 'bid,bjd->bij') * per_head(dm_ref)
    intra = mm(sc, v, 'bij,bje->bie')
    cross = mm(q, s, 'bid,bde->bie') * per_head(qd_ref)
    s_new = s * per_head(cd_ref) + mm(k * per_head(kd_ref), v, 'bjd,bje->bde')
    s_scr[...] = s_new

    y = intra + cross
    mu = jnp.mean(y, -1, keepdims=True)
    d = y - mu
    var = jnp.mean(d * d, -1, keepdims=True)
    yn = d * lax.rsqrt(var + RET_GN_EPS)
    for h in range(nh):
        cols = slice(h * RET_DV, (h + 1) * RET_DV)
        rg = g_ref[:, :, cols]
        y_ref[:, :, cols] = (yn[h * bb:(h + 1) * bb] * gng_ref[:, cols] + gnb_ref[:, cols]) * (rg * _sigmoid(rg))

    @pl.when(c == pl.num_programs(1) - 1)
    def _():
        for h in range(nh):
            so_ref[:, h] = s_new[h * bb:(h + 1) * bb]


def _retention(z, pos, t_real, s0, gn_g, gn_b, bb, passes, n_seq=None):
    flat = n_seq is not None
    tp = pos.shape[0]
    b = n_seq if flat else z.shape[0]
    cl = RET_CHUNK if t_real % RET_CHUNK == 0 else tp
    cr = min(cl, t_real)
    nc = tp // cl
    at = (lambda bi, c: (bi * nc + c, 0)) if flat else (lambda bi, c: (bi, c))
    half = RET_DK // 2
    inv = ROPE_BASE ** (-jnp.arange(half, dtype=F32) / half)
    ang = pos[:, None] * inv[None, :]
    cos = jnp.concatenate([jnp.cos(ang), jnp.cos(ang)], -1)
    sin = jnp.concatenate([-jnp.sin(ang), jnp.sin(ang)], -1)
    lg = jnp.log1p(-jnp.exp2(-5.0 - jnp.arange(RET_HEADS, dtype=F32)))
    i = jnp.arange(cl, dtype=F32)
    real = i < cr
    diff = i[:, None] - i[None, :]
    ok = (diff >= 0) & real[:, None] & real[None, :]
    dmask = jnp.exp(jnp.where(ok[None], diff[None] * lg[:, None, None], -jnp.inf))
    q_dec = jnp.exp((i[None, :] + 1.0) * lg[:, None])[..., None]
    k_dec = jnp.where(real[None, :], jnp.exp((cr - 1.0 - i)[None, :] * lg[:, None]), 0.0)[..., None]
    c_dec = jnp.exp(cr * lg)[:, None, None]

    has_state = s0 is not None
    cols = lambda w, j: pl.BlockSpec((bb, cl, w), lambda bi, c: at(bi, c) + (j,))
    full = lambda a: pl.BlockSpec(a.shape, lambda bi, c: (0,) * a.ndim)
    consts = [dmask, q_dec, k_dec, c_dec, gn_g.reshape(1, -1), gn_b.reshape(1, -1)]
    in_specs = [cols(RET_QK_W, 0), cols(RET_QK_W, 1), cols(RET_V_W, 1), cols(RET_V_W, 2),
                pl.BlockSpec((cl, RET_DK), lambda bi, c: (c, 0)), pl.BlockSpec((cl, RET_DK), lambda bi, c: (c, 0))]
    in_specs += [full(a) for a in consts]
    args = [z, z, z, z, cos, sin] + consts
    aliases = {}
    if has_state:
        all_states, layer, earlier = s0
        s_spec = pl.BlockSpec((None, bb, RET_HEADS, RET_DK, RET_DV), lambda bi, c: (layer, bi, 0, 0, 0))
        s_shape = all_states.shape
        in_specs.append(s_spec)
        args.append(all_states)
        if earlier is not None:
            aliases = {len(args): 1}
            in_specs.append(pl.BlockSpec(memory_space=pl.ANY))
            args.append(earlier)
    else:
        s_spec = pl.BlockSpec((bb, RET_HEADS, RET_DK, RET_DV), lambda bi, c: (bi, 0, 0, 0))
        s_shape = (b, RET_HEADS, RET_DK, RET_DV)
    return pl.pallas_call(
        functools.partial(_ret_kernel, has_state, passes),
        grid=(b // bb, nc),
        in_specs=in_specs,
        out_specs=[cols(RET_V_W, 0), s_spec],
        out_shape=[jax.ShapeDtypeStruct(z.shape[:2] + (RET_V_W,), F32), jax.ShapeDtypeStruct(s_shape, F32)],
        scratch_shapes=[pltpu.VMEM((RET_HEADS * bb, RET_DK, RET_DV), F32)],
        input_output_aliases=aliases,
        compiler_params=_params("parallel", "arbitrary"),
        name="retention",
    )(*args)


def _head_sum(x, bd):
    hi, lo = _split(x)
    return jnp.dot(hi, bd, preferred_element_type=F32) + jnp.dot(lo, bd, preferred_element_type=F32)


def _pair_shape(n_pairs, chunks, w):
    return (n_pairs + 1, chunks, 2, RWKV_CHUNK, w)


def _pair_tile_spec(tiles_per_seq, w):
    rows = TOKEN_TILE // RWKV_CHUNK
    return pl.BlockSpec((1, rows, 1, RWKV_CHUNK, w),
                        lambda i: (i // (2 * tiles_per_seq), i % tiles_per_seq, (i // tiles_per_seq) % 2, 0, 0))


def _to_pair_tile(ref, x):
    ref[0, :, 0] = x.reshape(TOKEN_TILE // RWKV_CHUNK, RWKV_CHUNK, x.shape[-1])


def _from_pair_tile(ref):
    return ref[0, :, 0].reshape(TOKEN_TILE, ref.shape[-1])


def _rwkv_pre_kernel(has_vres, n_prompt_tiles, tiles_per_seq, u_ref, tail_ref, shift_ref, mu_ref, w0_ref, wup_ref,
                     a0_ref, aup_ref, gup_ref, kk_ref, ka_ref, bd_ref, *rest):
    if has_vres:
        vf_ref, v0_ref, vd_ref, vu_ref, r_o, lw_o, k_o, v_o, a_o, b_o, g_o = rest
    else:
        r_o, lw_o, k_o, v_o, a_o, b_o, g_o = rest
    i = pl.program_id(0)
    u = u_ref[...]
    tm = u.shape[0]
    row = lax.broadcasted_iota(jnp.int32, (tm, 1), 0)
    before = jnp.where(i % tiles_per_seq == 0, 0.0, tail_ref[tail_ref.shape[0] - 1:, :])
    prev_prompt = jnp.where(row == 0, before, pltpu.roll(u, 1, axis=0))
    n_seq = shift_ref.shape[0]
    prev_sample = jnp.concatenate([shift_ref[...], u[:tm - n_seq]], 0)
    prev = jnp.where(i < n_prompt_tiles, prev_prompt, prev_sample)
    um = u + (prev - u) * mu_ref[...]
    w1, w2, w3 = RWKV_W, 2 * RWKV_W, 3 * RWKV_W
    r = um[:, :w1]
    kw = um[:, w1:w2]
    vw = um[:, w2:w3]
    wd = um[:, w3:w3 + LORA_W]
    ad = um[:, w3 + LORA_W:w3 + LORA_W + LORA_A]
    gd = um[:, w3 + LORA_W + LORA_A:]
    lora = lambda x, w_ref: _mm(x, w_ref[...], 'ik,kj->ij', EXACT_PASSES)
    xw = w0_ref[...] + lora(jnp.tanh(wd), wup_ref)
    softplus = jnp.maximum(-xw, 0.0) + jnp.log1p(jnp.exp(-jnp.abs(xw)))
    _to_pair_tile(lw_o, -jnp.exp(-softplus - 0.5))
    a = _sigmoid(a0_ref[...] + lora(ad, aup_ref))
    _to_pair_tile(g_o, lora(_sigmoid(gd), gup_ref))
    if has_vres:
        gate = _sigmoid(v0_ref[...] + lora(lora(vw, vd_ref), vu_ref))
        vw = vw + (_from_pair_tile(vf_ref) - vw) * gate
    kk = kw * kk_ref[...]
    norm = jnp.sqrt(_head_sum(kk * kk, bd_ref[...]))
    kk = kk / jnp.maximum(norm, 1e-12)
    _to_pair_tile(r_o, r)
    _to_pair_tile(k_o, kw * (1.0 + (a - 1.0) * ka_ref[...]))
    _to_pair_tile(v_o, vw)
    _to_pair_tile(a_o, -kk)
    _to_pair_tile(b_o, kk * a)


def _head_blockdiag():
    h = jnp.arange(RWKV_W) // RWKV_N
    return (h[:, None] == h[None, :]).astype(BF16)


def _rwkv_pre(u, shift_state, n_prompt, seq_len, p, l, v_first, vres):
    n = u.shape[0]
    tm = TOKEN_TILE
    assert n == n_prompt + tm and seq_len % tm == 0
    tiles_per_seq = seq_len // tm
    tail_rows = 8
    row = lambda a: a.reshape(1, -1)
    full = lambda a: pl.BlockSpec(a.shape, lambda i: (0,) * a.ndim)
    pair = _pair_tile_spec(tiles_per_seq, RWKV_W)
    has_vres = vres is not None
    args = [u, u, shift_state, row(p['rw_mu'][l]), row(p['rw_w0'][l]), p['rw_w_up'][l], row(p['rw_a0'][l]),
            p['rw_a_up'][l], p['rw_g_up'][l], row(p['rw_k_k'][l]), row(p['rw_k_a'][l]), _head_blockdiag()]
    in_specs = [pl.BlockSpec((tm, SHIFT_W), lambda i: (i, 0)),
                pl.BlockSpec((tail_rows, SHIFT_W), lambda i: (jnp.maximum(i * (tm // tail_rows) - 1, 0), 0))]
    in_specs += [full(a) for a in args[2:]]
    if has_vres:
        extra = [v_first, row(vres[0]), vres[1], vres[2]]
        in_specs += [pair] + [full(a) for a in extra[1:]]
        args += extra
    shape = _pair_shape(n_prompt // seq_len // 2, seq_len // RWKV_CHUNK, RWKV_W)
    return pl.pallas_call(
        functools.partial(_rwkv_pre_kernel, has_vres, n_prompt // tm, tiles_per_seq),
        grid=(n // tm,),
        in_specs=in_specs,
        out_specs=[pair] * 7,
        out_shape=[jax.ShapeDtypeStruct(shape, F32)] * 7,
        compiler_params=_params("parallel"),
        name="rwkv_pre",
    )(*args)


def _rwkv_chunk_kernel(passes, r_ref, lw_ref, k_ref, v_ref, a_ref, b_ref, tri_ref, y_ref, so_ref, s_scr):
    c = pl.program_id(1)

    @pl.when(c == 0)
    def _():
        s_scr[...] = jnp.zeros_like(s_scr)

    bb, cl = r_ref.shape[2], r_ref.shape[3]
    r_ref, lw_ref, k_ref, v_ref, a_ref, b_ref, y_ref = (ref.at[0, 0] for ref in
                                                        (r_ref, lw_ref, k_ref, v_ref, a_ref, b_ref, y_ref))
    mm = functools.partial(_mm, passes=passes)
    ti = lax.broadcasted_iota(jnp.int32, (cl, cl), 0)
    si = lax.broadcasted_iota(jnp.int32, (cl, cl), 1)
    strict = (ti > si).astype(F32)
    incl = (ti >= si).astype(F32)
    eye = (ti == si).astype(F32)

    def heads(x):
        return jnp.stack([x[bi][:, h * RWKV_N:(h + 1) * RWKV_N] for bi in range(bb) for h in range(RWKV_HEADS)])

    lw = lw_ref[...]
    cum = jnp.stack([_dot_hi(tri_ref[...], lw[bi]) for bi in range(bb)])
    last = cum[:, cl - 1:cl, :]
    e_neg = jnp.exp(-cum)
    e_end = jnp.exp(last - cum)
    at = heads(a_ref[...] * jnp.exp(cum - lw))
    rt = heads(r_ref[...] * jnp.exp(cum))
    bt = heads(b_ref[...] * e_neg)
    kt = heads(k_ref[...] * e_neg)
    bw = heads(b_ref[...] * e_end)
    kw = heads(k_ref[...] * e_end)
    wc = heads(jnp.exp(last))
    vh = heads(v_ref[...])

    lhs = jnp.concatenate([at, rt], 1)
    gram = mm(lhs, jnp.concatenate([bt, kt], 1), 'gik,gjk->gij')
    a_ab = gram[:, :cl, :cl] * strict
    a_ak = gram[:, :cl, cl:] * strict
    a_rb = gram[:, cl:, :cl] * incl
    a_rk = gram[:, cl:, cl:] * incl
    refine = RWKV_REFINE_STEPS if passes > 1 else 0
    mm_inv = functools.partial(_mm, passes=1) if refine else mm
    inv = eye + a_ab
    pw = a_ab
    for _ in range(cl.bit_length() - 2):
        pw = mm_inv(pw, pw, 'gij,gjk->gik')
        inv = inv + mm_inv(inv, pw, 'gij,gjk->gik')
    s0 = s_scr[...].reshape(bb * RWKV_HEADS, RWKV_N, RWKV_N)
    xs = mm(lhs, s0, 'gtj,gij->gti')
    av = mm(jnp.concatenate([a_ak, a_rk], 1), vh, 'gts,gsi->gti')
    rhs = xs[:, :cl] + av[:, :cl]
    u = mm_inv(inv, rhs, 'gts,gsi->gti')
    for _ in range(refine):
        u = u + mm_inv(inv, rhs - u + mm(a_ab, u, 'gts,gsi->gti'), 'gts,gsi->gti')
    y = xs[:, cl:] + av[:, cl:] + mm(a_rb, u, 'gts,gsi->gti')
    s_new = s0 * wc + mm(jnp.concatenate([u, vh], 1), jnp.concatenate([bw, kw], 1), 'gti,gtj->gij')
    s_scr[...] = s_new.reshape(bb, RWKV_HEADS, RWKV_N, RWKV_N)
    for bi in range(bb):
        for h in range(RWKV_HEADS):
            y_ref[bi, :, h * RWKV_N:(h + 1) * RWKV_N] = y[bi * RWKV_HEADS + h]

    @pl.when(c == pl.num_programs(1) - 1)
    def _():
        so_ref[...] = s_scr[...]


def _rwkv_chunks(r, lw, k, v, a, b, passes):
    n_pairs, chunks, bb, cl, _ = r.shape
    n_pairs -= 1
    seq = pl.BlockSpec((1, 1, bb, cl, RWKV_W), lambda pi, c: (pi, c, 0, 0, 0))
    s_spec = pl.BlockSpec((bb, RWKV_HEADS, RWKV_N, RWKV_N), lambda pi, c: (pi, 0, 0, 0))
    tri = (jnp.arange(cl)[:, None] >= jnp.arange(cl)[None, :]).astype(F32)
    return pl.pallas_call(
        functools.partial(_rwkv_chunk_kernel, passes),
        grid=(n_pairs, chunks),
        in_specs=[seq] * 6 + [pl.BlockSpec((cl, cl), lambda pi, c: (0, 0))],
        out_specs=[seq, s_spec],
        out_shape=[jax.ShapeDtypeStruct(r.shape, F32),
                   jax.ShapeDtypeStruct((n_pairs * bb, RWKV_HEADS, RWKV_N, RWKV_N), F32)],
        scratch_shapes=[pltpu.VMEM((bb, RWKV_HEADS, RWKV_N, RWKV_N), F32)],
        compiler_params=_params("parallel", "arbitrary"),
        name="rwkv_chunks",
    )(r, lw, k, v, a, b, tri)


def _rwkv_step_kernel(r_ref, lw_ref, k_ref, a_ref, b_ref, vt_ref, s0_ref, *rest):
    yt_ref, so_ref = rest[-2:]
    s = s0_ref[...]
    for t in range(r_ref.shape[2]):
        row = lambda ref: ref[:, :, t:t + 1, :]
        sa = jnp.sum(s * row(a_ref), -1, keepdims=True)
        s = s * jnp.exp(row(lw_ref)) + sa * row(b_ref) + vt_ref[:, :, :, t:t + 1] * row(k_ref)
        yt_ref[:, :, :, t:t + 1] = jnp.sum(s * row(r_ref), -1, keepdims=True)
    so_ref[...] = s


def _rwkv_steps(r, lw, k, v, a, b, s0, y_pairs):
    all_states, layer, earlier = s0
    bsz = all_states.shape[1]
    t = TOKEN_TILE // bsz
    bb = RWKV_STEP_BATCH_BLOCK
    tile_rows = TOKEN_TILE // RWKV_CHUNK
    steps = lambda x: x[-1, :tile_rows, 0].reshape(t, bsz, RWKV_HEADS, RWKV_N)
    rows = lambda x: steps(x).transpose(1, 2, 0, 3)
    vt = steps(v).transpose(1, 2, 3, 0)
    row_spec = pl.BlockSpec((bb, RWKV_HEADS, t, RWKV_N), lambda i: (i, 0, 0, 0))
    col_spec = pl.BlockSpec((bb, RWKV_HEADS, RWKV_N, t), lambda i: (i, 0, 0, 0))
    s_spec = pl.BlockSpec((None, bb, RWKV_HEADS, RWKV_N, RWKV_N), lambda i: (layer, i, 0, 0, 0))
    args = [rows(r), rows(lw), rows(k), rows(a), rows(b), vt, all_states]
    in_specs = [row_spec] * 5 + [col_spec, s_spec]
    aliases = {}
    if earlier is not None:
        aliases = {len(args): 1}
        in_specs.append(pl.BlockSpec(memory_space=pl.ANY))
        args.append(earlier)
    yt, s_new = pl.pallas_call(
        _rwkv_step_kernel,
        grid=(bsz // bb,),
        in_specs=in_specs,
        out_specs=[col_spec, s_spec],
        out_shape=[jax.ShapeDtypeStruct((bsz, RWKV_HEADS, RWKV_N, t), F32),
                   jax.ShapeDtypeStruct(all_states.shape, F32)],
        input_output_aliases=aliases,
        compiler_params=_params("parallel"),
        name="rwkv_steps",
    )(*args)
    y_tile = yt.transpose(3, 0, 1, 2).reshape(tile_rows, RWKV_CHUNK, RWKV_W)
    return y_pairs.at[-1, :tile_rows, 0].set(y_tile), s_new


def _post_kernel(n_w, tokens, xa_ref, xb_in_ref, yrw_ref, r_ref, k_ref, v_ref, g_ref, yret_ref, ga_ref, gb_ref, bd_ref,
                 rk_ref, gng_ref, gnb_ref, ln_g_ref, ln_b_ref, wr_ref, br_ref, *rest):
    wret, wrw, wo = rest[:n_w], rest[n_w:2 * n_w], rest[2 * n_w:3 * n_w]
    x1_ref, xb_ref, ids_ref, probs_ref = rest[3 * n_w:]
    x = _token_tile(tokens, pl.program_id(0), xa_ref, xb_in_ref)
    bd = bd_ref[...]
    y = _from_pair_tile(yrw_ref)
    mu = _head_sum(y, bd) * (1.0 / RWKV_N)
    d = y - mu
    var = _head_sum(d * d, bd) * (1.0 / RWKV_N)
    yn = d * lax.rsqrt(var + RWKV_GN_EPS) * gng_ref[...] + gnb_ref[...]
    bonus = _head_sum(_from_pair_tile(r_ref) * _from_pair_tile(k_ref) * rk_ref[...], bd) * _from_pair_tile(v_ref)
    yb = (yn + bonus) * _from_pair_tile(g_ref)
    merged = _sigmoid(ga_ref[...]) * _mm_w(yret_ref[...], wret) + _sigmoid(gb_ref[...]) * _mm_w(yb, wrw)
    out = _mm_w(merged, wo)
    x1 = _layer_norm(DN_ALPHA * x + out, ln_g_ref[...], ln_b_ref[...])
    x1_ref[...] = x1
    xb_ref[...] = x1.astype(BF16)

    logits = _mm(x1, wr_ref[...], 'ik,kj->ij', EXACT_PASSES) + br_ref[...]
    lane = lax.broadcasted_iota(jnp.int32, logits.shape, 1)
    work = logits
    ids, vals = [], []
    for _ in range(TOP_K):
        m = jnp.max(work, -1, keepdims=True)
        idx = jnp.min(jnp.where(work == m, lane, N_EXPERTS), -1, keepdims=True)
        ids.append(idx)
        vals.append(m)
        work = jnp.where(lane == idx, -jnp.inf, work)
    exps = [jnp.exp(m - vals[0]) for m in vals]
    inv_den = 1.0 / sum(exps)
    slot = lax.broadcasted_iota(jnp.int32, ids_ref.shape, 1)
    ids_out = jnp.zeros(ids_ref.shape, jnp.int32)
    probs_out = jnp.zeros(probs_ref.shape, F32)
    for j in range(TOP_K):
        ids_out = jnp.where(slot == j, ids[j], ids_out)
        probs_out = jnp.where(slot == j, exps[j] * inv_den, probs_out)
    ids_ref[...] = ids_out
    probs_ref[...] = probs_out


def _post(tokens, yrw, r, k, v, g, yret, gates, seq_len, p, l, passes):
    tm = TOKEN_TILE
    n = (tokens[3] + 1) * tm
    row = lambda a: a.reshape(1, -1)
    tile = lambda w, j=0: pl.BlockSpec((tm, w), lambda i: (i, j))
    full = lambda a: pl.BlockSpec(a.shape, lambda i: (0,) * a.ndim)
    pair = _pair_tile_spec(seq_len // tm, RWKV_W)
    weights = (_split_weight(p['w_ret_out'][l], passes) + _split_weight(p['w_rwkv_out'][l], passes)
               + _split_weight(p['w_o'][l], passes))
    consts = [_head_blockdiag(), row(p['rw_r_k'][l]), row(p['rw_gn_g'][l]), row(p['rw_gn_b'][l]),
              row(p['ln1_g'][l]), row(p['ln1_b'][l]), p['w_router'][l], row(p['b_router'][l]), *weights]
    in_specs = (_token_specs(tokens, tm, D_MODEL, lambda i: i) + [pair] * 5
                + [tile(RET_V_W), tile(D_MODEL, 0), tile(D_MODEL, 1)] + [full(a) for a in consts])
    return pl.pallas_call(
        functools.partial(_post_kernel, len(weights) // 3, (None, None) + tokens[2:]),
        grid=(n // tm,),
        in_specs=in_specs,
        out_specs=[tile(D_MODEL), tile(D_MODEL), tile(TOP_K), tile(TOP_K)],
        out_shape=[jax.ShapeDtypeStruct((n, D_MODEL), F32), jax.ShapeDtypeStruct((n, D_MODEL), BF16),
                   jax.ShapeDtypeStruct((n, TOP_K), jnp.int32), jax.ShapeDtypeStruct((n, TOP_K), F32)],
        compiler_params=_params("parallel"),
        name="merge_ln_router",
    )(tokens[0], tokens[1], yrw, r, k, v, g, yret, gates, gates, *consts)


SEG_ALIGN = 8
ROUTE_TILE = 512
LOCAL_ROWS = 2304
PROB_LANES = 128
ROW_W = D_MODEL + PROB_LANES


def _start_segments(i, cnt_ref, lst_ref, gst_ref, make_copy):
    def per_expert(e, carry):
        j = i * N_EXPERTS + e
        rows = pl.multiple_of(cnt_ref[j], SEG_ALIGN)

        @pl.when(rows > 0)
        def _():
            make_copy(pl.multiple_of(lst_ref[j], SEG_ALIGN), pl.multiple_of(gst_ref[j], SEG_ALIGN), rows).start()

        return carry

    lax.fori_loop(0, N_EXPERTS, per_expert, 0)


def _wait_segments(i, tot_ref, make_copy):
    make_copy(0, 0, pl.multiple_of(tot_ref[i], SEG_ALIGN)).wait()


def _dispatch_kernel(cnt_ref, lst_ref, gst_ref, tot_ref, ids_ref, probs_ref, lstart_ref, xb_ref, xs_hbm, buf, sem):
    i = pl.program_id(0)
    tm = xb_ref.shape[0]
    ids = ids_ref[0]
    probs = probs_ref[0]
    expert = lax.broadcasted_iota(jnp.int32, (N_EXPERTS, tm), 0)
    picks = [ids[k:k + 1, :] == expert for k in range(TOP_K)]
    picked = sum(pk.astype(F32) for pk in picks)
    m = lax.broadcasted_iota(jnp.int32, (tm, tm), 0)
    n = lax.broadcasted_iota(jnp.int32, (tm, tm), 1)
    earlier = (m < n).astype(BF16)
    rank = jnp.dot(picked.astype(BF16), earlier, preferred_element_type=F32)
    pos = lstart_ref[0] + rank
    row = lax.broadcasted_iota(jnp.int32, (LOCAL_ROWS, tm), 0)
    perm = jnp.zeros((LOCAL_ROWS, tm), F32)
    weight = jnp.zeros((LOCAL_ROWS, tm), F32)
    for k in range(TOP_K):
        lpos = jnp.sum(jnp.where(picks[k], pos, 0.0), 0, keepdims=True).astype(jnp.int32)
        hit = row == lpos
        perm = jnp.where(hit, 1.0, perm)
        weight = jnp.where(hit, probs[k:k + 1, :], weight)
    rows = jnp.dot(perm.astype(BF16), xb_ref[...], preferred_element_type=F32)
    row_weight = jnp.broadcast_to(jnp.sum(weight, 1, keepdims=True), (LOCAL_ROWS, PROB_LANES))

    def copies_of(slot):
        return lambda lo, go, size: pltpu.make_async_copy(buf.at[slot, pl.ds(lo, size)], xs_hbm.at[pl.ds(go, size)],
                                                          sem.at[slot])

    slot = i % 2

    @pl.when(i >= 2)
    def _():
        _wait_segments(i - 2, tot_ref, copies_of(slot))

    buf[slot, :, :D_MODEL] = rows
    buf[slot, :, D_MODEL:] = row_weight
    _start_segments(i, cnt_ref, lst_ref, gst_ref, copies_of(slot))

    @pl.when(i == pl.num_programs(0) - 1)
    def _():
        @pl.when(i >= 1)
        def _():
            _wait_segments(i - 1, tot_ref, copies_of(1 - slot))

        _wait_segments(i, tot_ref, copies_of(slot))


def _round_up(x, m):
    return (x + m - 1) // m * m


def _routing_tables(ids):
    n = ids.shape[0]
    nt = n // ROUTE_TILE
    picked = jnp.sum(ids[:, :, None] == jnp.arange(N_EXPERTS, dtype=jnp.int32)[None, None, :], 1)
    cnt = jnp.sum(picked.reshape(nt, ROUTE_TILE, N_EXPERTS), 1).astype(jnp.int32)
    cnt = _round_up(cnt, SEG_ALIGN)
    lstart = jnp.cumsum(cnt, 1) - cnt
    per_expert = jnp.sum(cnt, 0)
    region = _round_up(per_expert, EXPERT_ROW_TILE)
    gstart = (jnp.cumsum(region) - region)[None, :] + jnp.cumsum(cnt, 0) - cnt
    return cnt, lstart.astype(jnp.int32), gstart.astype(jnp.int32), jnp.sum(cnt, 1), region


def _table_args(tables):
    cnt, lstart, gstart, tile_rows, _ = tables
    return cnt.reshape(-1), lstart.reshape(-1), gstart.reshape(-1), tile_rows


def _dispatch(xb, ids, probs, tables, n_rows):
    n = xb.shape[0]
    tm = ROUTE_TILE
    nt = n // tm
    lstart = tables[1]
    to_lanes = lambda a: a.reshape(nt, tm, TOP_K).transpose(0, 2, 1)
    grid_spec = pltpu.PrefetchScalarGridSpec(
        num_scalar_prefetch=4,
        grid=(nt,),
        in_specs=[pl.BlockSpec((1, TOP_K, tm), lambda i, *_: (i, 0, 0)), pl.BlockSpec((1, TOP_K, tm), lambda i, *_: (i, 0, 0)),
                  pl.BlockSpec((1, N_EXPERTS, 1), lambda i, *_: (i, 0, 0)), pl.BlockSpec((tm, D_MODEL), lambda i, *_: (i, 0))],
        out_specs=pl.BlockSpec(memory_space=pl.ANY),
        scratch_shapes=[pltpu.VMEM((2, LOCAL_ROWS, ROW_W), F32), pltpu.SemaphoreType.DMA((2,))],
    )
    return pl.pallas_call(
        _dispatch_kernel,
        grid_spec=grid_spec,
        out_shape=jax.ShapeDtypeStruct((n_rows, ROW_W), F32),
        compiler_params=_params("arbitrary"),
        name="dispatch",
    )(*_table_args(tables), to_lanes(ids), to_lanes(probs), lstart.astype(F32)[:, :, None], xb)


def _expert_kernel(layer, tile_ref, exp_ref, valid_ref, slot_ref, next_ref, x_ref, wgu_hbm, bgu_ref, wd_hbm, bd_ref,
                   o_ref, wgu_f, wd_f, wgu_s, wd_s, sem):
    i = pl.program_id(0)

    def weight_copies(e, slot):
        return (pltpu.make_async_copy(wgu_hbm.at[layer, e], wgu_f.at[slot], sem.at[0, slot]),
                pltpu.make_async_copy(wd_hbm.at[layer, e], wd_f.at[slot], sem.at[1, slot]))

    @pl.when(valid_ref[i] == 1)
    def _():
        e = exp_ref[i]
        slot = slot_ref[i]

        @pl.when(i == 0)
        def _():
            for c in weight_copies(e, slot):
                c.start()

        @pl.when((i == 0) | (exp_ref[jnp.maximum(i - 1, 0)] != e))
        def _():
            for c in weight_copies(e, slot):
                c.wait()
            wgu_s[...] = wgu_f[slot].astype(BF16)
            wd_s[...] = wd_f[slot].astype(BF16)

            @pl.when(next_ref[i] >= 0)
            def _():
                for c in weight_copies(next_ref[i], 1 - slot):
                    c.start()

        h = jnp.dot(x_ref[:, :D_MODEL].astype(BF16), wgu_s[...], preferred_element_type=F32) + bgu_ref[0, 0]
        gate = jnp.minimum(h[:, :D_FF], SWIGLU_LIMIT)
        up = jnp.clip(h[:, D_FF:], -SWIGLU_LIMIT, SWIGLU_LIMIT)
        act = gate * _sigmoid(SWIGLU_ALPHA * gate) * (up + 1.0)
        y = jnp.dot(act.astype(BF16), wd_s[...], preferred_element_type=F32) + bd_ref[0, 0]
        o_ref[...] = y * x_ref[:, D_MODEL:D_MODEL + 1]


def _expert_schedule(region, n_rows, tm):
    n_entries = n_rows // tm
    tile_end = jnp.cumsum(region // tm)
    total = tile_end[-1]
    t = jnp.minimum(jnp.arange(n_entries), total - 1).astype(jnp.int32)
    e = jnp.sum(tile_end[None, :] <= t[:, None], -1).astype(jnp.int32)
    valid = (jnp.arange(n_entries) < total).astype(jnp.int32)
    used = region > 0
    slot = (jnp.cumsum(used) - 1) % 2
    ids = jnp.where(used, jnp.arange(N_EXPERTS), N_EXPERTS)
    first_used_from = jnp.flip(lax.cummin(jnp.flip(ids)))
    nxt = jnp.concatenate([first_used_from[1:], jnp.full((1,), N_EXPERTS)])
    nxt = jnp.where(nxt == N_EXPERTS, -1, nxt)
    return t, e, valid, slot[e].astype(jnp.int32), nxt[e].astype(jnp.int32)


def _experts(xs, sched, p, l):
    n_rows = xs.shape[0]
    tm = EXPERT_ROW_TILE
    bgu = p['b_gate_up'].reshape(DEPTH, N_EXPERTS, 1, 2 * D_FF)
    bdn = p['b_down'].reshape(DEPTH, N_EXPERTS, 1, D_MODEL)
    by_tile = lambda w: pl.BlockSpec((tm, w), lambda i, t, e, *_: (t[i], 0))
    by_expert = lambda a, b: pl.BlockSpec((1, 1, a, b), lambda i, t, e, *_: (l, e[i], 0, 0))
    in_hbm = pl.BlockSpec(memory_space=pl.ANY)
    grid_spec = pltpu.PrefetchScalarGridSpec(
        num_scalar_prefetch=5,
        grid=(sched[0].shape[0],),
        in_specs=[by_tile(ROW_W), in_hbm, by_expert(1, 2 * D_FF), in_hbm, by_expert(1, D_MODEL)],
        out_specs=by_tile(D_MODEL),
        scratch_shapes=[pltpu.VMEM((2, D_MODEL, 2 * D_FF), F32), pltpu.VMEM((2, D_FF, D_MODEL), F32),
                        pltpu.VMEM((D_MODEL, 2 * D_FF), BF16), pltpu.VMEM((D_FF, D_MODEL), BF16),
                        pltpu.SemaphoreType.DMA((2, 2))],
    )
    return pl.pallas_call(
        functools.partial(_expert_kernel, l),
        grid_spec=grid_spec,
        out_shape=jax.ShapeDtypeStruct((n_rows, D_MODEL), F32),
        compiler_params=_params("arbitrary"),
        name="experts",
    )(*sched, xs, p['w_gate_up'], bgu, p['w_down'], bdn)


def _combine_kernel(tile0, cnt_ref, lst_ref, gst_ref, tot_ref, ids_ref, lstart_ref, x_ref, ln_g_ref, ln_b_ref, ys_hbm,
                    o_ref, buf, sem):
    step = pl.program_id(0)
    i = step + tile0
    tm = x_ref.shape[0]
    slot = step % 2

    def copies_of(slot):
        return lambda lo, go, size: pltpu.make_async_copy(ys_hbm.at[pl.ds(go, size)], buf.at[slot, pl.ds(lo, size)],
                                                          sem.at[slot])

    def fetch(tile, slot):
        buf[slot, ROUTE_TILE * TOP_K:, :] = jnp.zeros((LOCAL_ROWS - ROUTE_TILE * TOP_K, D_MODEL), F32)
        _start_segments(tile, cnt_ref, lst_ref, gst_ref, copies_of(slot))

    @pl.when(step == 0)
    def _():
        fetch(i, slot)

    @pl.when(step + 1 < pl.num_programs(0))
    def _():
        fetch(i + 1, 1 - slot)

    ids = ids_ref[...]
    expert = lax.broadcasted_iota(jnp.int32, (tm, N_EXPERTS), 1)
    picks = [ids[:, k:k + 1] == expert for k in range(TOP_K)]
    picked = sum(pk.astype(F32) for pk in picks)
    m = lax.broadcasted_iota(jnp.int32, (tm, tm), 0)
    n = lax.broadcasted_iota(jnp.int32, (tm, tm), 1)
    earlier = (n < m).astype(BF16)
    rank = jnp.dot(earlier, picked.astype(BF16), preferred_element_type=F32)
    pos = lstart_ref[0] + rank
    col = lax.broadcasted_iota(jnp.int32, (tm, LOCAL_ROWS), 1)
    perm = jnp.zeros((tm, LOCAL_ROWS), F32)
    for k in range(TOP_K):
        lpos = jnp.sum(jnp.where(picks[k], pos, 0.0), 1, keepdims=True).astype(jnp.int32)
        perm = jnp.where(col == lpos, 1.0, perm)

    _wait_segments(i, tot_ref, copies_of(slot))
    moe = jnp.dot(perm.astype(BF16), buf[slot].astype(BF16), preferred_element_type=F32)
    o_ref[...] = _layer_norm(DN_ALPHA * x_ref[...] + moe, ln_g_ref[...], ln_b_ref[...])


def _combine(x1, ys, ids, tables, ln_g, ln_b, tile0, n_tiles):
    tm = ROUTE_TILE
    lstart = tables[1]
    at = lambda i, *_: (i + tile0, 0)
    grid_spec = pltpu.PrefetchScalarGridSpec(
        num_scalar_prefetch=4,
        grid=(n_tiles,),
        in_specs=[pl.BlockSpec((tm, TOP_K), at), pl.BlockSpec((1, 1, N_EXPERTS), lambda i, *_: (i + tile0, 0, 0)),
                  pl.BlockSpec((tm, D_MODEL), at), pl.BlockSpec((1, D_MODEL), lambda i, *_: (0, 0)),
                  pl.BlockSpec((1, D_MODEL), lambda i, *_: (0, 0)), pl.BlockSpec(memory_space=pl.ANY)],
        out_specs=pl.BlockSpec((tm, D_MODEL), lambda i, *_: (i, 0)),
        scratch_shapes=[pltpu.VMEM((2, LOCAL_ROWS, D_MODEL), F32), pltpu.SemaphoreType.DMA((2,))],
    )
    return pl.pallas_call(
        functools.partial(_combine_kernel, tile0),
        grid_spec=grid_spec,
        out_shape=jax.ShapeDtypeStruct((n_tiles * tm, D_MODEL), F32),
        compiler_params=_params("arbitrary"),
        name="combine_ln",
    )(*_table_args(tables), ids, lstart.astype(F32)[:, None, :], x1, ln_g.reshape(1, -1), ln_b.reshape(1, -1), ys)


def _moe(x1, xb, ids, probs, p, l, parts):
    n = x1.shape[0]
    tm = EXPERT_ROW_TILE
    seg_rows = n * TOP_K + (n // ROUTE_TILE) * N_EXPERTS * (SEG_ALIGN - 1)
    n_rows = _round_up(seg_rows + N_EXPERTS * (tm - SEG_ALIGN), tm)
    tables = _routing_tables(ids)
    xs = _dispatch(xb, ids, probs, tables, n_rows)
    ys = _experts(xs, _expert_schedule(tables[4], n_rows, tm), p, l)
    return [_combine(x1, ys, ids, tables, p['ln2_g'][l], p['ln2_b'][l], *part) for part in parts]


def _pad_time(a, tp):
    return jnp.pad(a, ((0, 0), (0, tp - a.shape[1]), (0, 0)))


def kernel(x_prompt, x_sample, state_ret, state_rwkv, state_shift, w_in, ret_gn_g, ret_gn_b, w_ret_out, rw_mu, rw_w0, rw_w_up, rw_a0, rw_a_up, rw_g_up, rw_k_k, rw_k_a, rw_r_k, rw_gn_g, rw_gn_b, rw_v0, rw_vres_down, rw_vres_up, w_rwkv_out, w_o, ln1_g, ln1_b, w_router, b_router, w_gate_up, b_gate_up, w_down, b_down, ln2_g, ln2_b):
    p = dict(w_ret_out=w_ret_out, rw_mu=rw_mu, rw_w0=rw_w0, rw_w_up=rw_w_up, rw_a0=rw_a0, rw_a_up=rw_a_up,
             rw_g_up=rw_g_up, rw_k_k=rw_k_k, rw_k_a=rw_k_a, rw_r_k=rw_r_k, rw_gn_g=rw_gn_g, rw_gn_b=rw_gn_b,
             w_rwkv_out=w_rwkv_out, w_o=w_o, ln1_g=ln1_g, ln1_b=ln1_b, w_router=w_router, b_router=b_router,
             w_gate_up=w_gate_up, b_gate_up=b_gate_up, w_down=w_down, b_down=b_down, ln2_g=ln2_g, ln2_b=ln2_b)
    bp, tp, _ = x_prompt.shape
    bs, ts, _ = x_sample.shape
    np_, ns = bp * tp, bs * ts
    pos_p = jnp.arange(tp, dtype=F32)
    ts_ret = 8
    pos_s = PAST_LEN + jnp.arange(ts_ret, dtype=F32)

    assert ns == TOKEN_TILE and np_ % TOKEN_TILE == 0
    n_prompt_tiles = np_ // TOKEN_TILE
    tokens = (x_prompt.reshape(np_, D_MODEL), x_sample.transpose(1, 0, 2).reshape(ns, D_MODEL), 0, n_prompt_tiles)
    outs = {k: [] for k in ('ret_p', 'rw_p', 'sh_p', 'sh_s')}
    sret_s = srw_s = None
    v_first = None
    u_off = RET_W
    g_off = RET_W + SHIFT_W
    for l in range(DEPTH):
        passes = EXACT_PASSES if l == 0 else 1
        z_ret = _matmul(tokens, w_in[l][:, :u_off], TOKEN_TILE, RET_W if passes == 1 else RET_W // 2, passes)
        u = _matmul(tokens, w_in[l][:, u_off:g_off], TOKEN_TILE, SHIFT_W, passes)
        gates = _matmul(tokens, w_in[l][:, g_off:], TOKEN_TILE, D_MODEL, passes)

        zr_s = _pad_time(z_ret[np_:].reshape(ts, bs, RET_W).transpose(1, 0, 2), ts_ret)
        yret, sret_p = _retention(z_ret.reshape(-1, RET_CHUNK, RET_W), pos_p, tp, None, ret_gn_g[l], ret_gn_b[l], 1,
                                  passes, n_seq=bp)
        yret_s, sret_s = _retention(zr_s, pos_s, ts, (state_ret, l, sret_s), ret_gn_g[l], ret_gn_b[l], 8, passes)
        yret = yret.reshape(-1, RET_V_W).at[np_:].set(yret_s[:, :ts].transpose(1, 0, 2).reshape(ns, RET_V_W))

        vres = None if l == 0 else (rw_v0[l - 1], rw_vres_down[l - 1], rw_vres_up[l - 1])
        r, lw, k, v, a, b, g = _rwkv_pre(u, state_shift[l], np_, tp, p, l, v_first, vres)
        if l == 0:
            v_first = v
        yrw, srw_p = _rwkv_chunks(r, lw, k, v, a, b, passes)
        yrw, srw_s = _rwkv_steps(r, lw, k, v, a, b, (state_rwkv, l, srw_s), yrw)

        x1, xb, ids, probs = _post(tokens, yrw, r, k, v, g, yret, gates, tp, p, l, passes)
        if l + 1 < DEPTH:
            x, = _moe(x1, xb, ids, probs, p, l, [(0, n_prompt_tiles + 1)])
            tokens = (x, x, n_prompt_tiles, n_prompt_tiles)
        else:
            y_p, y_s = _moe(x1, xb, ids, probs, p, l, [(0, n_prompt_tiles), (n_prompt_tiles, 1)])

        outs['ret_p'].append(sret_p)
        outs['rw_p'].append(srw_p)
        outs['sh_p'].append(u[tp - 1:np_:tp])
        outs['sh_s'].append(u[np_ + ns - bs:])

    y_prompt = y_p.reshape(bp, tp, D_MODEL)
    y_sample = y_s.reshape(ts, bs, D_MODEL).transpose(1, 0, 2)
    st = {k: jnp.stack(v) for k, v in outs.items()}
    return (y_prompt, y_sample, st['ret_p'], st['rw_p'], st['sh_p'], sret_s, srw_s, st['sh_s'])
```

```python
import functools

import jax
import jax.numpy as jnp
from jax import lax
from jax.experimental import pallas as pl
from jax.experimental.pallas import tpu as pltpu

F32 = jnp.float32
BF16 = jnp.bfloat16
HI = lax.Precision.HIGHEST

D_MODEL = 1024
DEPTH = 2
PAST_LEN = 16384
RET_HEADS = 4
RET_DK = 128
RET_DV = 256
RET_QK_W = RET_HEADS * RET_DK
RET_V_W = RET_HEADS * RET_DV
RET_W = 2 * RET_QK_W + 2 * RET_V_W
RET_CHUNK = 128
ROPE_BASE = 10000.0
RWKV_HEADS = 8
RWKV_N = 64
RWKV_W = RWKV_HEADS * RWKV_N
LORA_W = 64
LORA_A = 64
LORA_G = 128
SHIFT_W = 3 * RWKV_W + LORA_W + LORA_A + LORA_G
RWKV_CHUNK = 64
N_EXPERTS = 32
TOP_K = 4
D_FF = D_MODEL
SWIGLU_LIMIT = 7.0
SWIGLU_ALPHA = 1.702
DN_ALPHA = (2 * DEPTH) ** 0.25
LN_EPS = 1e-5
RET_GN_EPS = 1e-5
RWKV_GN_EPS = 64e-5

VMEM_LIMIT = 56 * 1024 * 1024
TOKEN_TILE = 512
EXPERT_ROW_TILE = 512
RWKV_STEP_BATCH_BLOCK = 8
EXACT_PASSES = 3
RWKV_REFINE_STEPS = 1


def _params(*sem):
    return pltpu.CompilerParams(dimension_semantics=sem, vmem_limit_bytes=VMEM_LIMIT)


def _split(x):
    hi = x.astype(BF16)
    lo = (x - hi.astype(F32)).astype(BF16)
    return hi, lo


def _split_kernel(w_ref, hi_ref, lo_ref):
    hi_ref[...], lo_ref[...] = _split(w_ref[...])


def _split_weight(w, passes):
    if passes == 1:
        return (w.astype(BF16),)
    rows = 256
    spec = pl.BlockSpec((rows, w.shape[1]), lambda i: (i, 0))
    return tuple(pl.pallas_call(
        _split_kernel,
        grid=(w.shape[0] // rows,),
        in_specs=[spec],
        out_specs=[spec, spec],
        out_shape=[jax.ShapeDtypeStruct(w.shape, BF16)] * 2,
        compiler_params=_params("parallel"),
        name="split_weight",
    )(w))


def _mm(a, b, spec, passes):
    dg = lambda x, y: jnp.einsum(spec, x, y, preferred_element_type=F32)
    if passes == 1:
        return dg(a.astype(BF16), b.astype(BF16))
    ah, al = _split(a)
    bh, bl = _split(b)
    return dg(ah, bh) + (dg(ah, bl) + dg(al, bh))


def _mm_w(a, w_refs):
    dg = lambda x, y: jnp.dot(x, y, preferred_element_type=F32)
    if len(w_refs) == 1:
        return dg(a.astype(BF16), w_refs[0][...])
    ah, al = _split(a)
    return dg(ah, w_refs[0][...]) + (dg(ah, w_refs[1][...]) + dg(al, w_refs[0][...]))


def _dot_hi(a, b):
    return jnp.dot(a, b, precision=HI, preferred_element_type=F32)


def _sigmoid(x):
    return 1.0 / (1.0 + jnp.exp(-x))


def _layer_norm(x, g, b):
    mu = jnp.mean(x, -1, keepdims=True)
    d = x - mu
    var = jnp.mean(d * d, -1, keepdims=True)
    return d * lax.rsqrt(var + LN_EPS) * g + b


def _token_specs(tokens, tm, w, tile_of):
    _, _, b_block, n_a = tokens
    return [pl.BlockSpec((tm, w), lambda *g: (jnp.minimum(tile_of(*g), n_a - 1), 0)),
            pl.BlockSpec((tm, w), lambda *g: (b_block, 0))]


def _token_tile(tokens, i, a_ref, b_ref):
    return jnp.where(i < tokens[3], a_ref[...], b_ref[...])


def _matmul_kernel(tokens, xa_ref, xb_ref, *refs):
    o_ref = refs[-1]
    o_ref[...] = _mm_w(_token_tile(tokens, pl.program_id(1), xa_ref, xb_ref), refs[:-1])


def _matmul(tokens, w, tm, tn, passes):
    k, n = w.shape
    n_tiles = tokens[3] + 1
    ws = _split_weight(w, passes)
    return pl.pallas_call(
        functools.partial(_matmul_kernel, (None, None) + tokens[2:]),
        grid=(n // tn, n_tiles),
        in_specs=_token_specs(tokens, tm, k, lambda j, i: i) + [pl.BlockSpec((k, tn), lambda j, i: (0, j))] * len(ws),
        out_specs=pl.BlockSpec((tm, tn), lambda j, i: (i, j)),
        out_shape=jax.ShapeDtypeStruct((n_tiles * tm, n), F32),
        compiler_params=_params("parallel", "parallel"),
        name="in_proj",
    )(tokens[0], tokens[1], *ws)


def _ret_kernel(has_state, passes, q_ref, k_ref, v_ref, g_ref, cos_ref, sin_ref, dm_ref, qd_ref, kd_ref, cd_ref,
                gng_ref, gnb_ref, *rest):
    if has_state:
        s0_ref, y_ref, so_ref, s_scr = rest[0], *rest[-3:]
    else:
        y_ref, so_ref, s_scr = rest
    c = pl.program_id(1)
    bb, cl, _ = q_ref.shape
    nh = RET_HEADS

    def heads(x, w):
        return jnp.concatenate([x[:, :, h * w:(h + 1) * w] for h in range(nh)], 0)

    def per_head(ref):
        return jnp.concatenate([jnp.broadcast_to(ref[h], (bb,) + ref.shape[1:]) for h in range(nh)], 0)

    @pl.when(c == 0)
    def _():
        if has_state:
            s_scr[...] = jnp.concatenate([s0_ref[:, h] for h in range(nh)], 0)
        else:
            s_scr[...] = jnp.zeros_like(s_scr)

    cos = cos_ref[...]
    sin = sin_ref[...]
    mm = functools.partial(_mm, passes=passes)

    def rope(x):
        x2 = x.reshape(nh * bb * cl, RET_DK)
        rot = pltpu.roll(x2, RET_DK // 2, axis=1).reshape(nh * bb, cl, RET_DK)
        return x * cos + rot * sin

    q = rope(heads(q_ref[...], RET_DK))
    k = rope(heads(k_ref[...], RET_DK)) * (RET_DK ** -0.5)
    v = heads(v_ref[...], RET_DV)
    s = s_scr[...]
    sc = mm(q, k, 'bid,bjd->bij') * per_head(dm_ref)
    intra = mm(sc, v, 'bij,bje->bie')
    cross = mm(q, s, 'bid,bde->bie') * per_head(qd_ref)
    s_new = s * per_head(cd_ref) + mm(k * per_head(kd_ref), v, 'bjd,bje->bde')
    s_scr[...] = s_new

    y = intra + cross
    mu = jnp.mean(y, -1, keepdims=True)
    d = y - mu
    var = jnp.mean(d * d, -1, keepdims=True)
    yn = d * lax.rsqrt(var + RET_GN_EPS)
    for h in range(nh):
        cols = slice(h * RET_DV, (h + 1) * RET_DV)
        rg = g_ref[:, :, cols]
        y_ref[:, :, cols] = (yn[h * bb:(h + 1) * bb] * gng_ref[:, cols] + gnb_ref[:, cols]) * (rg * _sigmoid(rg))

    @pl.when(c == pl.num_programs(1) - 1)
    def _():
        for h in range(nh):
            so_ref[:, h] = s_new[h * bb:(h + 1) * bb]


def _retention(z, pos, t_real, s0, gn_g, gn_b, bb, passes, n_seq=None):
    flat = n_seq is not None
    tp = pos.shape[0]
    b = n_seq if flat else z.shape[0]
    cl = RET_CHUNK if t_real % RET_CHUNK == 0 else tp
    cr = min(cl, t_real)
    nc = tp // cl
    at = (lambda bi, c: (bi * nc + c, 0)) if flat else (lambda bi, c: (bi, c))
    half = RET_DK // 2
    inv = ROPE_BASE ** (-jnp.arange(half, dtype=F32) / half)
    ang = pos[:, None] * inv[None, :]
    cos = jnp.concatenate([jnp.cos(ang), jnp.cos(ang)], -1)
    sin = jnp.concatenate([-jnp.sin(ang), jnp.sin(ang)], -1)
    lg = jnp.log1p(-jnp.exp2(-5.0 - jnp.arange(RET_HEADS, dtype=F32)))
    i = jnp.arange(cl, dtype=F32)
    real = i < cr
    diff = i[:, None] - i[None, :]
    ok = (diff >= 0) & real[:, None] & real[None, :]
    dmask = jnp.exp(jnp.where(ok[None], diff[None] * lg[:, None, None], -jnp.inf))
    q_dec = jnp.exp((i[None, :] + 1.0) * lg[:, None])[..., None]
    k_dec = jnp.where(real[None, :], jnp.exp((cr - 1.0 - i)[None, :] * lg[:, None]), 0.0)[..., None]
    c_dec = jnp.exp(cr * lg)[:, None, None]

    has_state = s0 is not None
    cols = lambda w, j: pl.BlockSpec((bb, cl, w), lambda bi, c: at(bi, c) + (j,))
    full = lambda a: pl.BlockSpec(a.shape, lambda bi, c: (0,) * a.ndim)
    consts = [dmask, q_dec, k_dec, c_dec, gn_g.reshape(1, -1), gn_b.reshape(1, -1)]
    in_specs = [cols(RET_QK_W, 0), cols(RET_QK_W, 1), cols(RET_V_W, 1), cols(RET_V_W, 2),
                pl.BlockSpec((cl, RET_DK), lambda bi, c: (c, 0)), pl.BlockSpec((cl, RET_DK), lambda bi, c: (c, 0))]
    in_specs += [full(a) for a in consts]
    args = [z, z, z, z, cos, sin] + consts
    aliases = {}
    if has_state:
        all_states, layer, earlier = s0
        s_spec = pl.BlockSpec((None, bb, RET_HEADS, RET_DK, RET_DV), lambda bi, c: (layer, bi, 0, 0, 0))
        s_shape = all_states.shape
        in_specs.append(s_spec)
        args.append(all_states)
        if earlier is not None:
            aliases = {len(args): 1}
            in_specs.append(pl.BlockSpec(memory_space=pl.ANY))
            args.append(earlier)
    else:
        s_spec = pl.BlockSpec((bb, RET_HEADS, RET_DK, RET_DV), lambda bi, c: (bi, 0, 0, 0))
        s_shape = (b, RET_HEADS, RET_DK, RET_DV)
    return pl.pallas_call(
        functools.partial(_ret_kernel, has_state, passes),
        grid=(b // bb, nc),
        in_specs=in_specs,
        out_specs=[cols(RET_V_W, 0), s_spec],
        out_shape=[jax.ShapeDtypeStruct(z.shape[:2] + (RET_V_W,), F32), jax.ShapeDtypeStruct(s_shape, F32)],
        scratch_shapes=[pltpu.VMEM((RET_HEADS * bb, RET_DK, RET_DV), F32)],
        input_output_aliases=aliases,
        compiler_params=_params("parallel", "arbitrary"),
        name="retention",
    )(*args)


def _head_sum(x, bd):
    hi, lo = _split(x)
    return jnp.dot(hi, bd, preferred_element_type=F32) + jnp.dot(lo, bd, preferred_element_type=F32)


def _pair_shape(n_pairs, chunks, w):
    return (n_pairs + 1, chunks, 2, RWKV_CHUNK, w)


def _pair_tile_spec(tiles_per_seq, w):
    rows = TOKEN_TILE // RWKV_CHUNK
    return pl.BlockSpec((1, rows, 1, RWKV_CHUNK, w),
                        lambda i: (i // (2 * tiles_per_seq), i % tiles_per_seq, (i // tiles_per_seq) % 2, 0, 0))


def _to_pair_tile(ref, x):
    ref[0, :, 0] = x.reshape(TOKEN_TILE // RWKV_CHUNK, RWKV_CHUNK, x.shape[-1])


def _from_pair_tile(ref):
    return ref[0, :, 0].reshape(TOKEN_TILE, ref.shape[-1])


def _rwkv_pre_kernel(has_vres, n_prompt_tiles, tiles_per_seq, u_ref, tail_ref, shift_ref, mu_ref, w0_ref, wup_ref,
                     a0_ref, aup_ref, gup_ref, kk_ref, ka_ref, bd_ref, *rest):
    if has_vres:
        vf_ref, v0_ref, vd_ref, vu_ref, r_o, lw_o, k_o, v_o, a_o, b_o, g_o = rest
    else:
        r_o, lw_o, k_o, v_o, a_o, b_o, g_o = rest
    i = pl.program_id(0)
    u = u_ref[...]
    tm = u.shape[0]
    row = lax.broadcasted_iota(jnp.int32, (tm, 1), 0)
    before = jnp.where(i % tiles_per_seq == 0, 0.0, tail_ref[tail_ref.shape[0] - 1:, :])
    prev_prompt = jnp.where(row == 0, before, pltpu.roll(u, 1, axis=0))
    n_seq = shift_ref.shape[0]
    prev_sample = jnp.concatenate([shift_ref[...], u[:tm - n_seq]], 0)
    prev = jnp.where(i < n_prompt_tiles, prev_prompt, prev_sample)
    um = u + (prev - u) * mu_ref[...]
    w1, w2, w3 = RWKV_W, 2 * RWKV_W, 3 * RWKV_W
    r = um[:, :w1]
    kw = um[:, w1:w2]
    vw = um[:, w2:w3]
    wd = um[:, w3:w3 + LORA_W]
    ad = um[:, w3 + LORA_W:w3 + LORA_W + LORA_A]
    gd = um[:, w3 + LORA_W + LORA_A:]
    lora = lambda x, w_ref: _mm(x, w_ref[...], 'ik,kj->ij', EXACT_PASSES)
    xw = w0_ref[...] + lora(jnp.tanh(wd), wup_ref)
    softplus = jnp.maximum(-xw, 0.0) + jnp.log1p(jnp.exp(-jnp.abs(xw)))
    _to_pair_tile(lw_o, -jnp.exp(-softplus - 0.5))
    a = _sigmoid(a0_ref[...] + lora(ad, aup_ref))
    _to_pair_tile(g_o, lora(_sigmoid(gd), gup_ref))
    if has_vres:
        gate = _sigmoid(v0_ref[...] + lora(lora(vw, vd_ref), vu_ref))
        vw = vw + (_from_pair_tile(vf_ref) - vw) * gate
    kk = kw * kk_ref[...]
    norm = jnp.sqrt(_head_sum(kk * kk, bd_ref[...]))
    kk = kk / jnp.maximum(norm, 1e-12)
    _to_pair_tile(r_o, r)
    _to_pair_tile(k_o, kw * (1.0 + (a - 1.0) * ka_ref[...]))
    _to_pair_tile(v_o, vw)
    _to_pair_tile(a_o, -kk)
    _to_pair_tile(b_o, kk * a)


def _head_blockdiag():
    h = jnp.arange(RWKV_W) // RWKV_N
    return (h[:, None] == h[None, :]).astype(BF16)


def _rwkv_pre(u, shift_state, n_prompt, seq_len, p, l, v_first, vres):
    n = u.shape[0]
    tm = TOKEN_TILE
    assert n == n_prompt + tm and seq_len % tm == 0
    tiles_per_seq = seq_len // tm
    tail_rows = 8
    row = lambda a: a.reshape(1, -1)
    full = lambda a: pl.BlockSpec(a.shape, lambda i: (0,) * a.ndim)
    pair = _pair_tile_spec(tiles_per_seq, RWKV_W)
    has_vres = vres is not None
    args = [u, u, shift_state, row(p['rw_mu'][l]), row(p['rw_w0'][l]), p['rw_w_up'][l], row(p['rw_a0'][l]),
            p['rw_a_up'][l], p['rw_g_up'][l], row(p['rw_k_k'][l]), row(p['rw_k_a'][l]), _head_blockdiag()]
    in_specs = [pl.BlockSpec((tm, SHIFT_W), lambda i: (i, 0)),
                pl.BlockSpec((tail_rows, SHIFT_W), lambda i: (jnp.maximum(i * (tm // tail_rows) - 1, 0), 0))]
    in_specs += [full(a) for a in args[2:]]
    if has_vres:
        extra = [v_first, row(vres[0]), vres[1], vres[2]]
        in_specs += [pair] + [full(a) for a in extra[1:]]
        args += extra
    shape = _pair_shape(n_prompt // seq_len // 2, seq_len // RWKV_CHUNK, RWKV_W)
    return pl.pallas_call(
        functools.partial(_rwkv_pre_kernel, has_vres, n_prompt // tm, tiles_per_seq),
        grid=(n // tm,),
        in_specs=in_specs,
        out_specs=[pair] * 7,
        out_shape=[jax.ShapeDtypeStruct(shape, F32)] * 7,
        compiler_params=_params("parallel"),
        name="rwkv_pre",
    )(*args)


def _rwkv_chunk_kernel(passes, r_ref, lw_ref, k_ref, v_ref, a_ref, b_ref, tri_ref, y_ref, so_ref, s_scr):
    c = pl.program_id(1)

    @pl.when(c == 0)
    def _():
        s_scr[...] = jnp.zeros_like(s_scr)

    bb, cl = r_ref.shape[2], r_ref.shape[3]
    r_ref, lw_ref, k_ref, v_ref, a_ref, b_ref, y_ref = (ref.at[0, 0] for ref in
                                                        (r_ref, lw_ref, k_ref, v_ref, a_ref, b_ref, y_ref))
    mm = functools.partial(_mm, passes=passes)
    ti = lax.broadcasted_iota(jnp.int32, (cl, cl), 0)
    si = lax.broadcasted_iota(jnp.int32, (cl, cl), 1)
    strict = (ti > si).astype(F32)
    incl = (ti >= si).astype(F32)
    eye = (ti == si).astype(F32)

    def heads(x):
        return jnp.stack([x[bi][:, h * RWKV_N:(h + 1) * RWKV_N] for bi in range(bb) for h in range(RWKV_HEADS)])

    lw = lw_ref[...]
    cum = jnp.stack([_dot_hi(tri_ref[...], lw[bi]) for bi in range(bb)])
    last = cum[:, cl - 1:cl, :]
    e_neg = jnp.exp(-cum)
    e_end = jnp.exp(last - cum)
    at = heads(a_ref[...] * jnp.exp(cum - lw))
    rt = heads(r_ref[...] * jnp.exp(cum))
    bt = heads(b_ref[...] * e_neg)
    kt = heads(k_ref[...] * e_neg)
    bw = heads(b_ref[...] * e_end)
    kw = heads(k_ref[...] * e_end)
    wc = heads(jnp.exp(last))
    vh = heads(v_ref[...])

    lhs = jnp.concatenate([at, rt], 1)
    gram = mm(lhs, jnp.concatenate([bt, kt], 1), 'gik,gjk->gij')
    a_ab = gram[:, :cl, :cl] * strict
    a_ak = gram[:, :cl, cl:] * strict
    a_rb = gram[:, cl:, :cl] * incl
    a_rk = gram[:, cl:, cl:] * incl
    refine = RWKV_REFINE_STEPS if passes > 1 else 0
    mm_inv = functools.partial(_mm, passes=1) if refine else mm
    inv = eye + a_ab
    pw = a_ab
    for _ in range(cl.bit_length() - 2):
        pw = mm_inv(pw, pw, 'gij,gjk->gik')
        inv = inv + mm_inv(inv, pw, 'gij,gjk->gik')
    s0 = s_scr[...].reshape(bb * RWKV_HEADS, RWKV_N, RWKV_N)
    xs = mm(lhs, s0, 'gtj,gij->gti')
    av = mm(jnp.concatenate([a_ak, a_rk], 1), vh, 'gts,gsi->gti')
    rhs = xs[:, :cl] + av[:, :cl]
    u = mm_inv(inv, rhs, 'gts,gsi->gti')
    for _ in range(refine):
        u = u + mm_inv(inv, rhs - u + mm(a_ab, u, 'gts,gsi->gti'), 'gts,gsi->gti')
    y = xs[:, cl:] + av[:, cl:] + mm(a_rb, u, 'gts,gsi->gti')
    s_new = s0 * wc + mm(jnp.concatenate([u, vh], 1), jnp.concatenate([bw, kw], 1), 'gti,gtj->gij')
    s_scr[...] = s_new.reshape(bb, RWKV_HEADS, RWKV_N, RWKV_N)
    for bi in range(bb):
        for h in range(RWKV_HEADS):
            y_ref[bi, :, h * RWKV_N:(h + 1) * RWKV_N] = y[bi * RWKV_HEADS + h]

    @pl.when(c == pl.num_programs(1) - 1)
    def _():
        so_ref[...] = s_scr[...]


def _rwkv_chunks(r, lw, k, v, a, b, passes):
    n_pairs, chunks, bb, cl, _ = r.shape
    n_pairs -= 1
    seq = pl.BlockSpec((1, 1, bb, cl, RWKV_W), lambda pi, c: (pi, c, 0, 0, 0))
    s_spec = pl.BlockSpec((bb, RWKV_HEADS, RWKV_N, RWKV_N), lambda pi, c: (pi, 0, 0, 0))
    tri = (jnp.arange(cl)[:, None] >= jnp.arange(cl)[None, :]).astype(F32)
    return pl.pallas_call(
        functools.partial(_rwkv_chunk_kernel, passes),
        grid=(n_pairs, chunks),
        in_specs=[seq] * 6 + [pl.BlockSpec((cl, cl), lambda pi, c: (0, 0))],
        out_specs=[seq, s_spec],
        out_shape=[jax.ShapeDtypeStruct(r.shape, F32),
                   jax.ShapeDtypeStruct((n_pairs * bb, RWKV_HEADS, RWKV_N, RWKV_N), F32)],
        scratch_shapes=[pltpu.VMEM((bb, RWKV_HEADS, RWKV_N, RWKV_N), F32)],
        compiler_params=_params("parallel", "arbitrary"),
        name="rwkv_chunks",
    )(r, lw, k, v, a, b, tri)


def _rwkv_step_kernel(r_ref, lw_ref, k_ref, a_ref, b_ref, vt_ref, s0_ref, *rest):
    yt_ref, so_ref = rest[-2:]
    s = s0_ref[...]
    for t in range(r_ref.shape[2]):
        row = lambda ref: ref[:, :, t:t + 1, :]
        sa = jnp.sum(s * row(a_ref), -1, keepdims=True)
        s = s * jnp.exp(row(lw_ref)) + sa * row(b_ref) + vt_ref[:, :, :, t:t + 1] * row(k_ref)
        yt_ref[:, :, :, t:t + 1] = jnp.sum(s * row(r_ref), -1, keepdims=True)
    so_ref[...] = s


def _rwkv_steps(r, lw, k, v, a, b, s0, y_pairs):
    all_states, layer, earlier = s0
    bsz = all_states.shape[1]
    t = TOKEN_TILE // bsz
    bb = RWKV_STEP_BATCH_BLOCK
    tile_rows = TOKEN_TILE // RWKV_CHUNK
    steps = lambda x: x[-1, :tile_rows, 0].reshape(t, bsz, RWKV_HEADS, RWKV_N)
    rows = lambda x: steps(x).transpose(1, 2, 0, 3)
    vt = steps(v).transpose(1, 2, 3, 0)
    row_spec = pl.BlockSpec((bb, RWKV_HEADS, t, RWKV_N), lambda i: (i, 0, 0, 0))
    col_spec = pl.BlockSpec((bb, RWKV_HEADS, RWKV_N, t), lambda i: (i, 0, 0, 0))
    s_spec = pl.BlockSpec((None, bb, RWKV_HEADS, RWKV_N, RWKV_N), lambda i: (layer, i, 0, 0, 0))
    args = [rows(r), rows(lw), rows(k), rows(a), rows(b), vt, all_states]
    in_specs = [row_spec] * 5 + [col_spec, s_spec]
    aliases = {}
    if earlier is not None:
        aliases = {len(args): 1}
        in_specs.append(pl.BlockSpec(memory_space=pl.ANY))
        args.append(earlier)
    yt, s_new = pl.pallas_call(
        _rwkv_step_kernel,
        grid=(bsz // bb,),
        in_specs=in_specs,
        out_specs=[col_spec, s_spec],
        out_shape=[jax.ShapeDtypeStruct((bsz, RWKV_HEADS, RWKV_N, t), F32),
                   jax.ShapeDtypeStruct(all_states.shape, F32)],
        input_output_aliases=aliases,
        compiler_params=_params("parallel"),
        name="rwkv_steps",
    )(*args)
    y_tile = yt.transpose(3, 0, 1, 2).reshape(tile_rows, RWKV_CHUNK, RWKV_W)
    return y_pairs.at[-1, :tile_rows, 0].set(y_tile), s_new


def _post_kernel(n_w, tokens, xa_ref, xb_in_ref, yrw_ref, r_ref, k_ref, v_ref, g_ref, yret_ref, ga_ref, gb_ref, bd_ref,
                 rk_ref, gng_ref, gnb_ref, ln_g_ref, ln_b_ref, wr_ref, br_ref, *rest):
    wret, wrw, wo = rest[:n_w], rest[n_w:2 * n_w], rest[2 * n_w:3 * n_w]
    x1_ref, xb_ref, ids_ref, probs_ref = rest[3 * n_w:]
    x = _token_tile(tokens, pl.program_id(0), xa_ref, xb_in_ref)
    bd = bd_ref[...]
    y = _from_pair_tile(yrw_ref)
    mu = _head_sum(y, bd) * (1.0 / RWKV_N)
    d = y - mu
    var = _head_sum(d * d, bd) * (1.0 / RWKV_N)
    yn = d * lax.rsqrt(var + RWKV_GN_EPS) * gng_ref[...] + gnb_ref[...]
    bonus = _head_sum(_from_pair_tile(r_ref) * _from_pair_tile(k_ref) * rk_ref[...], bd) * _from_pair_tile(v_ref)
    yb = (yn + bonus) * _from_pair_tile(g_ref)
    merged = _sigmoid(ga_ref[...]) * _mm_w(yret_ref[...], wret) + _sigmoid(gb_ref[...]) * _mm_w(yb, wrw)
    out = _mm_w(merged, wo)
    x1 = _layer_norm(DN_ALPHA * x + out, ln_g_ref[...], ln_b_ref[...])
    x1_ref[...] = x1
    xb_ref[...] = x1.astype(BF16)

    logits = _mm(x1, wr_ref[...], 'ik,kj->ij', EXACT_PASSES) + br_ref[...]
    lane = lax.broadcasted_iota(jnp.int32, logits.shape, 1)
    work = logits
    ids, vals = [], []
    for _ in range(TOP_K):
        m = jnp.max(work, -1, keepdims=True)
        idx = jnp.min(jnp.where(work == m, lane, N_EXPERTS), -1, keepdims=True)
        ids.append(idx)
        vals.append(m)
        work = jnp.where(lane == idx, -jnp.inf, work)
    exps = [jnp.exp(m - vals[0]) for m in vals]
    inv_den = 1.0 / sum(exps)
    slot = lax.broadcasted_iota(jnp.int32, ids_ref.shape, 1)
    ids_out = jnp.zeros(ids_ref.shape, jnp.int32)
    probs_out = jnp.zeros(probs_ref.shape, F32)
    for j in range(TOP_K):
        ids_out = jnp.where(slot == j, ids[j], ids_out)
        probs_out = jnp.where(slot == j, exps[j] * inv_den, probs_out)
    ids_ref[...] = ids_out
    probs_ref[...] = probs_out


def _post(tokens, yrw, r, k, v, g, yret, gates, seq_len, p, l, passes):
    tm = TOKEN_TILE
    n = (tokens[3] + 1) * tm
    row = lambda a: a.reshape(1, -1)
    tile = lambda w, j=0: pl.BlockSpec((tm, w), lambda i: (i, j))
    full = lambda a: pl.BlockSpec(a.shape, lambda i: (0,) * a.ndim)
    pair = _pair_tile_spec(seq_len // tm, RWKV_W)
    weights = (_split_weight(p['w_ret_out'][l], passes) + _split_weight(p['w_rwkv_out'][l], passes)
               + _split_weight(p['w_o'][l], passes))
    consts = [_head_blockdiag(), row(p['rw_r_k'][l]), row(p['rw_gn_g'][l]), row(p['rw_gn_b'][l]),
              row(p['ln1_g'][l]), row(p['ln1_b'][l]), p['w_router'][l], row(p['b_router'][l]), *weights]
    in_specs = (_token_specs(tokens, tm, D_MODEL, lambda i: i) + [pair] * 5
                + [tile(RET_V_W), tile(D_MODEL, 0), tile(D_MODEL, 1)] + [full(a) for a in consts])
    return pl.pallas_call(
        functools.partial(_post_kernel, len(weights) // 3, (None, None) + tokens[2:]),
        grid=(n // tm,),
        in_specs=in_specs,
        out_specs=[tile(D_MODEL), tile(D_MODEL), tile(TOP_K), tile(TOP_K)],
        out_shape=[jax.ShapeDtypeStruct((n, D_MODEL), F32), jax.ShapeDtypeStruct((n, D_MODEL), BF16),
                   jax.ShapeDtypeStruct((n, TOP_K), jnp.int32), jax.ShapeDtypeStruct((n, TOP_K), F32)],
        compiler_params=_params("parallel"),
        name="merge_ln_router",
    )(tokens[0], tokens[1], yrw, r, k, v, g, yret, gates, gates, *consts)


SEG_ALIGN = 8
ROUTE_TILE = 512
LOCAL_ROWS = 2304
LOCAL_BLOCK = 768
PROB_LANES = 128
ROW_W = D_MODEL + PROB_LANES


def _start_segments(i, cnt_ref, lst_ref, gst_ref, make_copy):
    def per_expert(e, carry):
        j = i * N_EXPERTS + e
        rows = pl.multiple_of(cnt_ref[j], SEG_ALIGN)

        @pl.when(rows > 0)
        def _():
            make_copy(pl.multiple_of(lst_ref[j], SEG_ALIGN), pl.multiple_of(gst_ref[j], SEG_ALIGN), rows).start()

        return carry

    lax.fori_loop(0, N_EXPERTS, per_expert, 0)


def _wait_segments(i, tot_ref, make_copy):
    make_copy(0, 0, pl.multiple_of(tot_ref[i], SEG_ALIGN)).wait()


def _dispatch_kernel(cnt_ref, lst_ref, gst_ref, tot_ref, ids_ref, probs_ref, lstart_ref, xb_ref, xs_hbm, buf, sem):
    i = pl.program_id(0)
    tm = xb_ref.shape[0]
    ids = ids_ref[0]
    probs = probs_ref[0]
    expert = lax.broadcasted_iota(jnp.int32, (N_EXPERTS, tm), 0)
    picks = [ids[k:k + 1, :] == expert for k in range(TOP_K)]
    picked = sum(pk.astype(F32) for pk in picks)
    m = lax.broadcasted_iota(jnp.int32, (tm, tm), 0)
    n = lax.broadcasted_iota(jnp.int32, (tm, tm), 1)
    earlier = (m < n).astype(BF16)
    rank = jnp.dot(picked.astype(BF16), earlier, preferred_element_type=F32)
    pos = lstart_ref[0] + rank
    lpos = [jnp.sum(jnp.where(picks[k], pos, 0.0), 0, keepdims=True).astype(jnp.int32) for k in range(TOP_K)]

    def copies_of(slot):
        return lambda lo, go, size: pltpu.make_async_copy(buf.at[slot, pl.ds(lo, size)], xs_hbm.at[pl.ds(go, size)],
                                                          sem.at[slot])

    slot = i % 2

    @pl.when(i >= 2)
    def _():
        _wait_segments(i - 2, tot_ref, copies_of(slot))

    xb = xb_ref[...]
    for r0 in range(0, LOCAL_ROWS, LOCAL_BLOCK):
        row = r0 + lax.broadcasted_iota(jnp.int32, (LOCAL_BLOCK, tm), 0)
        perm = jnp.zeros((LOCAL_BLOCK, tm), F32)
        weight = jnp.zeros((LOCAL_BLOCK, tm), F32)
        for k in range(TOP_K):
            hit = row == lpos[k]
            perm = jnp.where(hit, 1.0, perm)
            weight = jnp.where(hit, probs[k:k + 1, :], weight)
        buf[slot, r0:r0 + LOCAL_BLOCK, :D_MODEL] = jnp.dot(perm.astype(BF16), xb, preferred_element_type=F32)
        buf[slot, r0:r0 + LOCAL_BLOCK, D_MODEL:] = jnp.broadcast_to(jnp.sum(weight, 1, keepdims=True),
                                                                      (LOCAL_BLOCK, PROB_LANES))
    _start_segments(i, cnt_ref, lst_ref, gst_ref, copies_of(slot))

    @pl.when(i == pl.num_programs(0) - 1)
    def _():
        @pl.when(i >= 1)
        def _():
            _wait_segments(i - 1, tot_ref, copies_of(1 - slot))

        _wait_segments(i, tot_ref, copies_of(slot))


def _round_up(x, m):
    return (x + m - 1) // m * m


def _routing_tables(ids):
    n = ids.shape[0]
    nt = n // ROUTE_TILE
    picked = jnp.sum(ids[:, :, None] == jnp.arange(N_EXPERTS, dtype=jnp.int32)[None, None, :], 1)
    cnt = jnp.sum(picked.reshape(nt, ROUTE_TILE, N_EXPERTS), 1).astype(jnp.int32)
    cnt = _round_up(cnt, SEG_ALIGN)
    lstart = jnp.cumsum(cnt, 1) - cnt
    per_expert = jnp.sum(cnt, 0)
    region = _round_up(per_expert, EXPERT_ROW_TILE)
    gstart = (jnp.cumsum(region) - region)[None, :] + jnp.cumsum(cnt, 0) - cnt
    return cnt, lstart.astype(jnp.int32), gstart.astype(jnp.int32), jnp.sum(cnt, 1), region


def _table_args(tables):
    cnt, lstart, gstart, tile_rows, _ = tables
    return cnt.reshape(-1), lstart.reshape(-1), gstart.reshape(-1), tile_rows


def _dispatch(xb, ids, probs, tables, n_rows):
    n = xb.shape[0]
    tm = ROUTE_TILE
    nt = n // tm
    lstart = tables[1]
    to_lanes = lambda a: a.reshape(nt, tm, TOP_K).transpose(0, 2, 1)
    grid_spec = pltpu.PrefetchScalarGridSpec(
        num_scalar_prefetch=4,
        grid=(nt,),
        in_specs=[pl.BlockSpec((1, TOP_K, tm), lambda i, *_: (i, 0, 0)), pl.BlockSpec((1, TOP_K, tm), lambda i, *_: (i, 0, 0)),
                  pl.BlockSpec((1, N_EXPERTS, 1), lambda i, *_: (i, 0, 0)), pl.BlockSpec((tm, D_MODEL), lambda i, *_: (i, 0))],
        out_specs=pl.BlockSpec(memory_space=pl.ANY),
        scratch_shapes=[pltpu.VMEM((2, LOCAL_ROWS, ROW_W), F32), pltpu.SemaphoreType.DMA((2,))],
    )
    return pl.pallas_call(
        _dispatch_kernel,
        grid_spec=grid_spec,
        out_shape=jax.ShapeDtypeStruct((n_rows, ROW_W), F32),
        compiler_params=_params("arbitrary"),
        name="dispatch",
    )(*_table_args(tables), to_lanes(ids), to_lanes(probs), lstart.astype(F32)[:, :, None], xb)


def _expert_kernel(layer, tile_ref, exp_ref, valid_ref, slot_ref, next_ref, x_ref, wgu_hbm, bgu_ref, wd_hbm, bd_ref,
                   o_ref, wgu_f, wd_f, wgu_s, wd_s, sem):
    i = pl.program_id(0)

    def weight_copies(e, slot):
        return (pltpu.make_async_copy(wgu_hbm.at[layer, e], wgu_f.at[slot], sem.at[0, slot]),
                pltpu.make_async_copy(wd_hbm.at[layer, e], wd_f.at[slot], sem.at[1, slot]))

    @pl.when(valid_ref[i] == 1)
    def _():
        e = exp_ref[i]
        slot = slot_ref[i]

        @pl.when(i == 0)
        def _():
            for c in weight_copies(e, slot):
                c.start()

        @pl.when((i == 0) | (exp_ref[jnp.maximum(i - 1, 0)] != e))
        def _():
            for c in weight_copies(e, slot):
                c.wait()
            wgu_s[...] = wgu_f[slot].astype(BF16)
            wd_s[...] = wd_f[slot].astype(BF16)

            @pl.when(next_ref[i] >= 0)
            def _():
                for c in weight_copies(next_ref[i], 1 - slot):
                    c.start()

        h = jnp.dot(x_ref[:, :D_MODEL].astype(BF16), wgu_s[...], preferred_element_type=F32) + bgu_ref[0, 0]
        gate = jnp.minimum(h[:, :D_FF], SWIGLU_LIMIT)
        up = jnp.clip(h[:, D_FF:], -SWIGLU_LIMIT, SWIGLU_LIMIT)
        act = gate * _sigmoid(SWIGLU_ALPHA * gate) * (up + 1.0)
        y = jnp.dot(act.astype(BF16), wd_s[...], preferred_element_type=F32) + bd_ref[0, 0]
        o_ref[...] = y * x_ref[:, D_MODEL:D_MODEL + 1]


def _expert_schedule(region, n_rows, tm):
    n_entries = n_rows // tm
    tile_end = jnp.cumsum(region // tm)
    total = tile_end[-1]
    t = jnp.minimum(jnp.arange(n_entries), total - 1).astype(jnp.int32)
    e = jnp.sum(tile_end[None, :] <= t[:, None], -1).astype(jnp.int32)
    valid = (jnp.arange(n_entries) < total).astype(jnp.int32)
    used = region > 0
    slot = (jnp.cumsum(used) - 1) % 2
    ids = jnp.where(used, jnp.arange(N_EXPERTS), N_EXPERTS)
    first_used_from = jnp.flip(lax.cummin(jnp.flip(ids)))
    nxt = jnp.concatenate([first_used_from[1:], jnp.full((1,), N_EXPERTS)])
    nxt = jnp.where(nxt == N_EXPERTS, -1, nxt)
    return t, e, valid, slot[e].astype(jnp.int32), nxt[e].astype(jnp.int32)


def _experts(xs, sched, p, l):
    n_rows = xs.shape[0]
    tm = EXPERT_ROW_TILE
    bgu = p['b_gate_up'].reshape(DEPTH, N_EXPERTS, 1, 2 * D_FF)
    bdn = p['b_down'].reshape(DEPTH, N_EXPERTS, 1, D_MODEL)
    by_tile = lambda w: pl.BlockSpec((tm, w), lambda i, t, e, *_: (t[i], 0))
    by_expert = lambda a, b: pl.BlockSpec((1, 1, a, b), lambda i, t, e, *_: (l, e[i], 0, 0))
    in_hbm = pl.BlockSpec(memory_space=pl.ANY)
    grid_spec = pltpu.PrefetchScalarGridSpec(
        num_scalar_prefetch=5,
        grid=(sched[0].shape[0],),
        in_specs=[by_tile(ROW_W), in_hbm, by_expert(1, 2 * D_FF), in_hbm, by_expert(1, D_MODEL)],
        out_specs=by_tile(D_MODEL),
        scratch_shapes=[pltpu.VMEM((2, D_MODEL, 2 * D_FF), F32), pltpu.VMEM((2, D_FF, D_MODEL), F32),
                        pltpu.VMEM((D_MODEL, 2 * D_FF), BF16), pltpu.VMEM((D_FF, D_MODEL), BF16),
                        pltpu.SemaphoreType.DMA((2, 2))],
    )
    return pl.pallas_call(
        functools.partial(_expert_kernel, l),
        grid_spec=grid_spec,
        out_shape=jax.ShapeDtypeStruct((n_rows, D_MODEL), F32),
        compiler_params=_params("arbitrary"),
        name="experts",
    )(*sched, xs, p['w_gate_up'], bgu, p['w_down'], bdn)


def _combine_kernel(tile0, cnt_ref, lst_ref, gst_ref, tot_ref, ids_ref, lstart_ref, x_ref, ln_g_ref, ln_b_ref, ys_hbm,
                    o_ref, buf, sem):
    step = pl.program_id(0)
    i = step + tile0
    tm = x_ref.shape[0]
    slot = step % 2

    def copies_of(slot):
        return lambda lo, go, size: pltpu.make_async_copy(ys_hbm.at[pl.ds(go, size)], buf.at[slot, pl.ds(lo, size)],
                                                          sem.at[slot])

    def fetch(tile, slot):
        buf[slot, ROUTE_TILE * TOP_K:, :] = jnp.zeros((LOCAL_ROWS - ROUTE_TILE * TOP_K, D_MODEL), F32)
        _start_segments(tile, cnt_ref, lst_ref, gst_ref, copies_of(slot))

    @pl.when(step == 0)
    def _():
        fetch(i, slot)

    @pl.when(step + 1 < pl.num_programs(0))
    def _():
        fetch(i + 1, 1 - slot)

    ids = ids_ref[...]
    expert = lax.broadcasted_iota(jnp.int32, (tm, N_EXPERTS), 1)
    picks = [ids[:, k:k + 1] == expert for k in range(TOP_K)]
    picked = sum(pk.astype(F32) for pk in picks)
    m = lax.broadcasted_iota(jnp.int32, (tm, tm), 0)
    n = lax.broadcasted_iota(jnp.int32, (tm, tm), 1)
    earlier = (n < m).astype(BF16)
    rank = jnp.dot(earlier, picked.astype(BF16), preferred_element_type=F32)
    pos = lstart_ref[0] + rank
    lpos = [jnp.sum(jnp.where(picks[k], pos, 0.0), 1, keepdims=True).astype(jnp.int32) for k in range(TOP_K)]

    _wait_segments(i, tot_ref, copies_of(slot))
    moe = jnp.zeros((tm, D_MODEL), F32)
    for r0 in range(0, LOCAL_ROWS, LOCAL_BLOCK):
        col = r0 + lax.broadcasted_iota(jnp.int32, (tm, LOCAL_BLOCK), 1)
        perm = jnp.zeros((tm, LOCAL_BLOCK), F32)
        for k in range(TOP_K):
            perm = jnp.where(col == lpos[k], 1.0, perm)
        moe = moe + jnp.dot(perm.astype(BF16), buf[slot, r0:r0 + LOCAL_BLOCK, :].astype(BF16),
                            preferred_element_type=F32)
    o_ref[...] = _layer_norm(DN_ALPHA * x_ref[...] + moe, ln_g_ref[...], ln_b_ref[...])


def _combine(x1, ys, ids, tables, ln_g, ln_b, tile0, n_tiles):
    tm = ROUTE_TILE
    lstart = tables[1]
    at = lambda i, *_: (i + tile0, 0)
    grid_spec = pltpu.PrefetchScalarGridSpec(
        num_scalar_prefetch=4,
        grid=(n_tiles,),
        in_specs=[pl.BlockSpec((tm, TOP_K), at), pl.BlockSpec((1, 1, N_EXPERTS), lambda i, *_: (i + tile0, 0, 0)),
                  pl.BlockSpec((tm, D_MODEL), at), pl.BlockSpec((1, D_MODEL), lambda i, *_: (0, 0)),
                  pl.BlockSpec((1, D_MODEL), lambda i, *_: (0, 0)), pl.BlockSpec(memory_space=pl.ANY)],
        out_specs=pl.BlockSpec((tm, D_MODEL), lambda i, *_: (i, 0)),
        scratch_shapes=[pltpu.VMEM((2, LOCAL_ROWS, D_MODEL), F32), pltpu.SemaphoreType.DMA((2,))],
    )
    return pl.pallas_call(
        functools.partial(_combine_kernel, tile0),
        grid_spec=grid_spec,
        out_shape=jax.ShapeDtypeStruct((n_tiles * tm, D_MODEL), F32),
        compiler_params=_params("arbitrary"),
        name="combine_ln",
    )(*_table_args(tables), ids, lstart.astype(F32)[:, None, :], x1, ln_g.reshape(1, -1), ln_b.reshape(1, -1), ys)


def _moe(x1, xb, ids, probs, p, l, parts):
    n = x1.shape[0]
    tm = EXPERT_ROW_TILE
    seg_rows = n * TOP_K + (n // ROUTE_TILE) * N_EXPERTS * (SEG_ALIGN - 1)
    n_rows = _round_up(seg_rows + N_EXPERTS * (tm - SEG_ALIGN), tm)
    tables = _routing_tables(ids)
    xs = _dispatch(xb, ids, probs, tables, n_rows)
    ys = _experts(xs, _expert_schedule(tables[4], n_rows, tm), p, l)
    return [_combine(x1, ys, ids, tables, p['ln2_g'][l], p['ln2_b'][l], *part) for part in parts]


def _pad_time(a, tp):
    return jnp.pad(a, ((0, 0), (0, tp - a.shape[1]), (0, 0)))


def kernel(x_prompt, x_sample, state_ret, state_rwkv, state_shift, w_in, ret_gn_g, ret_gn_b, w_ret_out, rw_mu, rw_w0, rw_w_up, rw_a0, rw_a_up, rw_g_up, rw_k_k, rw_k_a, rw_r_k, rw_gn_g, rw_gn_b, rw_v0, rw_vres_down, rw_vres_up, w_rwkv_out, w_o, ln1_g, ln1_b, w_router, b_router, w_gate_up, b_gate_up, w_down, b_down, ln2_g, ln2_b):
    p = dict(w_ret_out=w_ret_out, rw_mu=rw_mu, rw_w0=rw_w0, rw_w_up=rw_w_up, rw_a0=rw_a0, rw_a_up=rw_a_up,
             rw_g_up=rw_g_up, rw_k_k=rw_k_k, rw_k_a=rw_k_a, rw_r_k=rw_r_k, rw_gn_g=rw_gn_g, rw_gn_b=rw_gn_b,
             w_rwkv_out=w_rwkv_out, w_o=w_o, ln1_g=ln1_g, ln1_b=ln1_b, w_router=w_router, b_router=b_router,
             w_gate_up=w_gate_up, b_gate_up=b_gate_up, w_down=w_down, b_down=b_down, ln2_g=ln2_g, ln2_b=ln2_b)
    bp, tp, _ = x_prompt.shape
    bs, ts, _ = x_sample.shape
    np_, ns = bp * tp, bs * ts
    pos_p = jnp.arange(tp, dtype=F32)
    ts_ret = 8
    pos_s = PAST_LEN + jnp.arange(ts_ret, dtype=F32)

    assert ns == TOKEN_TILE and np_ % TOKEN_TILE == 0
    n_prompt_tiles = np_ // TOKEN_TILE
    tokens = (x_prompt.reshape(np_, D_MODEL), x_sample.transpose(1, 0, 2).reshape(ns, D_MODEL), 0, n_prompt_tiles)
    outs = {k: [] for k in ('ret_p', 'rw_p', 'sh_p', 'sh_s')}
    sret_s = srw_s = None
    v_first = None
    u_off = RET_W
    g_off = RET_W + SHIFT_W
    for l in range(DEPTH):
        passes = EXACT_PASSES if l == 0 else 1
        z_ret = _matmul(tokens, w_in[l][:, :u_off], TOKEN_TILE, RET_W if passes == 1 else RET_W // 2, passes)
        u = _matmul(tokens, w_in[l][:, u_off:g_off], TOKEN_TILE, SHIFT_W, passes)
        gates = _matmul(tokens, w_in[l][:, g_off:], TOKEN_TILE, D_MODEL, passes)

        zr_s = _pad_time(z_ret[np_:].reshape(ts, bs, RET_W).transpose(1, 0, 2), ts_ret)
        yret, sret_p = _retention(z_ret.reshape(-1, RET_CHUNK, RET_W), pos_p, tp, None, ret_gn_g[l], ret_gn_b[l], 1,
                                  passes, n_seq=bp)
        yret_s, sret_s = _retention(zr_s, pos_s, ts, (state_ret, l, sret_s), ret_gn_g[l], ret_gn_b[l], 8, passes)
        yret = yret.reshape(-1, RET_V_W).at[np_:].set(yret_s[:, :ts].transpose(1, 0, 2).reshape(ns, RET_V_W))

        vres = None if l == 0 else (rw_v0[l - 1], rw_vres_down[l - 1], rw_vres_up[l - 1])
        r, lw, k, v, a, b, g = _rwkv_pre(u, state_shift[l], np_, tp, p, l, v_first, vres)
        if l == 0:
            v_first = v
        yrw, srw_p = _rwkv_chunks(r, lw, k, v, a, b, passes)
        yrw, srw_s = _rwkv_steps(r, lw, k, v, a, b, (state_rwkv, l, srw_s), yrw)

        x1, xb, ids, probs = _post(tokens, yrw, r, k, v, g, yret, gates, tp, p, l, passes)
        if l + 1 < DEPTH:
            x, = _moe(x1, xb, ids, probs, p, l, [(0, n_prompt_tiles + 1)])
            tokens = (x, x, n_prompt_tiles, n_prompt_tiles)
        else:
            y_p, y_s = _moe(x1, xb, ids, probs, p, l, [(0, n_prompt_tiles), (n_prompt_tiles, 1)])

        outs['ret_p'].append(sret_p)
        outs['rw_p'].append(srw_p)
        outs['sh_p'].append(u[tp - 1:np_:tp])
        outs['sh_s'].append(u[np_ + ns - bs:])

    y_prompt = y_p.reshape(bp, tp, D_MODEL)
    y_sample = y_s.reshape(ts, bs, D_MODEL).transpose(1, 0, 2)
    st = {k: jnp.stack(v) for k, v in outs.items()}
    return (y_prompt, y_sample, st['ret_p'], st['rw_p'], st['sh_p'], sret_s, srw_s, st['sh_s'])
```

```python
import functools

import jax
import jax.numpy as jnp
from jax import lax
from jax.experimental import pallas as pl
from jax.experimental.pallas import tpu as pltpu

F32 = jnp.float32
BF16 = jnp.bfloat16
HI = lax.Precision.HIGHEST

D_MODEL = 1024
DEPTH = 2
PAST_LEN = 16384
RET_HEADS = 4
RET_DK = 128
RET_DV = 256
RET_QK_W = RET_HEADS * RET_DK
RET_V_W = RET_HEADS * RET_DV
RET_W = 2 * RET_QK_W + 2 * RET_V_W
RET_CHUNK = 128
ROPE_BASE = 10000.0
RWKV_HEADS = 8
RWKV_N = 64
RWKV_W = RWKV_HEADS * RWKV_N
LORA_W = 64
LORA_A = 64
LORA_G = 128
SHIFT_W = 3 * RWKV_W + LORA_W + LORA_A + LORA_G
RWKV_CHUNK = 64
N_EXPERTS = 32
TOP_K = 4
D_FF = D_MODEL
SWIGLU_LIMIT = 7.0
SWIGLU_ALPHA = 1.702
DN_ALPHA = (2 * DEPTH) ** 0.25
LN_EPS = 1e-5
RET_GN_EPS = 1e-5
RWKV_GN_EPS = 64e-5

VMEM_LIMIT = 56 * 1024 * 1024
TOKEN_TILE = 512
EXPERT_ROW_TILE = 512
RWKV_PAIRS_PER_STEP = 2
RWKV_STEP_BATCH_BLOCK = 8
EXACT_PASSES = 3
RWKV_REFINE_STEPS = 1


def _params(*sem):
    return pltpu.CompilerParams(dimension_semantics=sem, vmem_limit_bytes=VMEM_LIMIT)


def _split(x):
    hi = x.astype(BF16)
    lo = (x - hi.astype(F32)).astype(BF16)
    return hi, lo


def _split_kernel(w_ref, hi_ref, lo_ref):
    hi_ref[...], lo_ref[...] = _split(w_ref[...])


def _split_weight(w, passes):
    if passes == 1:
        return (w.astype(BF16),)
    rows = 256
    spec = pl.BlockSpec((rows, w.shape[1]), lambda i: (i, 0))
    return tuple(pl.pallas_call(
        _split_kernel,
        grid=(w.shape[0] // rows,),
        in_specs=[spec],
        out_specs=[spec, spec],
        out_shape=[jax.ShapeDtypeStruct(w.shape, BF16)] * 2,
        compiler_params=_params("parallel"),
        name="split_weight",
    )(w))


def _mm(a, b, spec, passes):
    dg = lambda x, y: jnp.einsum(spec, x, y, preferred_element_type=F32)
    if passes == 1:
        return dg(a.astype(BF16), b.astype(BF16))
    ah, al = _split(a)
    bh, bl = _split(b)
    return dg(ah, bh) + (dg(ah, bl) + dg(al, bh))


def _mm_w(a, w_refs):
    dg = lambda x, y: jnp.dot(x, y, preferred_element_type=F32)
    if len(w_refs) == 1:
        return dg(a.astype(BF16), w_refs[0][...])
    ah, al = _split(a)
    return dg(ah, w_refs[0][...]) + (dg(ah, w_refs[1][...]) + dg(al, w_refs[0][...]))


def _dot_hi(a, b):
    return jnp.dot(a, b, precision=HI, preferred_element_type=F32)


def _sigmoid(x):
    return 1.0 / (1.0 + jnp.exp(-x))


def _layer_norm(x, g, b):
    mu = jnp.mean(x, -1, keepdims=True)
    d = x - mu
    var = jnp.mean(d * d, -1, keepdims=True)
    return d * lax.rsqrt(var + LN_EPS) * g + b


def _token_specs(tokens, tm, w, tile_of):
    _, _, b_block, n_a = tokens
    return [pl.BlockSpec((tm, w), lambda *g: (jnp.minimum(tile_of(*g), n_a - 1), 0)),
            pl.BlockSpec((tm, w), lambda *g: (b_block, 0))]


def _token_tile(tokens, i, a_ref, b_ref):
    return jnp.where(i < tokens[3], a_ref[...], b_ref[...])


def _matmul_kernel(tokens, xa_ref, xb_ref, *refs):
    o_ref = refs[-1]
    o_ref[...] = _mm_w(_token_tile(tokens, pl.program_id(1), xa_ref, xb_ref), refs[:-1])


def _matmul(tokens, w, tm, tn, passes):
    k, n = w.shape
    n_tiles = tokens[3] + 1
    ws = _split_weight(w, passes)
    return pl.pallas_call(
        functools.partial(_matmul_kernel, (None, None) + tokens[2:]),
        grid=(n // tn, n_tiles),
        in_specs=_token_specs(tokens, tm, k, lambda j, i: i) + [pl.BlockSpec((k, tn), lambda j, i: (0, j))] * len(ws),
        out_specs=pl.BlockSpec((tm, tn), lambda j, i: (i, j)),
        out_shape=jax.ShapeDtypeStruct((n_tiles * tm, n), F32),
        compiler_params=_params("parallel", "parallel"),
        name="in_proj",
    )(tokens[0], tokens[1], *ws)


def _ret_kernel(has_state, passes, q_ref, k_ref, v_ref, g_ref, cos_ref, sin_ref, dm_ref, qd_ref, kd_ref, cd_ref,
                gng_ref, gnb_ref, *rest):
    if has_state:
        s0_ref, y_ref, so_ref, s_scr = rest[0], *rest[-3:]
    else:
        y_ref, so_ref, s_scr = rest
    c = pl.program_id(1)
    bb, cl, _ = q_ref.shape
    nh = RET_HEADS

    def heads(x, w):
        return jnp.concatenate([x[:, :, h * w:(h + 1) * w] for h in range(nh)], 0)

    def per_head(ref):
        return jnp.concatenate([jnp.broadcast_to(ref[h], (bb,) + ref.shape[1:]) for h in range(nh)], 0)

    @pl.when(c == 0)
    def _():
        if has_state:
            s_scr[...] = jnp.concatenate([s0_ref[:, h] for h in range(nh)], 0)
        else:
            s_scr[...] = jnp.zeros_like(s_scr)

    cos = cos_ref[...]
    sin = sin_ref[...]
    mm = functools.partial(_mm, passes=passes)

    def rope(x):
        x2 = x.reshape(nh * bb * cl, RET_DK)
        rot = pltpu.roll(x2, RET_DK // 2, axis=1).reshape(nh * bb, cl, RET_DK)
        return x * cos + rot * sin

    q = rope(heads(q_ref[...], RET_DK))
    k = rope(heads(k_ref[...], RET_DK)) * (RET_DK ** -0.5)
    v = heads(v_ref[...], RET_DV)
    s = s_scr[...]
    sc = mm(q, k, 'bid,bjd->bij') * per_head(dm_ref)
    intra = mm(sc, v, 'bij,bje->bie')
    cross = mm(q, s, 'bid,bde->bie') * per_head(qd_ref)
    s_new = s * per_head(cd_ref) + mm(k * per_head(kd_ref), v, 'bjd,bje->bde')
    s_scr[...] = s_new

    y = intra + cross
    mu = jnp.mean(y, -1, keepdims=True)
    d = y - mu
    var = jnp.mean(d * d, -1, keepdims=True)
    yn = d * lax.rsqrt(var + RET_GN_EPS)
    for h in range(nh):
        cols = slice(h * RET_DV, (h + 1) * RET_DV)
        rg = g_ref[:, :, cols]
        y_ref[:, :, cols] = (yn[h * bb:(h + 1) * bb] * gng_ref[:, cols] + gnb_ref[:, cols]) * (rg * _sigmoid(rg))

    @pl.when(c == pl.num_programs(1) - 1)
    def _():
        for h in range(nh):
            so_ref[:, h] = s_new[h * bb:(h + 1) * bb]


def _retention(z, pos, t_real, s0, gn_g, gn_b, bb, passes, n_seq=None):
    flat = n_seq is not None
    tp = pos.shape[0]
    b = n_seq if flat else z.shape[0]
    cl = RET_CHUNK if t_real % RET_CHUNK == 0 else tp
    cr = min(cl, t_real)
    nc = tp // cl
    at = (lambda bi, c: (bi * nc + c, 0)) if flat else (lambda bi, c: (bi, c))
    half = RET_DK // 2
    inv = ROPE_BASE ** (-jnp.arange(half, dtype=F32) / half)
    ang = pos[:, None] * inv[None, :]
    cos = jnp.concatenate([jnp.cos(ang), jnp.cos(ang)], -1)
    sin = jnp.concatenate([-jnp.sin(ang), jnp.sin(ang)], -1)
    lg = jnp.log1p(-jnp.exp2(-5.0 - jnp.arange(RET_HEADS, dtype=F32)))
    i = jnp.arange(cl, dtype=F32)
    real = i < cr
    diff = i[:, None] - i[None, :]
    ok = (diff >= 0) & real[:, None] & real[None, :]
    dmask = jnp.exp(jnp.where(ok[None], diff[None] * lg[:, None, None], -jnp.inf))
    q_dec = jnp.exp((i[None, :] + 1.0) * lg[:, None])[..., None]
    k_dec = jnp.where(real[None, :], jnp.exp((cr - 1.0 - i)[None, :] * lg[:, None]), 0.0)[..., None]
    c_dec = jnp.exp(cr * lg)[:, None, None]

    has_state = s0 is not None
    cols = lambda w, j: pl.BlockSpec((bb, cl, w), lambda bi, c: at(bi, c) + (j,))
    full = lambda a: pl.BlockSpec(a.shape, lambda bi, c: (0,) * a.ndim)
    consts = [dmask, q_dec, k_dec, c_dec, gn_g.reshape(1, -1), gn_b.reshape(1, -1)]
    in_specs = [cols(RET_QK_W, 0), cols(RET_QK_W, 1), cols(RET_V_W, 1), cols(RET_V_W, 2),
                pl.BlockSpec((cl, RET_DK), lambda bi, c: (c, 0)), pl.BlockSpec((cl, RET_DK), lambda bi, c: (c, 0))]
    in_specs += [full(a) for a in consts]
    args = [z, z, z, z, cos, sin] + consts
    aliases = {}
    if has_state:
        all_states, layer, earlier = s0
        s_spec = pl.BlockSpec((None, bb, RET_HEADS, RET_DK, RET_DV), lambda bi, c: (layer, bi, 0, 0, 0))
        s_shape = all_states.shape
        in_specs.append(s_spec)
        args.append(all_states)
        if earlier is not None:
            aliases = {len(args): 1}
            in_specs.append(pl.BlockSpec(memory_space=pl.ANY))
            args.append(earlier)
    else:
        s_spec = pl.BlockSpec((bb, RET_HEADS, RET_DK, RET_DV), lambda bi, c: (bi, 0, 0, 0))
        s_shape = (b, RET_HEADS, RET_DK, RET_DV)
    return pl.pallas_call(
        functools.partial(_ret_kernel, has_state, passes),
        grid=(b // bb, nc),
        in_specs=in_specs,
        out_specs=[cols(RET_V_W, 0), s_spec],
        out_shape=[jax.ShapeDtypeStruct(z.shape[:2] + (RET_V_W,), F32), jax.ShapeDtypeStruct(s_shape, F32)],
        scratch_shapes=[pltpu.VMEM((RET_HEADS * bb, RET_DK, RET_DV), F32)],
        input_output_aliases=aliases,
        compiler_params=_params("parallel", "arbitrary"),
        name="retention",
    )(*args)


def _head_sum(x, bd):
    hi, lo = _split(x)
    return jnp.dot(hi, bd, preferred_element_type=F32) + jnp.dot(lo, bd, preferred_element_type=F32)


def _pair_shape(n_pairs, chunks, w):
    return (n_pairs + 1, chunks, 2, RWKV_CHUNK, w)


def _pair_tile_spec(tiles_per_seq, w):
    rows = TOKEN_TILE // RWKV_CHUNK
    return pl.BlockSpec((1, rows, 1, RWKV_CHUNK, w),
                        lambda i: (i // (2 * tiles_per_seq), i % tiles_per_seq, (i // tiles_per_seq) % 2, 0, 0))


def _to_pair_tile(ref, x):
    ref[0, :, 0] = x.reshape(TOKEN_TILE // RWKV_CHUNK, RWKV_CHUNK, x.shape[-1])


def _from_pair_tile(ref):
    return ref[0, :, 0].reshape(TOKEN_TILE, ref.shape[-1])


def _rwkv_pre_kernel(has_vres, n_prompt_tiles, tiles_per_seq, u_ref, tail_ref, shift_ref, mu_ref, w0_ref, wup_ref,
                     a0_ref, aup_ref, gup_ref, kk_ref, ka_ref, bd_ref, *rest):
    if has_vres:
        vf_ref, v0_ref, vd_ref, vu_ref, r_o, lw_o, k_o, v_o, a_o, b_o, g_o = rest
    else:
        r_o, lw_o, k_o, v_o, a_o, b_o, g_o = rest
    i = pl.program_id(0)
    u = u_ref[...]
    tm = u.shape[0]
    row = lax.broadcasted_iota(jnp.int32, (tm, 1), 0)
    before = jnp.where(i % tiles_per_seq == 0, 0.0, tail_ref[tail_ref.shape[0] - 1:, :])
    prev_prompt = jnp.where(row == 0, before, pltpu.roll(u, 1, axis=0))
    n_seq = shift_ref.shape[0]
    prev_sample = jnp.concatenate([shift_ref[...], u[:tm - n_seq]], 0)
    prev = jnp.where(i < n_prompt_tiles, prev_prompt, prev_sample)
    um = u + (prev - u) * mu_ref[...]
    w1, w2, w3 = RWKV_W, 2 * RWKV_W, 3 * RWKV_W
    r = um[:, :w1]
    kw = um[:, w1:w2]
    vw = um[:, w2:w3]
    wd = um[:, w3:w3 + LORA_W]
    ad = um[:, w3 + LORA_W:w3 + LORA_W + LORA_A]
    gd = um[:, w3 + LORA_W + LORA_A:]
    lora = lambda x, w_ref: _mm(x, w_ref[...], 'ik,kj->ij', EXACT_PASSES)
    xw = w0_ref[...] + lora(jnp.tanh(wd), wup_ref)
    softplus = jnp.maximum(-xw, 0.0) + jnp.log1p(jnp.exp(-jnp.abs(xw)))
    _to_pair_tile(lw_o, -jnp.exp(-softplus - 0.5))
    a = _sigmoid(a0_ref[...] + lora(ad, aup_ref))
    _to_pair_tile(g_o, lora(_sigmoid(gd), gup_ref))
    if has_vres:
        gate = _sigmoid(v0_ref[...] + lora(lora(vw, vd_ref), vu_ref))
        vw = vw + (_from_pair_tile(vf_ref) - vw) * gate
    kk = kw * kk_ref[...]
    norm = jnp.sqrt(_head_sum(kk * kk, bd_ref[...]))
    kk = kk / jnp.maximum(norm, 1e-12)
    _to_pair_tile(r_o, r)
    _to_pair_tile(k_o, kw * (1.0 + (a - 1.0) * ka_ref[...]))
    _to_pair_tile(v_o, vw)
    _to_pair_tile(a_o, -kk)
    _to_pair_tile(b_o, kk * a)


def _head_blockdiag():
    h = jnp.arange(RWKV_W) // RWKV_N
    return (h[:, None] == h[None, :]).astype(BF16)


def _rwkv_pre(u, shift_state, n_prompt, seq_len, p, l, v_first, vres):
    n = u.shape[0]
    tm = TOKEN_TILE
    assert n == n_prompt + tm and seq_len % tm == 0
    tiles_per_seq = seq_len // tm
    tail_rows = 8
    row = lambda a: a.reshape(1, -1)
    full = lambda a: pl.BlockSpec(a.shape, lambda i: (0,) * a.ndim)
    pair = _pair_tile_spec(tiles_per_seq, RWKV_W)
    has_vres = vres is not None
    args = [u, u, shift_state, row(p['rw_mu'][l]), row(p['rw_w0'][l]), p['rw_w_up'][l], row(p['rw_a0'][l]),
            p['rw_a_up'][l], p['rw_g_up'][l], row(p['rw_k_k'][l]), row(p['rw_k_a'][l]), _head_blockdiag()]
    in_specs = [pl.BlockSpec((tm, SHIFT_W), lambda i: (i, 0)),
                pl.BlockSpec((tail_rows, SHIFT_W), lambda i: (jnp.maximum(i * (tm // tail_rows) - 1, 0), 0))]
    in_specs += [full(a) for a in args[2:]]
    if has_vres:
        extra = [v_first, row(vres[0]), vres[1], vres[2]]
        in_specs += [pair] + [full(a) for a in extra[1:]]
        args += extra
    shape = _pair_shape(n_prompt // seq_len // 2, seq_len // RWKV_CHUNK, RWKV_W)
    return pl.pallas_call(
        functools.partial(_rwkv_pre_kernel, has_vres, n_prompt // tm, tiles_per_seq),
        grid=(n // tm,),
        in_specs=in_specs,
        out_specs=[pair] * 7,
        out_shape=[jax.ShapeDtypeStruct(shape, F32)] * 7,
        compiler_params=_params("parallel"),
        name="rwkv_pre",
    )(*args)


def _rwkv_chunk_kernel(passes, r_ref, lw_ref, k_ref, v_ref, a_ref, b_ref, tri_ref, y_ref, so_ref, s_scr):
    c = pl.program_id(1)

    @pl.when(c == 0)
    def _():
        s_scr[...] = jnp.zeros_like(s_scr)

    n_pairs, _, per_pair, cl, _ = r_ref.shape
    bb = n_pairs * per_pair
    r_all, lw, k_all, v_all, a_all, b_all = (ref[:, 0].reshape(bb, cl, RWKV_W) for ref in
                                             (r_ref, lw_ref, k_ref, v_ref, a_ref, b_ref))
    mm = functools.partial(_mm, passes=passes)
    ti = lax.broadcasted_iota(jnp.int32, (cl, cl), 0)
    si = lax.broadcasted_iota(jnp.int32, (cl, cl), 1)
    strict = (ti > si).astype(F32)
    incl = (ti >= si).astype(F32)
    eye = (ti == si).astype(F32)

    def heads(x):
        return jnp.stack([x[bi][:, h * RWKV_N:(h + 1) * RWKV_N] for bi in range(bb) for h in range(RWKV_HEADS)])

    cum = jnp.stack([_dot_hi(tri_ref[...], lw[bi]) for bi in range(bb)])
    last = cum[:, cl - 1:cl, :]
    e_neg = jnp.exp(-cum)
    e_end = jnp.exp(last - cum)
    at = heads(a_all * jnp.exp(cum - lw))
    rt = heads(r_all * jnp.exp(cum))
    bt = heads(b_all * e_neg)
    kt = heads(k_all * e_neg)
    bw = heads(b_all * e_end)
    kw = heads(k_all * e_end)
    wc = heads(jnp.exp(last))
    vh = heads(v_all)

    lhs = jnp.concatenate([at, rt], 1)
    gram = mm(lhs, jnp.concatenate([bt, kt], 1), 'gik,gjk->gij')
    a_ab = gram[:, :cl, :cl] * strict
    a_ak = gram[:, :cl, cl:] * strict
    a_rb = gram[:, cl:, :cl] * incl
    a_rk = gram[:, cl:, cl:] * incl
    refine = RWKV_REFINE_STEPS if passes > 1 else 0
    mm_inv = functools.partial(_mm, passes=1) if refine else mm
    inv = eye + a_ab
    pw = a_ab
    for _ in range(cl.bit_length() - 2):
        pw = mm_inv(pw, pw, 'gij,gjk->gik')
        inv = inv + mm_inv(inv, pw, 'gij,gjk->gik')
    s0 = s_scr[...].reshape(bb * RWKV_HEADS, RWKV_N, RWKV_N)
    xs = mm(lhs, s0, 'gtj,gij->gti')
    av = mm(jnp.concatenate([a_ak, a_rk], 1), vh, 'gts,gsi->gti')
    rhs = xs[:, :cl] + av[:, :cl]
    u = mm_inv(inv, rhs, 'gts,gsi->gti')
    for _ in range(refine):
        u = u + mm_inv(inv, rhs - u + mm(a_ab, u, 'gts,gsi->gti'), 'gts,gsi->gti')
    y = xs[:, cl:] + av[:, cl:] + mm(a_rb, u, 'gts,gsi->gti')
    s_new = s0 * wc + mm(jnp.concatenate([u, vh], 1), jnp.concatenate([bw, kw], 1), 'gti,gtj->gij')
    s_scr[...] = s_new.reshape(bb, RWKV_HEADS, RWKV_N, RWKV_N)
    for bi in range(bb):
        for h in range(RWKV_HEADS):
            y_ref[bi // per_pair, 0, bi % per_pair, :, h * RWKV_N:(h + 1) * RWKV_N] = y[bi * RWKV_HEADS + h]

    @pl.when(c == pl.num_programs(1) - 1)
    def _():
        so_ref[...] = s_scr[...]


def _rwkv_chunks(r, lw, k, v, a, b, passes):
    n_pairs, chunks, per_pair, cl, _ = r.shape
    n_pairs -= 1
    pp = RWKV_PAIRS_PER_STEP if n_pairs % RWKV_PAIRS_PER_STEP == 0 else 1
    bb = pp * per_pair
    seq = pl.BlockSpec((pp, 1, per_pair, cl, RWKV_W), lambda pi, c: (pi, c, 0, 0, 0))
    s_spec = pl.BlockSpec((bb, RWKV_HEADS, RWKV_N, RWKV_N), lambda pi, c: (pi, 0, 0, 0))
    tri = (jnp.arange(cl)[:, None] >= jnp.arange(cl)[None, :]).astype(F32)
    return pl.pallas_call(
        functools.partial(_rwkv_chunk_kernel, passes),
        grid=(n_pairs // pp, chunks),
        in_specs=[seq] * 6 + [pl.BlockSpec((cl, cl), lambda pi, c: (0, 0))],
        out_specs=[seq, s_spec],
        out_shape=[jax.ShapeDtypeStruct(r.shape, F32),
                   jax.ShapeDtypeStruct((n_pairs * per_pair, RWKV_HEADS, RWKV_N, RWKV_N), F32)],
        scratch_shapes=[pltpu.VMEM((bb, RWKV_HEADS, RWKV_N, RWKV_N), F32)],
        compiler_params=_params("parallel", "arbitrary"),
        name="rwkv_chunks",
    )(r, lw, k, v, a, b, tri)


def _rwkv_step_kernel(r_ref, lw_ref, k_ref, a_ref, b_ref, vt_ref, s0_ref, *rest):
    yt_ref, so_ref = rest[-2:]
    s = s0_ref[...]
    for t in range(r_ref.shape[2]):
        row = lambda ref: ref[:, :, t:t + 1, :]
        sa = jnp.sum(s * row(a_ref), -1, keepdims=True)
        s = s * jnp.exp(row(lw_ref)) + sa * row(b_ref) + vt_ref[:, :, :, t:t + 1] * row(k_ref)
        yt_ref[:, :, :, t:t + 1] = jnp.sum(s * row(r_ref), -1, keepdims=True)
    so_ref[...] = s


def _rwkv_steps(r, lw, k, v, a, b, s0, y_pairs):
    all_states, layer, earlier = s0
    bsz = all_states.shape[1]
    t = TOKEN_TILE // bsz
    bb = RWKV_STEP_BATCH_BLOCK
    tile_rows = TOKEN_TILE // RWKV_CHUNK
    steps = lambda x: x[-1, :tile_rows, 0].reshape(t, bsz, RWKV_HEADS, RWKV_N)
    rows = lambda x: steps(x).transpose(1, 2, 0, 3)
    vt = steps(v).transpose(1, 2, 3, 0)
    row_spec = pl.BlockSpec((bb, RWKV_HEADS, t, RWKV_N), lambda i: (i, 0, 0, 0))
    col_spec = pl.BlockSpec((bb, RWKV_HEADS, RWKV_N, t), lambda i: (i, 0, 0, 0))
    s_spec = pl.BlockSpec((None, bb, RWKV_HEADS, RWKV_N, RWKV_N), lambda i: (layer, i, 0, 0, 0))
    args = [rows(r), rows(lw), rows(k), rows(a), rows(b), vt, all_states]
    in_specs = [row_spec] * 5 + [col_spec, s_spec]
    aliases = {}
    if earlier is not None:
        aliases = {len(args): 1}
        in_specs.append(pl.BlockSpec(memory_space=pl.ANY))
        args.append(earlier)
    yt, s_new = pl.pallas_call(
        _rwkv_step_kernel,
        grid=(bsz // bb,),
        in_specs=in_specs,
        out_specs=[col_spec, s_spec],
        out_shape=[jax.ShapeDtypeStruct((bsz, RWKV_HEADS, RWKV_N, t), F32),
                   jax.ShapeDtypeStruct(all_states.shape, F32)],
        input_output_aliases=aliases,
        compiler_params=_params("parallel"),
        name="rwkv_steps",
    )(*args)
    y_tile = yt.transpose(3, 0, 1, 2).reshape(tile_rows, RWKV_CHUNK, RWKV_W)
    return y_pairs.at[-1, :tile_rows, 0].set(y_tile), s_new


def _post_kernel(n_w, tokens, xa_ref, xb_in_ref, yrw_ref, r_ref, k_ref, v_ref, g_ref, yret_ref, ga_ref, gb_ref, bd_ref,
                 rk_ref, gng_ref, gnb_ref, ln_g_ref, ln_b_ref, wr_ref, br_ref, *rest):
    wret, wrw, wo = rest[:n_w], rest[n_w:2 * n_w], rest[2 * n_w:3 * n_w]
    x1_ref, xb_ref, ids_ref, probs_ref = rest[3 * n_w:]
    x = _token_tile(tokens, pl.program_id(0), xa_ref, xb_in_ref)
    bd = bd_ref[...]
    y = _from_pair_tile(yrw_ref)
    mu = _head_sum(y, bd) * (1.0 / RWKV_N)
    d = y - mu
    var = _head_sum(d * d, bd) * (1.0 / RWKV_N)
    yn = d * lax.rsqrt(var + RWKV_GN_EPS) * gng_ref[...] + gnb_ref[...]
    bonus = _head_sum(_from_pair_tile(r_ref) * _from_pair_tile(k_ref) * rk_ref[...], bd) * _from_pair_tile(v_ref)
    yb = (yn + bonus) * _from_pair_tile(g_ref)
    merged = _sigmoid(ga_ref[...]) * _mm_w(yret_ref[...], wret) + _sigmoid(gb_ref[...]) * _mm_w(yb, wrw)
    out = _mm_w(merged, wo)
    x1 = _layer_norm(DN_ALPHA * x + out, ln_g_ref[...], ln_b_ref[...])
    x1_ref[...] = x1
    xb_ref[...] = x1.astype(BF16)

    logits = _mm(x1, wr_ref[...], 'ik,kj->ij', EXACT_PASSES) + br_ref[...]
    lane = lax.broadcasted_iota(jnp.int32, logits.shape, 1)
    work = logits
    ids, vals = [], []
    for _ in range(TOP_K):
        m = jnp.max(work, -1, keepdims=True)
        idx = jnp.min(jnp.where(work == m, lane, N_EXPERTS), -1, keepdims=True)
        ids.append(idx)
        vals.append(m)
        work = jnp.where(lane == idx, -jnp.inf, work)
    exps = [jnp.exp(m - vals[0]) for m in vals]
    inv_den = 1.0 / sum(exps)
    slot = lax.broadcasted_iota(jnp.int32, ids_ref.shape, 1)
    ids_out = jnp.zeros(ids_ref.shape, jnp.int32)
    probs_out = jnp.zeros(probs_ref.shape, F32)
    for j in range(TOP_K):
        ids_out = jnp.where(slot == j, ids[j], ids_out)
        probs_out = jnp.where(slot == j, exps[j] * inv_den, probs_out)
    ids_ref[...] = ids_out
    probs_ref[...] = probs_out


def _post(tokens, yrw, r, k, v, g, yret, gates, seq_len, p, l, passes):
    tm = TOKEN_TILE
    n = (tokens[3] + 1) * tm
    row = lambda a: a.reshape(1, -1)
    tile = lambda w, j=0: pl.BlockSpec((tm, w), lambda i: (i, j))
    full = lambda a: pl.BlockSpec(a.shape, lambda i: (0,) * a.ndim)
    pair = _pair_tile_spec(seq_len // tm, RWKV_W)
    weights = (_split_weight(p['w_ret_out'][l], passes) + _split_weight(p['w_rwkv_out'][l], passes)
               + _split_weight(p['w_o'][l], passes))
    consts = [_head_blockdiag(), row(p['rw_r_k'][l]), row(p['rw_gn_g'][l]), row(p['rw_gn_b'][l]),
              row(p['ln1_g'][l]), row(p['ln1_b'][l]), p['w_router'][l], row(p['b_router'][l]), *weights]
    in_specs = (_token_specs(tokens, tm, D_MODEL, lambda i: i) + [pair] * 5
                + [tile(RET_V_W), tile(D_MODEL, 0), tile(D_MODEL, 1)] + [full(a) for a in consts])
    return pl.pallas_call(
        functools.partial(_post_kernel, len(weights) // 3, (None, None) + tokens[2:]),
        grid=(n // tm,),
        in_specs=in_specs,
        out_specs=[tile(D_MODEL), tile(D_MODEL), tile(TOP_K), tile(TOP_K)],
        out_shape=[jax.ShapeDtypeStruct((n, D_MODEL), F32), jax.ShapeDtypeStruct((n, D_MODEL), BF16),
                   jax.ShapeDtypeStruct((n, TOP_K), jnp.int32), jax.ShapeDtypeStruct((n, TOP_K), F32)],
        compiler_params=_params("parallel"),
        name="merge_ln_router",
    )(tokens[0], tokens[1], yrw, r, k, v, g, yret, gates, gates, *consts)


SEG_ALIGN = 8
ROUTE_TILE = 512
LOCAL_ROWS = 2304
LOCAL_BLOCK = 768
PROB_LANES = 128
ROW_W = D_MODEL + PROB_LANES


def _start_segments(i, cnt_ref, lst_ref, gst_ref, make_copy):
    def per_expert(e, carry):
        j = i * N_EXPERTS + e
        rows = pl.multiple_of(cnt_ref[j], SEG_ALIGN)

        @pl.when(rows > 0)
        def _():
            make_copy(pl.multiple_of(lst_ref[j], SEG_ALIGN), pl.multiple_of(gst_ref[j], SEG_ALIGN), rows).start()

        return carry

    lax.fori_loop(0, N_EXPERTS, per_expert, 0)


def _wait_segments(i, tot_ref, make_copy):
    make_copy(0, 0, pl.multiple_of(tot_ref[i], SEG_ALIGN)).wait()


def _dispatch_kernel(cnt_ref, lst_ref, gst_ref, tot_ref, ids_ref, probs_ref, lstart_ref, xb_ref, xs_hbm, buf, sem):
    i = pl.program_id(0)
    tm = xb_ref.shape[0]
    ids = ids_ref[0]
    probs = probs_ref[0]
    expert = lax.broadcasted_iota(jnp.int32, (N_EXPERTS, tm), 0)
    picks = [ids[k:k + 1, :] == expert for k in range(TOP_K)]
    picked = sum(pk.astype(F32) for pk in picks)
    m = lax.broadcasted_iota(jnp.int32, (tm, tm), 0)
    n = lax.broadcasted_iota(jnp.int32, (tm, tm), 1)
    earlier = (m < n).astype(BF16)
    rank = jnp.dot(picked.astype(BF16), earlier, preferred_element_type=F32)
    pos = lstart_ref[0] + rank
    lpos = [jnp.sum(jnp.where(picks[k], pos, 0.0), 0, keepdims=True).astype(jnp.int32) for k in range(TOP_K)]

    def copies_of(slot):
        return lambda lo, go, size: pltpu.make_async_copy(buf.at[slot, pl.ds(lo, size)], xs_hbm.at[pl.ds(go, size)],
                                                          sem.at[slot])

    slot = i % 2

    @pl.when(i >= 2)
    def _():
        _wait_segments(i - 2, tot_ref, copies_of(slot))

    xb = xb_ref[...]
    for r0 in range(0, LOCAL_ROWS, LOCAL_BLOCK):
        row = r0 + lax.broadcasted_iota(jnp.int32, (LOCAL_BLOCK, tm), 0)
        perm = jnp.zeros((LOCAL_BLOCK, tm), F32)
        weight = jnp.zeros((LOCAL_BLOCK, tm), F32)
        for k in range(TOP_K):
            hit = row == lpos[k]
            perm = jnp.where(hit, 1.0, perm)
            weight = jnp.where(hit, probs[k:k + 1, :], weight)
        buf[slot, r0:r0 + LOCAL_BLOCK, :D_MODEL] = jnp.dot(perm.astype(BF16), xb, preferred_element_type=F32)
        buf[slot, r0:r0 + LOCAL_BLOCK, D_MODEL:] = jnp.broadcast_to(jnp.sum(weight, 1, keepdims=True),
                                                                      (LOCAL_BLOCK, PROB_LANES))
    _start_segments(i, cnt_ref, lst_ref, gst_ref, copies_of(slot))

    @pl.when(i == pl.num_programs(0) - 1)
    def _():
        @pl.when(i >= 1)
        def _():
            _wait_segments(i - 1, tot_ref, copies_of(1 - slot))

        _wait_segments(i, tot_ref, copies_of(slot))


def _round_up(x, m):
    return (x + m - 1) // m * m


def _routing_tables(ids):
    n = ids.shape[0]
    nt = n // ROUTE_TILE
    picked = jnp.sum(ids[:, :, None] == jnp.arange(N_EXPERTS, dtype=jnp.int32)[None, None, :], 1)
    cnt = jnp.sum(picked.reshape(nt, ROUTE_TILE, N_EXPERTS), 1).astype(jnp.int32)
    cnt = _round_up(cnt, SEG_ALIGN)
    lstart = jnp.cumsum(cnt, 1) - cnt
    per_expert = jnp.sum(cnt, 0)
    region = _round_up(per_expert, EXPERT_ROW_TILE)
    gstart = (jnp.cumsum(region) - region)[None, :] + jnp.cumsum(cnt, 0) - cnt
    return cnt, lstart.astype(jnp.int32), gstart.astype(jnp.int32), jnp.sum(cnt, 1), region


def _table_args(tables):
    cnt, lstart, gstart, tile_rows, _ = tables
    return cnt.reshape(-1), lstart.reshape(-1), gstart.reshape(-1), tile_rows


def _dispatch(xb, ids, probs, tables, n_rows):
    n = xb.shape[0]
    tm = ROUTE_TILE
    nt = n // tm
    lstart = tables[1]
    to_lanes = lambda a: a.reshape(nt, tm, TOP_K).transpose(0, 2, 1)
    grid_spec = pltpu.PrefetchScalarGridSpec(
        num_scalar_prefetch=4,
        grid=(nt,),
        in_specs=[pl.BlockSpec((1, TOP_K, tm), lambda i, *_: (i, 0, 0)), pl.BlockSpec((1, TOP_K, tm), lambda i, *_: (i, 0, 0)),
                  pl.BlockSpec((1, N_EXPERTS, 1), lambda i, *_: (i, 0, 0)), pl.BlockSpec((tm, D_MODEL), lambda i, *_: (i, 0))],
        out_specs=pl.BlockSpec(memory_space=pl.ANY),
        scratch_shapes=[pltpu.VMEM((2, LOCAL_ROWS, ROW_W), F32), pltpu.SemaphoreType.DMA((2,))],
    )
    return pl.pallas_call(
        _dispatch_kernel,
        grid_spec=grid_spec,
        out_shape=jax.ShapeDtypeStruct((n_rows, ROW_W), F32),
        compiler_params=_params("arbitrary"),
        name="dispatch",
    )(*_table_args(tables), to_lanes(ids), to_lanes(probs), lstart.astype(F32)[:, :, None], xb)


def _expert_kernel(layer, tile_ref, exp_ref, valid_ref, slot_ref, next_ref, x_ref, wgu_hbm, bgu_ref, wd_hbm, bd_ref,
                   o_ref, wgu_f, wd_f, wgu_s, wd_s, sem):
    i = pl.program_id(0)

    def weight_copies(e, slot):
        return (pltpu.make_async_copy(wgu_hbm.at[layer, e], wgu_f.at[slot], sem.at[0, slot]),
                pltpu.make_async_copy(wd_hbm.at[layer, e], wd_f.at[slot], sem.at[1, slot]))

    @pl.when(valid_ref[i] == 1)
    def _():
        e = exp_ref[i]
        slot = slot_ref[i]

        @pl.when(i == 0)
        def _():
            for c in weight_copies(e, slot):
                c.start()

        @pl.when((i == 0) | (exp_ref[jnp.maximum(i - 1, 0)] != e))
        def _():
            for c in weight_copies(e, slot):
                c.wait()
            wgu_s[...] = wgu_f[slot].astype(BF16)
            wd_s[...] = wd_f[slot].astype(BF16)

            @pl.when(next_ref[i] >= 0)
            def _():
                for c in weight_copies(next_ref[i], 1 - slot):
                    c.start()

        h = jnp.dot(x_ref[:, :D_MODEL].astype(BF16), wgu_s[...], preferred_element_type=F32) + bgu_ref[0, 0]
        gate = jnp.minimum(h[:, :D_FF], SWIGLU_LIMIT)
        up = jnp.clip(h[:, D_FF:], -SWIGLU_LIMIT, SWIGLU_LIMIT)
        act = gate * _sigmoid(SWIGLU_ALPHA * gate) * (up + 1.0)
        y = jnp.dot(act.astype(BF16), wd_s[...], preferred_element_type=F32) + bd_ref[0, 0]
        o_ref[...] = y * x_ref[:, D_MODEL:D_MODEL + 1]


def _expert_schedule(region, n_rows, tm):
    n_entries = n_rows // tm
    tile_end = jnp.cumsum(region // tm)
    total = tile_end[-1]
    t = jnp.minimum(jnp.arange(n_entries), total - 1).astype(jnp.int32)
    e = jnp.sum(tile_end[None, :] <= t[:, None], -1).astype(jnp.int32)
    valid = (jnp.arange(n_entries) < total).astype(jnp.int32)
    used = region > 0
    slot = (jnp.cumsum(used) - 1) % 2
    ids = jnp.where(used, jnp.arange(N_EXPERTS), N_EXPERTS)
    first_used_from = jnp.flip(lax.cummin(jnp.flip(ids)))
    nxt = jnp.concatenate([first_used_from[1:], jnp.full((1,), N_EXPERTS)])
    nxt = jnp.where(nxt == N_EXPERTS, -1, nxt)
    return t, e, valid, slot[e].astype(jnp.int32), nxt[e].astype(jnp.int32)


def _experts(xs, sched, p, l):
    n_rows = xs.shape[0]
    tm = EXPERT_ROW_TILE
    bgu = p['b_gate_up'].reshape(DEPTH, N_EXPERTS, 1, 2 * D_FF)
    bdn = p['b_down'].reshape(DEPTH, N_EXPERTS, 1, D_MODEL)
    by_tile = lambda w: pl.BlockSpec((tm, w), lambda i, t, e, *_: (t[i], 0))
    by_expert = lambda a, b: pl.BlockSpec((1, 1, a, b), lambda i, t, e, *_: (l, e[i], 0, 0))
    in_hbm = pl.BlockSpec(memory_space=pl.ANY)
    grid_spec = pltpu.PrefetchScalarGridSpec(
        num_scalar_prefetch=5,
        grid=(sched[0].shape[0],),
        in_specs=[by_tile(ROW_W), in_hbm, by_expert(1, 2 * D_FF), in_hbm, by_expert(1, D_MODEL)],
        out_specs=by_tile(D_MODEL),
        scratch_shapes=[pltpu.VMEM((2, D_MODEL, 2 * D_FF), F32), pltpu.VMEM((2, D_FF, D_MODEL), F32),
                        pltpu.VMEM((D_MODEL, 2 * D_FF), BF16), pltpu.VMEM((D_FF, D_MODEL), BF16),
                        pltpu.SemaphoreType.DMA((2, 2))],
    )
    return pl.pallas_call(
        functools.partial(_expert_kernel, l),
        grid_spec=grid_spec,
        out_shape=jax.ShapeDtypeStruct((n_rows, D_MODEL), F32),
        compiler_params=_params("arbitrary"),
        name="experts",
    )(*sched, xs, p['w_gate_up'], bgu, p['w_down'], bdn)


def _combine_kernel(tile0, cnt_ref, lst_ref, gst_ref, tot_ref, ids_ref, lstart_ref, x_ref, ln_g_ref, ln_b_ref, ys_hbm,
                    o_ref, buf, sem):
    step = pl.program_id(0)
    i = step + tile0
    tm = x_ref.shape[0]
    slot = step % 2

    def copies_of(slot):
        return lambda lo, go, size: pltpu.make_async_copy(ys_hbm.at[pl.ds(go, size)], buf.at[slot, pl.ds(lo, size)],
                                                          sem.at[slot])

    def fetch(tile, slot):
        buf[slot, ROUTE_TILE * TOP_K:, :] = jnp.zeros((LOCAL_ROWS - ROUTE_TILE * TOP_K, D_MODEL), F32)
        _start_segments(tile, cnt_ref, lst_ref, gst_ref, copies_of(slot))

    @pl.when(step == 0)
    def _():
        fetch(i, slot)

    @pl.when(step + 1 < pl.num_programs(0))
    def _():
        fetch(i + 1, 1 - slot)

    ids = ids_ref[...]
    expert = lax.broadcasted_iota(jnp.int32, (tm, N_EXPERTS), 1)
    picks = [ids[:, k:k + 1] == expert for k in range(TOP_K)]
    picked = sum(pk.astype(F32) for pk in picks)
    m = lax.broadcasted_iota(jnp.int32, (tm, tm), 0)
    n = lax.broadcasted_iota(jnp.int32, (tm, tm), 1)
    earlier = (n < m).astype(BF16)
    rank = jnp.dot(earlier, picked.astype(BF16), preferred_element_type=F32)
    pos = lstart_ref[0] + rank
    lpos = [jnp.sum(jnp.where(picks[k], pos, 0.0), 1, keepdims=True).astype(jnp.int32) for k in range(TOP_K)]

    _wait_segments(i, tot_ref, copies_of(slot))
    moe = jnp.zeros((tm, D_MODEL), F32)
    for r0 in range(0, LOCAL_ROWS, LOCAL_BLOCK):
        col = r0 + lax.broadcasted_iota(jnp.int32, (tm, LOCAL_BLOCK), 1)
        perm = jnp.zeros((tm, LOCAL_BLOCK), F32)
        for k in range(TOP_K):
            perm = jnp.where(col == lpos[k], 1.0, perm)
        moe = moe + jnp.dot(perm.astype(BF16), buf[slot, r0:r0 + LOCAL_BLOCK, :].astype(BF16),
                            preferred_element_type=F32)
    o_ref[...] = _layer_norm(DN_ALPHA * x_ref[...] + moe, ln_g_ref[...], ln_b_ref[...])


def _combine(x1, ys, ids, tables, ln_g, ln_b, tile0, n_tiles):
    tm = ROUTE_TILE
    lstart = tables[1]
    at = lambda i, *_: (i + tile0, 0)
    grid_spec = pltpu.PrefetchScalarGridSpec(
        num_scalar_prefetch=4,
        grid=(n_tiles,),
        in_specs=[pl.BlockSpec((tm, TOP_K), at), pl.BlockSpec((1, 1, N_EXPERTS), lambda i, *_: (i + tile0, 0, 0)),
                  pl.BlockSpec((tm, D_MODEL), at), pl.BlockSpec((1, D_MODEL), lambda i, *_: (0, 0)),
                  pl.BlockSpec((1, D_MODEL), lambda i, *_: (0, 0)), pl.BlockSpec(memory_space=pl.ANY)],
        out_specs=pl.BlockSpec((tm, D_MODEL), lambda i, *_: (i, 0)),
        scratch_shapes=[pltpu.VMEM((2, LOCAL_ROWS, D_MODEL), F32), pltpu.SemaphoreType.DMA((2,))],
    )
    return pl.pallas_call(
        functools.partial(_combine_kernel, tile0),
        grid_spec=grid_spec,
        out_shape=jax.ShapeDtypeStruct((n_tiles * tm, D_MODEL), F32),
        compiler_params=_params("arbitrary"),
        name="combine_ln",
    )(*_table_args(tables), ids, lstart.astype(F32)[:, None, :], x1, ln_g.reshape(1, -1), ln_b.reshape(1, -1), ys)


def _moe(x1, xb, ids, probs, p, l, parts):
    n = x1.shape[0]
    tm = EXPERT_ROW_TILE
    seg_rows = n * TOP_K + (n // ROUTE_TILE) * N_EXPERTS * (SEG_ALIGN - 1)
    n_rows = _round_up(seg_rows + N_EXPERTS * (tm - SEG_ALIGN), tm)
    tables = _routing_tables(ids)
    xs = _dispatch(xb, ids, probs, tables, n_rows)
    ys = _experts(xs, _expert_schedule(tables[4], n_rows, tm), p, l)
    return [_combine(x1, ys, ids, tables, p['ln2_g'][l], p['ln2_b'][l], *part) for part in parts]


def _pad_time(a, tp):
    return jnp.pad(a, ((0, 0), (0, tp - a.shape[1]), (0, 0)))


def kernel(x_prompt, x_sample, state_ret, state_rwkv, state_shift, w_in, ret_gn_g, ret_gn_b, w_ret_out, rw_mu, rw_w0, rw_w_up, rw_a0, rw_a_up, rw_g_up, rw_k_k, rw_k_a, rw_r_k, rw_gn_g, rw_gn_b, rw_v0, rw_vres_down, rw_vres_up, w_rwkv_out, w_o, ln1_g, ln1_b, w_router, b_router, w_gate_up, b_gate_up, w_down, b_down, ln2_g, ln2_b):
    p = dict(w_ret_out=w_ret_out, rw_mu=rw_mu, rw_w0=rw_w0, rw_w_up=rw_w_up, rw_a0=rw_a0, rw_a_up=rw_a_up,
             rw_g_up=rw_g_up, rw_k_k=rw_k_k, rw_k_a=rw_k_a, rw_r_k=rw_r_k, rw_gn_g=rw_gn_g, rw_gn_b=rw_gn_b,
             w_rwkv_out=w_rwkv_out, w_o=w_o, ln1_g=ln1_g, ln1_b=ln1_b, w_router=w_router, b_router=b_router,
             w_gate_up=w_gate_up, b_gate_up=b_gate_up, w_down=w_down, b_down=b_down, ln2_g=ln2_g, ln2_b=ln2_b)
    bp, tp, _ = x_prompt.shape
    bs, ts, _ = x_sample.shape
    np_, ns = bp * tp, bs * ts
    pos_p = jnp.arange(tp, dtype=F32)
    ts_ret = 8
    pos_s = PAST_LEN + jnp.arange(ts_ret, dtype=F32)

    assert ns == TOKEN_TILE and np_ % TOKEN_TILE == 0
    n_prompt_tiles = np_ // TOKEN_TILE
    tokens = (x_prompt.reshape(np_, D_MODEL), x_sample.transpose(1, 0, 2).reshape(ns, D_MODEL), 0, n_prompt_tiles)
    outs = {k: [] for k in ('ret_p', 'rw_p', 'sh_p', 'sh_s')}
    sret_s = srw_s = None
    v_first = None
    u_off = RET_W
    g_off = RET_W + SHIFT_W
    for l in range(DEPTH):
        passes = EXACT_PASSES if l == 0 else 1
        z_ret = _matmul(tokens, w_in[l][:, :u_off], TOKEN_TILE, RET_W if passes == 1 else RET_W // 2, passes)
        u = _matmul(tokens, w_in[l][:, u_off:g_off], TOKEN_TILE, SHIFT_W, passes)
        gates = _matmul(tokens, w_in[l][:, g_off:], TOKEN_TILE, D_MODEL, passes)

        zr_s = _pad_time(z_ret[np_:].reshape(ts, bs, RET_W).transpose(1, 0, 2), ts_ret)
        yret, sret_p = _retention(z_ret.reshape(-1, RET_CHUNK, RET_W), pos_p, tp, None, ret_gn_g[l], ret_gn_b[l], 1,
                                  passes, n_seq=bp)
        yret_s, sret_s = _retention(zr_s, pos_s, ts, (state_ret, l, sret_s), ret_gn_g[l], ret_gn_b[l], 8, passes)
        yret = yret.reshape(-1, RET_V_W).at[np_:].set(yret_s[:, :ts].transpose(1, 0, 2).reshape(ns, RET_V_W))

        vres = None if l == 0 else (rw_v0[l - 1], rw_vres_down[l - 1], rw_vres_up[l - 1])
        r, lw, k, v, a, b, g = _rwkv_pre(u, state_shift[l], np_, tp, p, l, v_first, vres)
        if l == 0:
            v_first = v
        yrw, srw_p = _rwkv_chunks(r, lw, k, v, a, b, passes)
        yrw, srw_s = _rwkv_steps(r, lw, k, v, a, b, (state_rwkv, l, srw_s), yrw)

        x1, xb, ids, probs = _post(tokens, yrw, r, k, v, g, yret, gates, tp, p, l, passes)
        if l + 1 < DEPTH:
            x, = _moe(x1, xb, ids, probs, p, l, [(0, n_prompt_tiles + 1)])
            tokens = (x, x, n_prompt_tiles, n_prompt_tiles)
        else:
            y_p, y_s = _moe(x1, xb, ids, probs, p, l, [(0, n_prompt_tiles), (n_prompt_tiles, 1)])

        outs['ret_p'].append(sret_p)
        outs['rw_p'].append(srw_p)
        outs['sh_p'].append(u[tp - 1:np_:tp])
        outs['sh_s'].append(u[np_ + ns - bs:])

    y_prompt = y_p.reshape(bp, tp, D_MODEL)
    y_sample = y_s.reshape(ts, bs, D_MODEL).transpose(1, 0, 2)
    st = {k: jnp.stack(v) for k, v in outs.items()}
    return (y_prompt, y_sample, st['ret_p'], st['rw_p'], st['sh_p'], sret_s, srw_s, st['sh_s'])
```

```python
import functools

import jax
import jax.numpy as jnp
from jax import lax
from jax.experimental import pallas as pl
from jax.experimental.pallas import tpu as pltpu

F32 = jnp.float32
BF16 = jnp.bfloat16
HI = lax.Precision.HIGHEST

D_MODEL = 1024
DEPTH = 2
PAST_LEN = 16384
RET_HEADS = 4
RET_DK = 128
RET_DV = 256
RET_QK_W = RET_HEADS * RET_DK
RET_V_W = RET_HEADS * RET_DV
RET_W = 2 * RET_QK_W + 2 * RET_V_W
RET_CHUNK = 128
ROPE_BASE = 10000.0
RWKV_HEADS = 8
RWKV_N = 64
RWKV_W = RWKV_HEADS * RWKV_N
LORA_W = 64
LORA_A = 64
LORA_G = 128
SHIFT_W = 3 * RWKV_W + LORA_W + LORA_A + LORA_G
RWKV_CHUNK = 64
N_EXPERTS = 32
TOP_K = 4
D_FF = D_MODEL
SWIGLU_LIMIT = 7.0
SWIGLU_ALPHA = 1.702
DN_ALPHA = (2 * DEPTH) ** 0.25
LN_EPS = 1e-5
RET_GN_EPS = 1e-5
RWKV_GN_EPS = 64e-5

VMEM_LIMIT = 56 * 1024 * 1024
TOKEN_TILE = 512
EXPERT_ROW_TILE = 512
RWKV_PAIRS_PER_STEP = 4
RWKV_STEP_BATCH_BLOCK = 8
EXACT_PASSES = 3
RWKV_REFINE_STEPS = 1


def _params(*sem):
    return pltpu.CompilerParams(dimension_semantics=sem, vmem_limit_bytes=VMEM_LIMIT)


def _split(x):
    hi = x.astype(BF16)
    lo = (x - hi.astype(F32)).astype(BF16)
    return hi, lo


def _split_kernel(w_ref, hi_ref, lo_ref):
    hi_ref[...], lo_ref[...] = _split(w_ref[...])


def _split_weight(w, passes):
    if passes == 1:
        return (w.astype(BF16),)
    rows = 256
    spec = pl.BlockSpec((rows, w.shape[1]), lambda i: (i, 0))
    return tuple(pl.pallas_call(
        _split_kernel,
        grid=(w.shape[0] // rows,),
        in_specs=[spec],
        out_specs=[spec, spec],
        out_shape=[jax.ShapeDtypeStruct(w.shape, BF16)] * 2,
        compiler_params=_params("parallel"),
        name="split_weight",
    )(w))


def _mm(a, b, spec, passes):
    dg = lambda x, y: jnp.einsum(spec, x, y, preferred_element_type=F32)
    if passes == 1:
        return dg(a.astype(BF16), b.astype(BF16))
    ah, al = _split(a)
    bh, bl = _split(b)
    return dg(ah, bh) + (dg(ah, bl) + dg(al, bh))


def _mm_w(a, w_refs):
    dg = lambda x, y: jnp.dot(x, y, preferred_element_type=F32)
    if len(w_refs) == 1:
        return dg(a.astype(BF16), w_refs[0][...])
    ah, al = _split(a)
    return dg(ah, w_refs[0][...]) + (dg(ah, w_refs[1][...]) + dg(al, w_refs[0][...]))


def _dot_hi(a, b):
    return jnp.dot(a, b, precision=HI, preferred_element_type=F32)


def _sigmoid(x):
    return 1.0 / (1.0 + jnp.exp(-x))


def _layer_norm(x, g, b):
    mu = jnp.mean(x, -1, keepdims=True)
    d = x - mu
    var = jnp.mean(d * d, -1, keepdims=True)
    return d * lax.rsqrt(var + LN_EPS) * g + b


def _token_specs(tokens, tm, w, tile_of):
    _, _, b_block, n_a = tokens
    return [pl.BlockSpec((tm, w), lambda *g: (jnp.minimum(tile_of(*g), n_a - 1), 0)),
            pl.BlockSpec((tm, w), lambda *g: (b_block, 0))]


def _token_tile(tokens, i, a_ref, b_ref):
    return jnp.where(i < tokens[3], a_ref[...], b_ref[...])


def _matmul_kernel(tokens, xa_ref, xb_ref, *refs):
    o_ref = refs[-1]
    o_ref[...] = _mm_w(_token_tile(tokens, pl.program_id(1), xa_ref, xb_ref), refs[:-1])


def _matmul(tokens, w, tm, tn, passes):
    k, n = w.shape
    n_tiles = tokens[3] + 1
    ws = _split_weight(w, passes)
    return pl.pallas_call(
        functools.partial(_matmul_kernel, (None, None) + tokens[2:]),
        grid=(n // tn, n_tiles),
        in_specs=_token_specs(tokens, tm, k, lambda j, i: i) + [pl.BlockSpec((k, tn), lambda j, i: (0, j))] * len(ws),
        out_specs=pl.BlockSpec((tm, tn), lambda j, i: (i, j)),
        out_shape=jax.ShapeDtypeStruct((n_tiles * tm, n), F32),
        compiler_params=_params("parallel", "parallel"),
        name="in_proj",
    )(tokens[0], tokens[1], *ws)


def _ret_kernel(has_state, passes, q_ref, k_ref, v_ref, g_ref, cos_ref, sin_ref, dm_ref, qd_ref, kd_ref, cd_ref,
                gng_ref, gnb_ref, *rest):
    if has_state:
        s0_ref, y_ref, so_ref, s_scr = rest[0], *rest[-3:]
    else:
        y_ref, so_ref, s_scr = rest
    c = pl.program_id(1)
    bb, cl, _ = q_ref.shape
    nh = RET_HEADS

    def heads(x, w):
        return jnp.concatenate([x[:, :, h * w:(h + 1) * w] for h in range(nh)], 0)

    def per_head(ref):
        return jnp.concatenate([jnp.broadcast_to(ref[h], (bb,) + ref.shape[1:]) for h in range(nh)], 0)

    @pl.when(c == 0)
    def _():
        if has_state:
            s_scr[...] = jnp.concatenate([s0_ref[:, h] for h in range(nh)], 0)
        else:
            s_scr[...] = jnp.zeros_like(s_scr)

    cos = cos_ref[...]
    sin = sin_ref[...]
    mm = functools.partial(_mm, passes=passes)

    def rope(x):
        x2 = x.reshape(nh * bb * cl, RET_DK)
        rot = pltpu.roll(x2, RET_DK // 2, axis=1).reshape(nh * bb, cl, RET_DK)
        return x * cos + rot * sin

    q = rope(heads(q_ref[...], RET_DK))
    k = rope(heads(k_ref[...], RET_DK)) * (RET_DK ** -0.5)
    v = heads(v_ref[...], RET_DV)
    s = s_scr[...]
    sc = mm(q, k, 'bid,bjd->bij') * per_head(dm_ref)
    intra = mm(sc, v, 'bij,bje->bie')
    cross = mm(q, s, 'bid,bde->bie') * per_head(qd_ref)
    s_new = s * per_head(cd_ref) + mm(k * per_head(kd_ref), v, 'bjd,bje->bde')
    s_scr[...] = s_new

    y = intra + cross
    mu = jnp.mean(y, -1, keepdims=True)
    d = y - mu
    var = jnp.mean(d * d, -1, keepdims=True)
    yn = d * lax.rsqrt(var + RET_GN_EPS)
    for h in range(nh):
        cols = slice(h * RET_DV, (h + 1) * RET_DV)
        rg = g_ref[:, :, cols]
        y_ref[:, :, cols] = (yn[h * bb:(h + 1) * bb] * gng_ref[:, cols] + gnb_ref[:, cols]) * (rg * _sigmoid(rg))

    @pl.when(c == pl.num_programs(1) - 1)
    def _():
        for h in range(nh):
            so_ref[:, h] = s_new[h * bb:(h + 1) * bb]


def _retention(z, pos, t_real, s0, gn_g, gn_b, bb, passes, n_seq=None):
    flat = n_seq is not None
    tp = pos.shape[0]
    b = n_seq if flat else z.shape[0]
    cl = RET_CHUNK if t_real % RET_CHUNK == 0 else tp
    cr = min(cl, t_real)
    nc = tp // cl
    at = (lambda bi, c: (bi * nc + c, 0)) if flat else (lambda bi, c: (bi, c))
    half = RET_DK // 2
    inv = ROPE_BASE ** (-jnp.arange(half, dtype=F32) / half)
    ang = pos[:, None] * inv[None, :]
    cos = jnp.concatenate([jnp.cos(ang), jnp.cos(ang)], -1)
    sin = jnp.concatenate([-jnp.sin(ang), jnp.sin(ang)], -1)
    lg = jnp.log1p(-jnp.exp2(-5.0 - jnp.arange(RET_HEADS, dtype=F32)))
    i = jnp.arange(cl, dtype=F32)
    real = i < cr
    diff = i[:, None] - i[None, :]
    ok = (diff >= 0) & real[:, None] & real[None, :]
    dmask = jnp.exp(jnp.where(ok[None], diff[None] * lg[:, None, None], -jnp.inf))
    q_dec = jnp.exp((i[None, :] + 1.0) * lg[:, None])[..., None]
    k_dec = jnp.where(real[None, :], jnp.exp((cr - 1.0 - i)[None, :] * lg[:, None]), 0.0)[..., None]
    c_dec = jnp.exp(cr * lg)[:, None, None]

    has_state = s0 is not None
    cols = lambda w, j: pl.BlockSpec((bb, cl, w), lambda bi, c: at(bi, c) + (j,))
    full = lambda a: pl.BlockSpec(a.shape, lambda bi, c: (0,) * a.ndim)
    consts = [dmask, q_dec, k_dec, c_dec, gn_g.reshape(1, -1), gn_b.reshape(1, -1)]
    in_specs = [cols(RET_QK_W, 0), cols(RET_QK_W, 1), cols(RET_V_W, 1), cols(RET_V_W, 2),
                pl.BlockSpec((cl, RET_DK), lambda bi, c: (c, 0)), pl.BlockSpec((cl, RET_DK), lambda bi, c: (c, 0))]
    in_specs += [full(a) for a in consts]
    args = [z, z, z, z, cos, sin] + consts
    aliases = {}
    if has_state:
        all_states, layer, earlier = s0
        s_spec = pl.BlockSpec((None, bb, RET_HEADS, RET_DK, RET_DV), lambda bi, c: (layer, bi, 0, 0, 0))
        s_shape = all_states.shape
        in_specs.append(s_spec)
        args.append(all_states)
        if earlier is not None:
            aliases = {len(args): 1}
            in_specs.append(pl.BlockSpec(memory_space=pl.ANY))
            args.append(earlier)
    else:
        s_spec = pl.BlockSpec((bb, RET_HEADS, RET_DK, RET_DV), lambda bi, c: (bi, 0, 0, 0))
        s_shape = (b, RET_HEADS, RET_DK, RET_DV)
    return pl.pallas_call(
        functools.partial(_ret_kernel, has_state, passes),
        grid=(b // bb, nc),
        in_specs=in_specs,
        out_specs=[cols(RET_V_W, 0), s_spec],
        out_shape=[jax.ShapeDtypeStruct(z.shape[:2] + (RET_V_W,), F32), jax.ShapeDtypeStruct(s_shape, F32)],
        scratch_shapes=[pltpu.VMEM((RET_HEADS * bb, RET_DK, RET_DV), F32)],
        input_output_aliases=aliases,
        compiler_params=_params("parallel", "arbitrary"),
        name="retention",
    )(*args)


def _head_sum(x, bd):
    hi, lo = _split(x)
    return jnp.dot(hi, bd, preferred_element_type=F32) + jnp.dot(lo, bd, preferred_element_type=F32)


def _pair_shape(n_pairs, chunks, w):
    return (n_pairs + 1, chunks, 2, RWKV_CHUNK, w)


def _pair_tile_spec(tiles_per_seq, w):
    rows = TOKEN_TILE // RWKV_CHUNK
    return pl.BlockSpec((1, rows, 1, RWKV_CHUNK, w),
                        lambda i: (i // (2 * tiles_per_seq), i % tiles_per_seq, (i // tiles_per_seq) % 2, 0, 0))


def _to_pair_tile(ref, x):
    ref[0, :, 0] = x.reshape(TOKEN_TILE // RWKV_CHUNK, RWKV_CHUNK, x.shape[-1])


def _from_pair_tile(ref):
    return ref[0, :, 0].reshape(TOKEN_TILE, ref.shape[-1])


def _rwkv_pre_kernel(has_vres, n_prompt_tiles, tiles_per_seq, u_ref, tail_ref, shift_ref, mu_ref, w0_ref, wup_ref,
                     a0_ref, aup_ref, gup_ref, kk_ref, ka_ref, bd_ref, *rest):
    if has_vres:
        vf_ref, v0_ref, vd_ref, vu_ref, r_o, lw_o, k_o, v_o, a_o, b_o, g_o = rest
    else:
        r_o, lw_o, k_o, v_o, a_o, b_o, g_o = rest
    i = pl.program_id(0)
    u = u_ref[...]
    tm = u.shape[0]
    row = lax.broadcasted_iota(jnp.int32, (tm, 1), 0)
    before = jnp.where(i % tiles_per_seq == 0, 0.0, tail_ref[tail_ref.shape[0] - 1:, :])
    prev_prompt = jnp.where(row == 0, before, pltpu.roll(u, 1, axis=0))
    n_seq = shift_ref.shape[0]
    prev_sample = jnp.concatenate([shift_ref[...], u[:tm - n_seq]], 0)
    prev = jnp.where(i < n_prompt_tiles, prev_prompt, prev_sample)
    um = u + (prev - u) * mu_ref[...]
    w1, w2, w3 = RWKV_W, 2 * RWKV_W, 3 * RWKV_W
    r = um[:, :w1]
    kw = um[:, w1:w2]
    vw = um[:, w2:w3]
    wd = um[:, w3:w3 + LORA_W]
    ad = um[:, w3 + LORA_W:w3 + LORA_W + LORA_A]
    gd = um[:, w3 + LORA_W + LORA_A:]
    lora = lambda x, w_ref: _mm(x, w_ref[...], 'ik,kj->ij', EXACT_PASSES)
    xw = w0_ref[...] + lora(jnp.tanh(wd), wup_ref)
    softplus = jnp.maximum(-xw, 0.0) + jnp.log1p(jnp.exp(-jnp.abs(xw)))
    _to_pair_tile(lw_o, -jnp.exp(-softplus - 0.5))
    a = _sigmoid(a0_ref[...] + lora(ad, aup_ref))
    _to_pair_tile(g_o, lora(_sigmoid(gd), gup_ref))
    if has_vres:
        gate = _sigmoid(v0_ref[...] + lora(lora(vw, vd_ref), vu_ref))
        vw = vw + (_from_pair_tile(vf_ref) - vw) * gate
    kk = kw * kk_ref[...]
    norm = jnp.sqrt(_head_sum(kk * kk, bd_ref[...]))
    kk = kk / jnp.maximum(norm, 1e-12)
    _to_pair_tile(r_o, r)
    _to_pair_tile(k_o, kw * (1.0 + (a - 1.0) * ka_ref[...]))
    _to_pair_tile(v_o, vw)
    _to_pair_tile(a_o, -kk)
    _to_pair_tile(b_o, kk * a)


def _head_blockdiag():
    h = jnp.arange(RWKV_W) // RWKV_N
    return (h[:, None] == h[None, :]).astype(BF16)


def _rwkv_pre(u, shift_state, n_prompt, seq_len, p, l, v_first, vres):
    n = u.shape[0]
    tm = TOKEN_TILE
    assert n == n_prompt + tm and seq_len % tm == 0
    tiles_per_seq = seq_len // tm
    tail_rows = 8
    row = lambda a: a.reshape(1, -1)
    full = lambda a: pl.BlockSpec(a.shape, lambda i: (0,) * a.ndim)
    pair = _pair_tile_spec(tiles_per_seq, RWKV_W)
    has_vres = vres is not None
    args = [u, u, shift_state, row(p['rw_mu'][l]), row(p['rw_w0'][l]), p['rw_w_up'][l], row(p['rw_a0'][l]),
            p['rw_a_up'][l], p['rw_g_up'][l], row(p['rw_k_k'][l]), row(p['rw_k_a'][l]), _head_blockdiag()]
    in_specs = [pl.BlockSpec((tm, SHIFT_W), lambda i: (i, 0)),
                pl.BlockSpec((tail_rows, SHIFT_W), lambda i: (jnp.maximum(i * (tm // tail_rows) - 1, 0), 0))]
    in_specs += [full(a) for a in args[2:]]
    if has_vres:
        extra = [v_first, row(vres[0]), vres[1], vres[2]]
        in_specs += [pair] + [full(a) for a in extra[1:]]
        args += extra
    shape = _pair_shape(n_prompt // seq_len // 2, seq_len // RWKV_CHUNK, RWKV_W)
    return pl.pallas_call(
        functools.partial(_rwkv_pre_kernel, has_vres, n_prompt // tm, tiles_per_seq),
        grid=(n // tm,),
        in_specs=in_specs,
        out_specs=[pair] * 7,
        out_shape=[jax.ShapeDtypeStruct(shape, F32)] * 7,
        compiler_params=_params("parallel"),
        name="rwkv_pre",
    )(*args)


def _rwkv_chunk_kernel(passes, r_ref, lw_ref, k_ref, v_ref, a_ref, b_ref, tri_ref, y_ref, so_ref, s_scr):
    c = pl.program_id(1)

    @pl.when(c == 0)
    def _():
        s_scr[...] = jnp.zeros_like(s_scr)

    n_pairs, _, per_pair, cl, _ = r_ref.shape
    bb = n_pairs * per_pair
    r_all, lw, k_all, v_all, a_all, b_all = (ref[:, 0].reshape(bb, cl, RWKV_W) for ref in
                                             (r_ref, lw_ref, k_ref, v_ref, a_ref, b_ref))
    mm = functools.partial(_mm, passes=passes)
    ti = lax.broadcasted_iota(jnp.int32, (cl, cl), 0)
    si = lax.broadcasted_iota(jnp.int32, (cl, cl), 1)
    strict = (ti > si).astype(F32)
    incl = (ti >= si).astype(F32)
    eye = (ti == si).astype(F32)

    def heads(x):
        return jnp.stack([x[bi][:, h * RWKV_N:(h + 1) * RWKV_N] for bi in range(bb) for h in range(RWKV_HEADS)])

    cum = jnp.stack([_dot_hi(tri_ref[...], lw[bi]) for bi in range(bb)])
    last = cum[:, cl - 1:cl, :]
    e_neg = jnp.exp(-cum)
    e_end = jnp.exp(last - cum)
    at = heads(a_all * jnp.exp(cum - lw))
    rt = heads(r_all * jnp.exp(cum))
    bt = heads(b_all * e_neg)
    kt = heads(k_all * e_neg)
    bw = heads(b_all * e_end)
    kw = heads(k_all * e_end)
    wc = heads(jnp.exp(last))
    vh = heads(v_all)

    lhs = jnp.concatenate([at, rt], 1)
    gram = mm(lhs, jnp.concatenate([bt, kt], 1), 'gik,gjk->gij')
    a_ab = gram[:, :cl, :cl] * strict
    a_ak = gram[:, :cl, cl:] * strict
    a_rb = gram[:, cl:, :cl] * incl
    a_rk = gram[:, cl:, cl:] * incl
    refine = RWKV_REFINE_STEPS if passes > 1 else 0
    mm_inv = functools.partial(_mm, passes=1) if refine else mm
    inv = eye + a_ab
    pw = a_ab
    for _ in range(cl.bit_length() - 2):
        pw = mm_inv(pw, pw, 'gij,gjk->gik')
        inv = inv + mm_inv(inv, pw, 'gij,gjk->gik')
    s0 = s_scr[...].reshape(bb * RWKV_HEADS, RWKV_N, RWKV_N)
    xs = mm(lhs, s0, 'gtj,gij->gti')
    av = mm(jnp.concatenate([a_ak, a_rk], 1), vh, 'gts,gsi->gti')
    rhs = xs[:, :cl] + av[:, :cl]
    u = mm_inv(inv, rhs, 'gts,gsi->gti')
    for _ in range(refine):
        u = u + mm_inv(inv, rhs - u + mm(a_ab, u, 'gts,gsi->gti'), 'gts,gsi->gti')
    y = xs[:, cl:] + av[:, cl:] + mm(a_rb, u, 'gts,gsi->gti')
    s_new = s0 * wc + mm(jnp.concatenate([u, vh], 1), jnp.concatenate([bw, kw], 1), 'gti,gtj->gij')
    s_scr[...] = s_new.reshape(bb, RWKV_HEADS, RWKV_N, RWKV_N)
    for bi in range(bb):
        for h in range(RWKV_HEADS):
            y_ref[bi // per_pair, 0, bi % per_pair, :, h * RWKV_N:(h + 1) * RWKV_N] = y[bi * RWKV_HEADS + h]

    @pl.when(c == pl.num_programs(1) - 1)
    def _():
        so_ref[...] = s_scr[...]


def _rwkv_chunks(r, lw, k, v, a, b, passes):
    n_pairs, chunks, per_pair, cl, _ = r.shape
    n_pairs -= 1
    pp = RWKV_PAIRS_PER_STEP if n_pairs % RWKV_PAIRS_PER_STEP == 0 else 1
    bb = pp * per_pair
    seq = pl.BlockSpec((pp, 1, per_pair, cl, RWKV_W), lambda pi, c: (pi, c, 0, 0, 0))
    s_spec = pl.BlockSpec((bb, RWKV_HEADS, RWKV_N, RWKV_N), lambda pi, c: (pi, 0, 0, 0))
    tri = (jnp.arange(cl)[:, None] >= jnp.arange(cl)[None, :]).astype(F32)
    return pl.pallas_call(
        functools.partial(_rwkv_chunk_kernel, passes),
        grid=(n_pairs // pp, chunks),
        in_specs=[seq] * 6 + [pl.BlockSpec((cl, cl), lambda pi, c: (0, 0))],
        out_specs=[seq, s_spec],
        out_shape=[jax.ShapeDtypeStruct(r.shape, F32),
                   jax.ShapeDtypeStruct((n_pairs * per_pair, RWKV_HEADS, RWKV_N, RWKV_N), F32)],
        scratch_shapes=[pltpu.VMEM((bb, RWKV_HEADS, RWKV_N, RWKV_N), F32)],
        compiler_params=_params("parallel", "arbitrary"),
        name="rwkv_chunks",
    )(r, lw, k, v, a, b, tri)


def _rwkv_step_kernel(r_ref, lw_ref, k_ref, a_ref, b_ref, vt_ref, s0_ref, *rest):
    yt_ref, so_ref = rest[-2:]
    s = s0_ref[...]
    for t in range(r_ref.shape[2]):
        row = lambda ref: ref[:, :, t:t + 1, :]
        sa = jnp.sum(s * row(a_ref), -1, keepdims=True)
        s = s * jnp.exp(row(lw_ref)) + sa * row(b_ref) + vt_ref[:, :, :, t:t + 1] * row(k_ref)
        yt_ref[:, :, :, t:t + 1] = jnp.sum(s * row(r_ref), -1, keepdims=True)
    so_ref[...] = s


def _rwkv_steps(r, lw, k, v, a, b, s0, y_pairs):
    all_states, layer, earlier = s0
    bsz = all_states.shape[1]
    t = TOKEN_TILE // bsz
    bb = RWKV_STEP_BATCH_BLOCK
    tile_rows = TOKEN_TILE // RWKV_CHUNK
    steps = lambda x: x[-1, :tile_rows, 0].reshape(t, bsz, RWKV_HEADS, RWKV_N)
    rows = lambda x: steps(x).transpose(1, 2, 0, 3)
    vt = steps(v).transpose(1, 2, 3, 0)
    row_spec = pl.BlockSpec((bb, RWKV_HEADS, t, RWKV_N), lambda i: (i, 0, 0, 0))
    col_spec = pl.BlockSpec((bb, RWKV_HEADS, RWKV_N, t), lambda i: (i, 0, 0, 0))
    s_spec = pl.BlockSpec((None, bb, RWKV_HEADS, RWKV_N, RWKV_N), lambda i: (layer, i, 0, 0, 0))
    args = [rows(r), rows(lw), rows(k), rows(a), rows(b), vt, all_states]
    in_specs = [row_spec] * 5 + [col_spec, s_spec]
    aliases = {}
    if earlier is not None:
        aliases = {len(args): 1}
        in_specs.append(pl.BlockSpec(memory_space=pl.ANY))
        args.append(earlier)
    yt, s_new = pl.pallas_call(
        _rwkv_step_kernel,
        grid=(bsz // bb,),
        in_specs=in_specs,
        out_specs=[col_spec, s_spec],
        out_shape=[jax.ShapeDtypeStruct((bsz, RWKV_HEADS, RWKV_N, t), F32),
                   jax.ShapeDtypeStruct(all_states.shape, F32)],
        input_output_aliases=aliases,
        compiler_params=_params("parallel"),
        name="rwkv_steps",
    )(*args)
    y_tile = yt.transpose(3, 0, 1, 2).reshape(tile_rows, RWKV_CHUNK, RWKV_W)
    return y_pairs.at[-1, :tile_rows, 0].set(y_tile), s_new


def _post_kernel(n_w, tokens, xa_ref, xb_in_ref, yrw_ref, r_ref, k_ref, v_ref, g_ref, yret_ref, ga_ref, gb_ref, bd_ref,
                 rk_ref, gng_ref, gnb_ref, ln_g_ref, ln_b_ref, wr_ref, br_ref, *rest):
    wret, wrw, wo = rest[:n_w], rest[n_w:2 * n_w], rest[2 * n_w:3 * n_w]
    x1_ref, xb_ref, ids_ref, probs_ref = rest[3 * n_w:]
    x = _token_tile(tokens, pl.program_id(0), xa_ref, xb_in_ref)
    bd = bd_ref[...]
    y = _from_pair_tile(yrw_ref)
    mu = _head_sum(y, bd) * (1.0 / RWKV_N)
    d = y - mu
    var = _head_sum(d * d, bd) * (1.0 / RWKV_N)
    yn = d * lax.rsqrt(var + RWKV_GN_EPS) * gng_ref[...] + gnb_ref[...]
    bonus = _head_sum(_from_pair_tile(r_ref) * _from_pair_tile(k_ref) * rk_ref[...], bd) * _from_pair_tile(v_ref)
    yb = (yn + bonus) * _from_pair_tile(g_ref)
    merged = _sigmoid(ga_ref[...]) * _mm_w(yret_ref[...], wret) + _sigmoid(gb_ref[...]) * _mm_w(yb, wrw)
    out = _mm_w(merged, wo)
    x1 = _layer_norm(DN_ALPHA * x + out, ln_g_ref[...], ln_b_ref[...])
    x1_ref[...] = x1
    xb_ref[...] = x1.astype(BF16)

    logits = _mm(x1, wr_ref[...], 'ik,kj->ij', EXACT_PASSES) + br_ref[...]
    lane = lax.broadcasted_iota(jnp.int32, logits.shape, 1)
    work = logits
    ids, vals = [], []
    for _ in range(TOP_K):
        m = jnp.max(work, -1, keepdims=True)
        idx = jnp.min(jnp.where(work == m, lane, N_EXPERTS), -1, keepdims=True)
        ids.append(idx)
        vals.append(m)
        work = jnp.where(lane == idx, -jnp.inf, work)
    exps = [jnp.exp(m - vals[0]) for m in vals]
    inv_den = 1.0 / sum(exps)
    slot = lax.broadcasted_iota(jnp.int32, ids_ref.shape, 1)
    ids_out = jnp.zeros(ids_ref.shape, jnp.int32)
    probs_out = jnp.zeros(probs_ref.shape, F32)
    for j in range(TOP_K):
        ids_out = jnp.where(slot == j, ids[j], ids_out)
        probs_out = jnp.where(slot == j, exps[j] * inv_den, probs_out)
    ids_ref[...] = ids_out
    probs_ref[...] = probs_out


def _post(tokens, yrw, r, k, v, g, yret, gates, seq_len, p, l, passes):
    tm = TOKEN_TILE
    n = (tokens[3] + 1) * tm
    row = lambda a: a.reshape(1, -1)
    tile = lambda w, j=0: pl.BlockSpec((tm, w), lambda i: (i, j))
    full = lambda a: pl.BlockSpec(a.shape, lambda i: (0,) * a.ndim)
    pair = _pair_tile_spec(seq_len // tm, RWKV_W)
    weights = (_split_weight(p['w_ret_out'][l], passes) + _split_weight(p['w_rwkv_out'][l], passes)
               + _split_weight(p['w_o'][l], passes))
    consts = [_head_blockdiag(), row(p['rw_r_k'][l]), row(p['rw_gn_g'][l]), row(p['rw_gn_b'][l]),
              row(p['ln1_g'][l]), row(p['ln1_b'][l]), p['w_router'][l], row(p['b_router'][l]), *weights]
    in_specs = (_token_specs(tokens, tm, D_MODEL, lambda i: i) + [pair] * 5
                + [tile(RET_V_W), tile(D_MODEL, 0), tile(D_MODEL, 1)] + [full(a) for a in consts])
    return pl.pallas_call(
        functools.partial(_post_kernel, len(weights) // 3, (None, None) + tokens[2:]),
        grid=(n // tm,),
        in_specs=in_specs,
        out_specs=[tile(D_MODEL), tile(D_MODEL), tile(TOP_K), tile(TOP_K)],
        out_shape=[jax.ShapeDtypeStruct((n, D_MODEL), F32), jax.ShapeDtypeStruct((n, D_MODEL), BF16),
                   jax.ShapeDtypeStruct((n, TOP_K), jnp.int32), jax.ShapeDtypeStruct((n, TOP_K), F32)],
        compiler_params=_params("parallel"),
        name="merge_ln_router",
    )(tokens[0], tokens[1], yrw, r, k, v, g, yret, gates, gates, *consts)


SEG_ALIGN = 8
ROUTE_TILE = 512
LOCAL_ROWS = 2304
LOCAL_BLOCK = 768
PROB_LANES = 128
ROW_W = D_MODEL + PROB_LANES


def _start_segments(i, cnt_ref, lst_ref, gst_ref, make_copy):
    def per_expert(e, carry):
        j = i * N_EXPERTS + e
        rows = pl.multiple_of(cnt_ref[j], SEG_ALIGN)

        @pl.when(rows > 0)
        def _():
            make_copy(pl.multiple_of(lst_ref[j], SEG_ALIGN), pl.multiple_of(gst_ref[j], SEG_ALIGN), rows).start()

        return carry

    lax.fori_loop(0, N_EXPERTS, per_expert, 0)


def _wait_segments(i, tot_ref, make_copy):
    make_copy(0, 0, pl.multiple_of(tot_ref[i], SEG_ALIGN)).wait()


def _dispatch_kernel(cnt_ref, lst_ref, gst_ref, tot_ref, ids_ref, probs_ref, lstart_ref, xb_ref, xs_hbm, buf, sem):
    i = pl.program_id(0)
    tm = xb_ref.shape[0]
    ids = ids_ref[0]
    probs = probs_ref[0]
    expert = lax.broadcasted_iota(jnp.int32, (N_EXPERTS, tm), 0)
    picks = [ids[k:k + 1, :] == expert for k in range(TOP_K)]
    picked = sum(pk.astype(F32) for pk in picks)
    m = lax.broadcasted_iota(jnp.int32, (tm, tm), 0)
    n = lax.broadcasted_iota(jnp.int32, (tm, tm), 1)
    earlier = (m < n).astype(BF16)
    rank = jnp.dot(picked.astype(BF16), earlier, preferred_element_type=F32)
    pos = lstart_ref[0] + rank
    lpos = [jnp.sum(jnp.where(picks[k], pos, 0.0), 0, keepdims=True).astype(jnp.int32) for k in range(TOP_K)]

    def copies_of(slot):
        return lambda lo, go, size: pltpu.make_async_copy(buf.at[slot, pl.ds(lo, size)], xs_hbm.at[pl.ds(go, size)],
                                                          sem.at[slot])

    slot = i % 2

    @pl.when(i >= 2)
    def _():
        _wait_segments(i - 2, tot_ref, copies_of(slot))

    xb = xb_ref[...]
    for r0 in range(0, LOCAL_ROWS, LOCAL_BLOCK):
        row = r0 + lax.broadcasted_iota(jnp.int32, (LOCAL_BLOCK, tm), 0)
        perm = jnp.zeros((LOCAL_BLOCK, tm), F32)
        weight = jnp.zeros((LOCAL_BLOCK, tm), F32)
        for k in range(TOP_K):
            hit = row == lpos[k]
            perm = jnp.where(hit, 1.0, perm)
            weight = jnp.where(hit, probs[k:k + 1, :], weight)
        buf[slot, r0:r0 + LOCAL_BLOCK, :D_MODEL] = jnp.dot(perm.astype(BF16), xb, preferred_element_type=F32)
        buf[slot, r0:r0 + LOCAL_BLOCK, D_MODEL:] = jnp.broadcast_to(jnp.sum(weight, 1, keepdims=True),
                                                                      (LOCAL_BLOCK, PROB_LANES))
    _start_segments(i, cnt_ref, lst_ref, gst_ref, copies_of(slot))

    @pl.when(i == pl.num_programs(0) - 1)
    def _():
        @pl.when(i >= 1)
        def _():
            _wait_segments(i - 1, tot_ref, copies_of(1 - slot))

        _wait_segments(i, tot_ref, copies_of(slot))


def _round_up(x, m):
    return (x + m - 1) // m * m


def _routing_tables(ids):
    n = ids.shape[0]
    nt = n // ROUTE_TILE
    picked = jnp.sum(ids[:, :, None] == jnp.arange(N_EXPERTS, dtype=jnp.int32)[None, None, :], 1)
    cnt = jnp.sum(picked.reshape(nt, ROUTE_TILE, N_EXPERTS), 1).astype(jnp.int32)
    cnt = _round_up(cnt, SEG_ALIGN)
    lstart = jnp.cumsum(cnt, 1) - cnt
    per_expert = jnp.sum(cnt, 0)
    region = _round_up(per_expert, EXPERT_ROW_TILE)
    gstart = (jnp.cumsum(region) - region)[None, :] + jnp.cumsum(cnt, 0) - cnt
    return cnt, lstart.astype(jnp.int32), gstart.astype(jnp.int32), jnp.sum(cnt, 1), region


def _table_args(tables):
    cnt, lstart, gstart, tile_rows, _ = tables
    return cnt.reshape(-1), lstart.reshape(-1), gstart.reshape(-1), tile_rows


def _dispatch(xb, ids, probs, tables, n_rows):
    n = xb.shape[0]
    tm = ROUTE_TILE
    nt = n // tm
    lstart = tables[1]
    to_lanes = lambda a: a.reshape(nt, tm, TOP_K).transpose(0, 2, 1)
    grid_spec = pltpu.PrefetchScalarGridSpec(
        num_scalar_prefetch=4,
        grid=(nt,),
        in_specs=[pl.BlockSpec((1, TOP_K, tm), lambda i, *_: (i, 0, 0)), pl.BlockSpec((1, TOP_K, tm), lambda i, *_: (i, 0, 0)),
                  pl.BlockSpec((1, N_EXPERTS, 1), lambda i, *_: (i, 0, 0)), pl.BlockSpec((tm, D_MODEL), lambda i, *_: (i, 0))],
        out_specs=pl.BlockSpec(memory_space=pl.ANY),
        scratch_shapes=[pltpu.VMEM((2, LOCAL_ROWS, ROW_W), F32), pltpu.SemaphoreType.DMA((2,))],
    )
    return pl.pallas_call(
        _dispatch_kernel,
        grid_spec=grid_spec,
        out_shape=jax.ShapeDtypeStruct((n_rows, ROW_W), F32),
        compiler_params=_params("arbitrary"),
        name="dispatch",
    )(*_table_args(tables), to_lanes(ids), to_lanes(probs), lstart.astype(F32)[:, :, None], xb)


def _expert_kernel(layer, tile_ref, exp_ref, valid_ref, slot_ref, next_ref, x_ref, wgu_hbm, bgu_ref, wd_hbm, bd_ref,
                   o_ref, wgu_f, wd_f, wgu_s, wd_s, sem):
    i = pl.program_id(0)

    def weight_copies(e, slot):
        return (pltpu.make_async_copy(wgu_hbm.at[layer, e], wgu_f.at[slot], sem.at[0, slot]),
                pltpu.make_async_copy(wd_hbm.at[layer, e], wd_f.at[slot], sem.at[1, slot]))

    @pl.when(valid_ref[i] == 1)
    def _():
        e = exp_ref[i]
        slot = slot_ref[i]

        @pl.when(i == 0)
        def _():
            for c in weight_copies(e, slot):
                c.start()

        @pl.when((i == 0) | (exp_ref[jnp.maximum(i - 1, 0)] != e))
        def _():
            for c in weight_copies(e, slot):
                c.wait()
            wgu_s[...] = wgu_f[slot].astype(BF16)
            wd_s[...] = wd_f[slot].astype(BF16)

            @pl.when(next_ref[i] >= 0)
            def _():
                for c in weight_copies(next_ref[i], 1 - slot):
                    c.start()

        h = jnp.dot(x_ref[:, :D_MODEL].astype(BF16), wgu_s[...], preferred_element_type=F32) + bgu_ref[0, 0]
        gate = jnp.minimum(h[:, :D_FF], SWIGLU_LIMIT)
        up = jnp.clip(h[:, D_FF:], -SWIGLU_LIMIT, SWIGLU_LIMIT)
        act = gate * _sigmoid(SWIGLU_ALPHA * gate) * (up + 1.0)
        y = jnp.dot(act.astype(BF16), wd_s[...], preferred_element_type=F32) + bd_ref[0, 0]
        o_ref[...] = y * x_ref[:, D_MODEL:D_MODEL + 1]


def _expert_schedule(region, n_rows, tm):
    n_entries = n_rows // tm
    tile_end = jnp.cumsum(region // tm)
    total = tile_end[-1]
    t = jnp.minimum(jnp.arange(n_entries), total - 1).astype(jnp.int32)
    e = jnp.sum(tile_end[None, :] <= t[:, None], -1).astype(jnp.int32)
    valid = (jnp.arange(n_entries) < total).astype(jnp.int32)
    used = region > 0
    slot = (jnp.cumsum(used) - 1) % 2
    ids = jnp.where(used, jnp.arange(N_EXPERTS), N_EXPERTS)
    first_used_from = jnp.flip(lax.cummin(jnp.flip(ids)))
    nxt = jnp.concatenate([first_used_from[1:], jnp.full((1,), N_EXPERTS)])
    nxt = jnp.where(nxt == N_EXPERTS, -1, nxt)
    return t, e, valid, slot[e].astype(jnp.int32), nxt[e].astype(jnp.int32)


def _experts(xs, sched, p, l):
    n_rows = xs.shape[0]
    tm = EXPERT_ROW_TILE
    bgu = p['b_gate_up'].reshape(DEPTH, N_EXPERTS, 1, 2 * D_FF)
    bdn = p['b_down'].reshape(DEPTH, N_EXPERTS, 1, D_MODEL)
    by_tile = lambda w: pl.BlockSpec((tm, w), lambda i, t, e, *_: (t[i], 0))
    by_expert = lambda a, b: pl.BlockSpec((1, 1, a, b), lambda i, t, e, *_: (l, e[i], 0, 0))
    in_hbm = pl.BlockSpec(memory_space=pl.ANY)
    grid_spec = pltpu.PrefetchScalarGridSpec(
        num_scalar_prefetch=5,
        grid=(sched[0].shape[0],),
        in_specs=[by_tile(ROW_W), in_hbm, by_expert(1, 2 * D_FF), in_hbm, by_expert(1, D_MODEL)],
        out_specs=by_tile(D_MODEL),
        scratch_shapes=[pltpu.VMEM((2, D_MODEL, 2 * D_FF), F32), pltpu.VMEM((2, D_FF, D_MODEL), F32),
                        pltpu.VMEM((D_MODEL, 2 * D_FF), BF16), pltpu.VMEM((D_FF, D_MODEL), BF16),
                        pltpu.SemaphoreType.DMA((2, 2))],
    )
    return pl.pallas_call(
        functools.partial(_expert_kernel, l),
        grid_spec=grid_spec,
        out_shape=jax.ShapeDtypeStruct((n_rows, D_MODEL), F32),
        compiler_params=_params("arbitrary"),
        name="experts",
    )(*sched, xs, p['w_gate_up'], bgu, p['w_down'], bdn)


def _combine_kernel(tile0, cnt_ref, lst_ref, gst_ref, tot_ref, ids_ref, lstart_ref, x_ref, ln_g_ref, ln_b_ref, ys_hbm,
                    o_ref, buf, sem):
    step = pl.program_id(0)
    i = step + tile0
    tm = x_ref.shape[0]
    slot = step % 2

    def copies_of(slot):
        return lambda lo, go, size: pltpu.make_async_copy(ys_hbm.at[pl.ds(go, size)], buf.at[slot, pl.ds(lo, size)],
                                                          sem.at[slot])

    def fetch(tile, slot):
        buf[slot, ROUTE_TILE * TOP_K:, :] = jnp.zeros((LOCAL_ROWS - ROUTE_TILE * TOP_K, D_MODEL), F32)
        _start_segments(tile, cnt_ref, lst_ref, gst_ref, copies_of(slot))

    @pl.when(step == 0)
    def _():
        fetch(i, slot)

    @pl.when(step + 1 < pl.num_programs(0))
    def _():
        fetch(i + 1, 1 - slot)

    ids = ids_ref[...]
    expert = lax.broadcasted_iota(jnp.int32, (tm, N_EXPERTS), 1)
    picks = [ids[:, k:k + 1] == expert for k in range(TOP_K)]
    picked = sum(pk.astype(F32) for pk in picks)
    m = lax.broadcasted_iota(jnp.int32, (tm, tm), 0)
    n = lax.broadcasted_iota(jnp.int32, (tm, tm), 1)
    earlier = (n < m).astype(BF16)
    rank = jnp.dot(earlier, picked.astype(BF16), preferred_element_type=F32)
    pos = lstart_ref[0] + rank
    lpos = [jnp.sum(jnp.where(picks[k], pos, 0.0), 1, keepdims=True).astype(jnp.int32) for k in range(TOP_K)]

    _wait_segments(i, tot_ref, copies_of(slot))
    moe = jnp.zeros((tm, D_MODEL), F32)
    for r0 in range(0, LOCAL_ROWS, LOCAL_BLOCK):
        col = r0 + lax.broadcasted_iota(jnp.int32, (tm, LOCAL_BLOCK), 1)
        perm = jnp.zeros((tm, LOCAL_BLOCK), F32)
        for k in range(TOP_K):
            perm = jnp.where(col == lpos[k], 1.0, perm)
        moe = moe + jnp.dot(perm.astype(BF16), buf[slot, r0:r0 + LOCAL_BLOCK, :].astype(BF16),
                            preferred_element_type=F32)
    o_ref[...] = _layer_norm(DN_ALPHA * x_ref[...] + moe, ln_g_ref[...], ln_b_ref[...])


def _combine(x1, ys, ids, tables, ln_g, ln_b, tile0, n_tiles):
    tm = ROUTE_TILE
    lstart = tables[1]
    at = lambda i, *_: (i + tile0, 0)
    grid_spec = pltpu.PrefetchScalarGridSpec(
        num_scalar_prefetch=4,
        grid=(n_tiles,),
        in_specs=[pl.BlockSpec((tm, TOP_K), at), pl.BlockSpec((1, 1, N_EXPERTS), lambda i, *_: (i + tile0, 0, 0)),
                  pl.BlockSpec((tm, D_MODEL), at), pl.BlockSpec((1, D_MODEL), lambda i, *_: (0, 0)),
                  pl.BlockSpec((1, D_MODEL), lambda i, *_: (0, 0)), pl.BlockSpec(memory_space=pl.ANY)],
        out_specs=pl.BlockSpec((tm, D_MODEL), lambda i, *_: (i, 0)),
        scratch_shapes=[pltpu.VMEM((2, LOCAL_ROWS, D_MODEL), F32), pltpu.SemaphoreType.DMA((2,))],
    )
    return pl.pallas_call(
        functools.partial(_combine_kernel, tile0),
        grid_spec=grid_spec,
        out_shape=jax.ShapeDtypeStruct((n_tiles * tm, D_MODEL), F32),
        compiler_params=_params("arbitrary"),
        name="combine_ln",
    )(*_table_args(tables), ids, lstart.astype(F32)[:, None, :], x1, ln_g.reshape(1, -1), ln_b.reshape(1, -1), ys)


def _moe(x1, xb, ids, probs, p, l, parts):
    n = x1.shape[0]
    tm = EXPERT_ROW_TILE
    seg_rows = n * TOP_K + (n // ROUTE_TILE) * N_EXPERTS * (SEG_ALIGN - 1)
    n_rows = _round_up(seg_rows + N_EXPERTS * (tm - SEG_ALIGN), tm)
    tables = _routing_tables(ids)
    xs = _dispatch(xb, ids, probs, tables, n_rows)
    ys = _experts(xs, _expert_schedule(tables[4], n_rows, tm), p, l)
    return [_combine(x1, ys, ids, tables, p['ln2_g'][l], p['ln2_b'][l], *part) for part in parts]


def _pad_time(a, tp):
    return jnp.pad(a, ((0, 0), (0, tp - a.shape[1]), (0, 0)))


def kernel(x_prompt, x_sample, state_ret, state_rwkv, state_shift, w_in, ret_gn_g, ret_gn_b, w_ret_out, rw_mu, rw_w0, rw_w_up, rw_a0, rw_a_up, rw_g_up, rw_k_k, rw_k_a, rw_r_k, rw_gn_g, rw_gn_b, rw_v0, rw_vres_down, rw_vres_up, w_rwkv_out, w_o, ln1_g, ln1_b, w_router, b_router, w_gate_up, b_gate_up, w_down, b_down, ln2_g, ln2_b):
    p = dict(w_ret_out=w_ret_out, rw_mu=rw_mu, rw_w0=rw_w0, rw_w_up=rw_w_up, rw_a0=rw_a0, rw_a_up=rw_a_up,
             rw_g_up=rw_g_up, rw_k_k=rw_k_k, rw_k_a=rw_k_a, rw_r_k=rw_r_k, rw_gn_g=rw_gn_g, rw_gn_b=rw_gn_b,
             w_rwkv_out=w_rwkv_out, w_o=w_o, ln1_g=ln1_g, ln1_b=ln1_b, w_router=w_router, b_router=b_router,
             w_gate_up=w_gate_up, b_gate_up=b_gate_up, w_down=w_down, b_down=b_down, ln2_g=ln2_g, ln2_b=ln2_b)
    bp, tp, _ = x_prompt.shape
    bs, ts, _ = x_sample.shape
    np_, ns = bp * tp, bs * ts
    pos_p = jnp.arange(tp, dtype=F32)
    ts_ret = 8
    pos_s = PAST_LEN + jnp.arange(ts_ret, dtype=F32)

    assert ns == TOKEN_TILE and np_ % TOKEN_TILE == 0
    n_prompt_tiles = np_ // TOKEN_TILE
    tokens = (x_prompt.reshape(np_, D_MODEL), x_sample.transpose(1, 0, 2).reshape(ns, D_MODEL), 0, n_prompt_tiles)
    outs = {k: [] for k in ('ret_p', 'rw_p', 'sh_p', 'sh_s')}
    sret_s = srw_s = None
    v_first = None
    u_off = RET_W
    g_off = RET_W + SHIFT_W
    for l in range(DEPTH):
        passes = EXACT_PASSES if l == 0 else 1
        z_ret = _matmul(tokens, w_in[l][:, :u_off], TOKEN_TILE, RET_W if passes == 1 else RET_W // 2, passes)
        u = _matmul(tokens, w_in[l][:, u_off:g_off], TOKEN_TILE, SHIFT_W, passes)
        gates = _matmul(tokens, w_in[l][:, g_off:], TOKEN_TILE, 2 * D_MODEL, passes)

        zr_s = _pad_time(z_ret[np_:].reshape(ts, bs, RET_W).transpose(1, 0, 2), ts_ret)
        yret, sret_p = _retention(z_ret.reshape(-1, RET_CHUNK, RET_W), pos_p, tp, None, ret_gn_g[l], ret_gn_b[l], 1,
                                  passes, n_seq=bp)
        yret_s, sret_s = _retention(zr_s, pos_s, ts, (state_ret, l, sret_s), ret_gn_g[l], ret_gn_b[l], 8, passes)
        yret = yret.reshape(-1, RET_V_W).at[np_:].set(yret_s[:, :ts].transpose(1, 0, 2).reshape(ns, RET_V_W))

        vres = None if l == 0 else (rw_v0[l - 1], rw_vres_down[l - 1], rw_vres_up[l - 1])
        r, lw, k, v, a, b, g = _rwkv_pre(u, state_shift[l], np_, tp, p, l, v_first, vres)
        if l == 0:
            v_first = v
        yrw, srw_p = _rwkv_chunks(r, lw, k, v, a, b, passes)
        yrw, srw_s = _rwkv_steps(r, lw, k, v, a, b, (state_rwkv, l, srw_s), yrw)

        x1, xb, ids, probs = _post(tokens, yrw, r, k, v, g, yret, gates, tp, p, l, passes)
        if l + 1 < DEPTH:
            x, = _moe(x1, xb, ids, probs, p, l, [(0, n_prompt_tiles + 1)])
            tokens = (x, x, n_prompt_tiles, n_prompt_tiles)
        else:
            y_p, y_s = _moe(x1, xb, ids, probs, p, l, [(0, n_prompt_tiles), (n_prompt_tiles, 1)])

        outs['ret_p'].append(sret_p)
        outs['rw_p'].append(srw_p)
        outs['sh_p'].append(u[tp - 1:np_:tp])
        outs['sh_s'].append(u[np_ + ns - bs:])

    y_prompt = y_p.reshape(bp, tp, D_MODEL)
    y_sample = y_s.reshape(ts, bs, D_MODEL).transpose(1, 0, 2)
    st = {k: jnp.stack(v) for k, v in outs.items()}
    return (y_prompt, y_sample, st['ret_p'], st['rw_p'], st['sh_p'], sret_s, srw_s, st['sh_s'])
```
